```python
import math
import jax, jax.numpy as jnp
from jax import lax
import numpy as np

D_MODEL = 1024
BATCH = 1
SEQ = 16384
DEPTH = 2

GRID_W = 64
CTX_LEN = 256
HEAD_DIM = D_MODEL // 16
A_HEADS = 4
A_QK_DIM = HEAD_DIM
A_V_DIM = 2 * HEAD_DIM
A_WIDTH = A_HEADS * A_V_DIM
A_SCALE = A_QK_DIM ** -0.5
B_HEADS = 4
B_HEAD_DIM = HEAD_DIM
B_WIDTH = B_HEADS * B_HEAD_DIM
C_GROUPS = 4
C_GROUP_DIM = HEAD_DIM
C_WIDTH = C_GROUPS * C_GROUP_DIM
MIX_WIDTH = A_WIDTH + B_WIDTH + C_WIDTH
IN_SIZES = (A_WIDTH, A_WIDTH, A_WIDTH, B_WIDTH, B_WIDTH, B_WIDTH, B_WIDTH, 4 * B_HEADS, C_WIDTH)
IN_WIDTH = sum(IN_SIZES)
ROPE_BASE = 10000.0
Q_BLOCK = 128
MLSTM_CHUNK = 64
N_EXPERTS = 32
TOP_K = 4
D_EXPERT = D_MODEL
SWIGLU_LIMIT = 7.0
SWIGLU_ALPHA = 1.702
MOE_BLOCK = 128
DEEPNORM_ALPHA = (2 * DEPTH) ** 0.25
DEEPNORM_BETA = (8 * DEPTH) ** -0.25
EPS = 1e-5

kernel_name = 'hybrid_diffattn_mlstm_fourier_moe_dit'


def layer_norm(x, w, b):
    xf = x.astype(jnp.float32)
    mu = jnp.mean(xf, axis=-1, keepdims=True)
    var = jnp.mean(jnp.square(xf - mu), axis=-1, keepdims=True)
    return ((xf - mu) * lax.rsqrt(var + EPS)).astype(x.dtype) * w + b


def rms_norm_heads(x, w):
    xf = x.astype(jnp.float32)
    y = xf * lax.rsqrt(jnp.mean(jnp.square(xf), axis=-1, keepdims=True) + EPS)
    return y.astype(x.dtype) * w.reshape(x.shape[-2:])


def modulation(vec, w_ada, b_ada):
    m = (jax.nn.silu(vec) @ w_ada + b_ada)[..., None, :]
    return jnp.split(m, 6, axis=-1)


def axial_rope_tables(n_tokens, dtype):
    rows = n_tokens // GRID_W
    row = jnp.broadcast_to(jnp.arange(rows, dtype=jnp.float32)[:, None], (rows, GRID_W)).reshape(-1)
    col = jnp.broadcast_to(jnp.arange(GRID_W, dtype=jnp.float32)[None, :], (rows, GRID_W)).reshape(-1)
    axis_dim = A_QK_DIM // 2
    inv_freq = ROPE_BASE ** (-jnp.arange(0, axis_dim, 2, dtype=jnp.float32) / axis_dim)
    ang_r = row[:, None] * inv_freq
    ang_c = col[:, None] * inv_freq
    ang = jnp.concatenate([ang_r, ang_r, ang_c, ang_c], axis=-1)
    return jnp.cos(ang).astype(dtype), jnp.sin(ang).astype(dtype)


def apply_axial_rope(x, cos, sin):
    xr = x.reshape(x.shape[:-1] + (2, 2, A_QK_DIM // 4))
    rot = jnp.stack([-xr[..., 1, :], xr[..., 0, :]], axis=-2).reshape(x.shape)
    return x * cos + rot * sin


def split_projection(p, gate_bias):
    bsz, n, _ = p.shape
    aq, ak, av, bq, bk, bv, bo, bg, cu = jnp.split(p, np.cumsum(IN_SIZES)[:-1].tolist(), axis=-1)
    pair = lambda t: t.reshape(bsz, n, A_HEADS, 2, A_QK_DIM).transpose(0, 2, 3, 1, 4)
    heads = lambda t, h: t.reshape(bsz, n, h, -1).transpose(0, 2, 1, 3)
    gates = bg.reshape(bsz, n, 4, B_HEADS).transpose(2, 0, 3, 1) + gate_bias[:, None, :, None]
    return (pair(aq), pair(ak), heads(av, A_HEADS), heads(bq, B_HEADS),
            heads(bk, B_HEADS) * (B_HEAD_DIM ** -0.5), heads(bv, B_HEADS), bo, gates, cu)


def diff_attention_block(q, k_all, v_all, lam):
    s = jnp.einsum('bhcqd,bhckd->bhcqk', q, k_all, preferred_element_type=jnp.float32) * A_SCALE
    p = jax.nn.softmax(s, axis=-1)
    a = p[:, :, 0] - lam * p[:, :, 1]
    return jnp.einsum('bhqk,bhkv->bhqv', a.astype(v_all.dtype), v_all)


def diff_head_out(o, norm_w, lam_init):
    bsz, h, n, dv = o.shape
    y = rms_norm_heads(o.transpose(0, 2, 1, 3), norm_w) * (1.0 - lam_init)
    return y.reshape(bsz, n, h * dv)


def mlstm_scan_states(g, m_loc, c_loc, n_loc, state0):
    def step(carry, xs):
        c_prev, n_prev, m_prev = carry
        g_c, m_c, cc, nc_ = xs
        m_new = jnp.maximum(g_c + m_prev, m_c)
        dec = jnp.exp(g_c + m_prev - m_new)
        inj = jnp.exp(m_c - m_new)
        c_new = dec[..., None, None] * c_prev + inj[..., None, None] * cc
        n_new = dec[..., None] * n_prev + inj[..., None] * nc_
        return (c_new, n_new, m_new), (c_prev, n_prev, m_prev)
    xs = tuple(jnp.moveaxis(t, 2, 0) for t in (g, m_loc, c_loc, n_loc))
    final, starts = lax.scan(step, state0, xs)
    return tuple(jnp.moveaxis(t, 0, 2) for t in starts), final


def mlstm_chunk_outputs(q, k, v, i_pre, b, starts):
    c0, n0, m0 = starts
    L = q.shape[3]
    lower = jnp.tril(jnp.ones((L, L), dtype=bool))
    dmat = jnp.where(lower, b[..., :, None] - b[..., None, :] + i_pre[..., None, :], -jnp.inf)
    inter = b + m0[..., None]
    m = jnp.maximum(inter, jnp.max(dmat, axis=-1))
    s = jnp.einsum('bhctk,bhcsk->bhcts', q, k) * jnp.exp(dmat - m[..., None])
    e_inter = jnp.exp(inter - m)
    num = jnp.einsum('bhcts,bhcsv->bhctv', s, v) + e_inter[..., None] * jnp.einsum('bhcvk,bhctk->bhctv', c0, q)
    den = jnp.sum(s, axis=-1) + e_inter * jnp.einsum('bhck,bhctk->bhct', n0, q)
    return num / jnp.maximum(jnp.abs(den), jnp.exp(-m))[..., None]


def mlstm_direction(q, k, v, i_pre, f_pre, state0, with_outputs):
    bsz, h, n, d = k.shape
    nc = n // MLSTM_CHUNK
    chunk = lambda t: t.reshape((bsz, h, nc, MLSTM_CHUNK) + t.shape[3:])
    k, v, i_pre = chunk(k), chunk(v), chunk(i_pre)
    b = jnp.cumsum(chunk(jax.nn.log_sigmoid(f_pre)), axis=-1)
    g = b[..., -1]
    a = g[..., None] - b + i_pre
    m_loc = jnp.max(a, axis=-1)
    w = jnp.exp(a - m_loc[..., None])
    c_loc = jnp.einsum('bhcl,bhclv,bhclk->bhcvk', w, v, k)
    n_loc = jnp.einsum('bhcl,bhclk->bhck', w, k)
    starts, final = mlstm_scan_states(g, m_loc, c_loc, n_loc, state0)
    if not with_outputs:
        return None, final
    h_out = mlstm_chunk_outputs(chunk(q), k, v, i_pre, b, starts)
    return h_out.reshape(bsz, h, n, d), final


def bidirectional_mlstm(lat, g_l, ctx, g_c, need_ctx):
    f32 = jnp.float32
    q_l, k_l, v_l = [t.astype(f32) for t in lat]
    q_c, k_c, v_c = [t.astype(f32) for t in ctx]
    g_l = g_l.astype(f32)
    g_c = g_c.astype(f32)
    bsz = q_l.shape[0]
    state0 = (jnp.zeros((bsz, B_HEADS, B_HEAD_DIM, B_HEAD_DIM), f32),
              jnp.zeros((bsz, B_HEADS, B_HEAD_DIM), f32),
              jnp.zeros((bsz, B_HEADS), f32))
    h_lat, h_ctx = [], []
    for d in range(2):
        seq = (lambda t: t) if d == 0 else (lambda t: jnp.flip(t, axis=2))
        hc, st = mlstm_direction(seq(q_c), seq(k_c), seq(v_c), seq(g_c[d]), seq(g_c[2 + d]), state0, need_ctx)
        hl, _ = mlstm_direction(seq(q_l), seq(k_l), seq(v_l), seq(g_l[d]), seq(g_l[2 + d]), st, True)
        h_lat.append(seq(hl))
        if need_ctx:
            h_ctx.append(seq(hc))
    return h_lat[0] + h_lat[1], ((h_ctx[0] + h_ctx[1]) if need_ctx else None)


def mlstm_head_out(h, o_pre, norm_w):
    bsz, hh, n, d = h.shape
    y = rms_norm_heads(h.transpose(0, 2, 1, 3), norm_w).reshape(bsz, n, hh * d)
    return (y * jax.nn.sigmoid(o_pre.astype(jnp.float32))).astype(o_pre.dtype)


def fourier_mix(u):
    bsz, n, _ = u.shape
    ug = u.reshape(bsz, n, C_GROUPS, C_GROUP_DIM).astype(jnp.float32)
    y = jnp.fft.fft2(ug, axes=(1, 3), norm='ortho').real
    return y.reshape(bsz, n, C_WIDTH).astype(u.dtype)


def token_mixer(h_lat, h_ctx, w_in, gate_bias, diff_lambda, diff_norm_w, mlstm_norm_w, w_out, lam_init, need_ctx):
    bsz, n_lat, _ = h_lat.shape
    aq_l, ak_l, av_l, bq_l, bk_l, bv_l, bo_l, g_l, cu_l = split_projection(h_lat @ w_in, gate_bias)
    aq_c, ak_c, av_c, bq_c, bk_c, bv_c, bo_c, g_c, cu_c = split_projection(h_ctx @ w_in, gate_bias)

    lq1, lk1, lq2, lk2 = diff_lambda.astype(jnp.float32)
    lam = jnp.exp(jnp.sum(lq1 * lk1)) - jnp.exp(jnp.sum(lq2 * lk2)) + lam_init
    cos, sin = axial_rope_tables(n_lat, h_lat.dtype)
    q_l = apply_axial_rope(aq_l, cos, sin)
    k_all = jnp.concatenate([ak_c, apply_axial_rope(ak_l, cos, sin)], axis=3)
    v_all = jnp.concatenate([av_c, av_l], axis=2)
    n_blocks = n_lat // Q_BLOCK
    q_blocks = jnp.moveaxis(q_l.reshape(bsz, A_HEADS, 2, n_blocks, Q_BLOCK, A_QK_DIM), 3, 0)
    o_l = lax.map(lambda qb: diff_attention_block(qb, k_all, v_all, lam), q_blocks)
    o_l = jnp.moveaxis(o_l, 0, 2).reshape(bsz, A_HEADS, n_lat, A_V_DIM)

    h_l, h_c = bidirectional_mlstm((bq_l, bk_l, bv_l), g_l, (bq_c, bk_c, bv_c), g_c, need_ctx)

    out_l = jnp.concatenate([diff_head_out(o_l, diff_norm_w, lam_init),
                             mlstm_head_out(h_l, bo_l, mlstm_norm_w),
                             fourier_mix(cu_l)], axis=-1) @ w_out
    if not need_ctx:
        return out_l, None
    o_c = diff_attention_block(aq_c, ak_c, av_c, lam)
    out_c = jnp.concatenate([diff_head_out(o_c, diff_norm_w, lam_init),
                             mlstm_head_out(h_c, bo_c, mlstm_norm_w),
                             fourier_mix(cu_c)], axis=-1) @ w_out
    return out_l, out_c


def moe_ffn(h, w_router, b_router, w_gate_up, b_gate_up, w_down, b_down):
    shp = h.shape
    t = h.reshape(-1, shp[-1])
    n_tok = t.shape[0]
    logits = (t @ w_router + b_router).astype(jnp.float32)
    top_val, top_idx = lax.top_k(logits, TOP_K)
    gates = jax.nn.softmax(top_val, axis=-1).astype(h.dtype)
    n_asg = n_tok * TOP_K
    e_flat = top_idx.reshape(n_asg)
    tok = jnp.arange(n_asg, dtype=jnp.int32) // TOP_K
    w_flat = gates.reshape(n_asg)
    counts = jnp.bincount(e_flat, length=N_EXPERTS)
    padded = (counts + MOE_BLOCK - 1) // MOE_BLOCK * MOE_BLOCK
    pad_end = jnp.cumsum(padded)
    pad_start = pad_end - padded
    start = jnp.cumsum(counts) - counts
    order = jnp.argsort(e_flat)
    e_sorted = e_flat[order]
    dest = pad_start[e_sorted] + jnp.arange(n_asg, dtype=jnp.int32) - start[e_sorted]
    n_blocks = -(-n_asg // MOE_BLOCK) + N_EXPERTS
    n_rows = n_blocks * MOE_BLOCK
    row_tok = jnp.full((n_rows,), n_tok, jnp.int32).at[dest].set(tok[order])
    row_w = jnp.zeros((n_rows,), h.dtype).at[dest].set(w_flat[order])
    block_exp = jnp.minimum(jnp.searchsorted(pad_end, jnp.arange(n_blocks, dtype=jnp.int32) * MOE_BLOCK, side='right'), N_EXPERTS - 1)
    t_pad = jnp.concatenate([t, jnp.zeros((1, shp[-1]), t.dtype)], axis=0)
    xb = t_pad[row_tok].reshape(n_blocks, MOE_BLOCK, shp[-1])

    def expert_block(args):
        xs, e = args
        gu = xs @ w_gate_up[e] + b_gate_up[e]
        gate = jnp.minimum(gu[:, :D_EXPERT], SWIGLU_LIMIT)
        up = jnp.clip(gu[:, D_EXPERT:], -SWIGLU_LIMIT, SWIGLU_LIMIT)
        act = gate * jax.nn.sigmoid(SWIGLU_ALPHA * gate) * (up + 1.0)
        return act @ w_down[e] + b_down[e]

    yb = lax.map(expert_block, (xb, block_exp)).reshape(n_rows, shp[-1])
    out = jax.ops.segment_sum(yb * row_w[:, None], row_tok, num_segments=n_tok + 1)[:n_tok]
    return out.reshape(shp)


def setup_inputs(seed: int = 0) -> dict:
    key = jax.random.key(seed)
    ks = jax.random.split(key, 22)
    f32 = jnp.float32
    nrm = lambda k, shape, s: jax.random.normal(k, shape, f32) * s
    D = D_MODEL
    gate_base = jnp.concatenate([jnp.zeros((2, B_HEADS), f32),
                                 jnp.broadcast_to(jnp.linspace(3.0, 6.0, B_HEADS, dtype=f32), (2, B_HEADS))], axis=0)
    return {
        'x': nrm(ks[0], (BATCH, SEQ, D), 1.0),
        'c': nrm(ks[1], (BATCH, D), 1.0),
        'ctx': nrm(ks[2], (BATCH, CTX_LEN, D), 1.0),
        'c_ctx': nrm(ks[3], (D,), 1.0),
        'w_ada': nrm(ks[4], (DEPTH, D, 6 * D), 0.5 * D ** -0.5),
        'b_ada': nrm(ks[5], (DEPTH, 6 * D), 0.02),
        'w_in': nrm(ks[6], (DEPTH, D, IN_WIDTH), D ** -0.5),
        'mlstm_gate_bias': gate_base + nrm(ks[7], (DEPTH, 4, B_HEADS), 0.1),
        'diff_lambda': nrm(ks[8], (DEPTH, 4, A_QK_DIM), 0.1),
        'diff_norm_w': 1.0 + nrm(ks[9], (DEPTH, A_WIDTH), 0.02),
        'mlstm_norm_w': 1.0 + nrm(ks[10], (DEPTH, B_WIDTH), 0.02),
        'w_out': nrm(ks[11], (DEPTH, MIX_WIDTH, D), DEEPNORM_BETA * MIX_WIDTH ** -0.5),
        'ln1_w': 1.0 + nrm(ks[12], (DEPTH, D), 0.02),
        'ln1_b': nrm(ks[13], (DEPTH, D), 0.02),
        'w_router': nrm(ks[14], (DEPTH, D, N_EXPERTS), D ** -0.5),
        'b_router': nrm(ks[15], (DEPTH, N_EXPERTS), 0.01),
        'w_gate_up': nrm(ks[16], (DEPTH, N_EXPERTS, D, 2 * D_EXPERT), D ** -0.5),
        'b_gate_up': nrm(ks[17], (DEPTH, N_EXPERTS, 2 * D_EXPERT), 0.02),
        'w_down': nrm(ks[18], (DEPTH, N_EXPERTS, D_EXPERT, D), DEEPNORM_BETA * D_EXPERT ** -0.5),
        'b_down': nrm(ks[19], (DEPTH, N_EXPERTS, D), 0.02),
        'ln2_w': 1.0 + nrm(ks[20], (DEPTH, D), 0.02),
        'ln2_b': nrm(ks[21], (DEPTH, D), 0.02),
    }


def reference(x, c, ctx, c_ctx, w_ada, b_ada, w_in, mlstm_gate_bias, diff_lambda, diff_norm_w,
              mlstm_norm_w, w_out, ln1_w, ln1_b, w_router, b_router, w_gate_up, b_gate_up,
              w_down, b_down, ln2_w, ln2_b):
    n_ctx = ctx.shape[1]
    for l in range(DEPTH):
        need_ctx = l < DEPTH - 1
        lam_init = 0.8 - 0.6 * math.exp(-0.3 * l)
        sh1, sc1, g1, sh2, sc2, g2 = modulation(c, w_ada[l], b_ada[l])
        csh1, csc1, cg1, csh2, csc2, cg2 = modulation(c_ctx, w_ada[l], b_ada[l])
        mix_l, mix_c = token_mixer(x * (1.0 + sc1) + sh1, ctx * (1.0 + csc1) + csh1, w_in[l],
                                   mlstm_gate_bias[l], diff_lambda[l], diff_norm_w[l], mlstm_norm_w[l],
                                   w_out[l], lam_init, need_ctx)
        x = layer_norm(DEEPNORM_ALPHA * x + g1 * mix_l, ln1_w[l], ln1_b[l])
        moe_args = (w_router[l], b_router[l], w_gate_up[l], b_gate_up[l], w_down[l], b_down[l])
        if need_ctx:
            ctx = layer_norm(DEEPNORM_ALPHA * ctx + cg1 * mix_c, ln1_w[l], ln1_b[l])
            f_in = jnp.concatenate([ctx * (1.0 + csc2) + csh2, x * (1.0 + sc2) + sh2], axis=1)
            f_out = moe_ffn(f_in, *moe_args)
            f_lat = f_out[:, n_ctx:]
            ctx = layer_norm(DEEPNORM_ALPHA * ctx + cg2 * f_out[:, :n_ctx], ln2_w[l], ln2_b[l])
        else:
            f_lat = moe_ffn(x * (1.0 + sc2) + sh2, *moe_args)
        x = layer_norm(DEEPNORM_ALPHA * x + g2 * f_lat, ln2_w[l], ln2_b[l])
    return x
```

```python
import functools
import math

import numpy as np
import jax
import jax.numpy as jnp
from jax import lax
from jax.experimental import pallas as pl
from jax.experimental.pallas import tpu as pltpu

F32 = jnp.float32
BF16 = jnp.bfloat16
HI = lax.Precision.HIGHEST

D_MODEL = 1024
GRID_W = 64
HEAD_DIM = 64
A_HEADS = 4
A_QK_DIM = HEAD_DIM
A_V_DIM = 2 * HEAD_DIM
A_WIDTH = A_HEADS * A_V_DIM
A_SCALE = A_QK_DIM ** -0.5
B_HEADS = 4
B_HEAD_DIM = HEAD_DIM
B_WIDTH = B_HEADS * B_HEAD_DIM
C_GROUPS = 4
C_GROUP_DIM = HEAD_DIM
C_WIDTH = C_GROUPS * C_GROUP_DIM
IN_SIZES = (A_WIDTH, A_WIDTH, A_WIDTH, B_WIDTH, B_WIDTH, B_WIDTH, B_WIDTH, 4 * B_HEADS, C_WIDTH)
ROPE_BASE = 10000.0
MLSTM_CHUNK = 64
N_EXPERTS = 32
TOP_K = 4
D_EXPERT = D_MODEL
SWIGLU_LIMIT = 7.0
SWIGLU_ALPHA = 1.702
EPS = 1e-5

LANES = 128
VMEM_LIMIT = 56 * 1024 * 1024
TOK_TILE = 256
ATT_TQ = 256
ATT_TK = 256
MOE_BM = 256

_OQ, _OQR, _OK, _OKR, _OV = 0, 512, 1024, 1536, 2048
_OMQ, _OMK, _OMV, _OMO, _OG, _OCU = 2560, 2816, 3072, 3328, 3584, 3712
IN_EXT = 3968


def _cparams(sem, vmem=VMEM_LIMIT):
    return pltpu.CompilerParams(dimension_semantics=sem, vmem_limit_bytes=vmem)


def _ada_kernel(v_ref, w_ref, b_ref, o_ref):
    v = v_ref[...]
    s = v * jax.nn.sigmoid(v)
    o_ref[...] = jnp.dot(s, w_ref[...], precision=HI, preferred_element_type=F32) + b_ref[...]


def _ada(vecs, w_ada, b_ada):
    depth, d, d6 = w_ada.shape
    bn = 1024
    return pl.pallas_call(
        _ada_kernel,
        out_shape=jax.ShapeDtypeStruct((depth, 8, d6), F32),
        grid=(depth, d6 // bn),
        in_specs=[pl.BlockSpec((8, d), lambda l, j: (0, 0)),
                  pl.BlockSpec((None, d, bn), lambda l, j: (l, 0, j)),
                  pl.BlockSpec((None, 1, bn), lambda l, j: (l, 0, j))],
        out_specs=pl.BlockSpec((None, 8, bn), lambda l, j: (l, 0, j)),
        compiler_params=_cparams(("arbitrary", "arbitrary")),
        name="ada_modulation",
    )(vecs, w_ada, b_ada.reshape(depth, 1, d6))


def _mod_rows(mod_ref, k, is_ctx):
    lat = mod_ref[0:1, k * D_MODEL:(k + 1) * D_MODEL]
    ctx = mod_ref[1:2, k * D_MODEL:(k + 1) * D_MODEL]
    return jnp.where(is_ctx, ctx, lat)


def _inproj_kernel(x_ref, mod_ref, w_ref, cos_ref, sin_ref, gb_ref,
                   q_ref, k_ref, v_ref, mq_ref, mk_ref, mv_ref, mo_ref, g_ref, cu_ref, *, n_ctx_tiles):
    is_ctx = pl.program_id(0) < n_ctx_tiles
    sh = _mod_rows(mod_ref, 0, is_ctx)
    sc = _mod_rows(mod_ref, 1, is_ctx)
    hm = (x_ref[...] * (1.0 + sc) + sh).astype(BF16)

    def proj(off, width):
        return jnp.dot(hm, w_ref[:, off:off + width], preferred_element_type=F32)

    cos = cos_ref[...]
    sin = sin_ref[...]
    for j in range(A_HEADS):
        o = j * LANES
        q = proj(_OQ + o, LANES) * cos + proj(_OQR + o, LANES) * sin
        q_ref[:, o:o + LANES] = (q * A_SCALE).astype(BF16)
        k = proj(_OK + o, LANES) * cos + proj(_OKR + o, LANES) * sin
        k_ref[:, o:o + LANES] = k.astype(BF16)
    v_ref[...] = proj(_OV, A_WIDTH).astype(BF16)
    mq_ref[...] = proj(_OMQ, B_WIDTH)
    mk_ref[...] = proj(_OMK, B_WIDTH) * (B_HEAD_DIM ** -0.5)
    mv_ref[...] = proj(_OMV, B_WIDTH)
    mo_ref[...] = proj(_OMO, B_WIDTH)
    g_ref[...] = proj(_OG, LANES) + gb_ref[...]
    cu_ref[...] = proj(_OCU, C_WIDTH)


def _rot_cols(w):
    d, width = w.shape
    wr = w.reshape(d, width // 64, 2, 2, 16)
    rot = jnp.stack([-wr[:, :, :, 1, :], wr[:, :, :, 0, :]], axis=3)
    return rot.reshape(d, width)


def _prep_w_in(w_in):
    offs = np.cumsum((0,) + IN_SIZES)
    aq, ak, av, bq, bk, bv, bo, bg, cu = [w_in[:, offs[i]:offs[i + 1]] for i in range(9)]
    bg = jnp.pad(bg, ((0, 0), (0, LANES - bg.shape[1])))
    w = jnp.concatenate([aq, _rot_cols(aq), ak, _rot_cols(ak), av, bq, bk, bv, bo, bg, cu], axis=1)
    return w.astype(BF16)


def _rope_tables(n_lat, n_ctx):
    rows = n_lat // GRID_W
    row = jnp.broadcast_to(jnp.arange(rows, dtype=F32)[:, None], (rows, GRID_W)).reshape(-1)
    col = jnp.broadcast_to(jnp.arange(GRID_W, dtype=F32)[None, :], (rows, GRID_W)).reshape(-1)
    axis_dim = A_QK_DIM // 2
    inv_freq = ROPE_BASE ** (-jnp.arange(0, axis_dim, 2, dtype=F32) / axis_dim)
    ang_r = row[:, None] * inv_freq
    ang_c = col[:, None] * inv_freq
    ang = jnp.concatenate([ang_r, ang_r, ang_c, ang_c], axis=-1)
    cos = jnp.concatenate([jnp.ones((n_ctx, A_QK_DIM), F32), jnp.cos(ang)], axis=0)
    sin = jnp.concatenate([jnp.zeros((n_ctx, A_QK_DIM), F32), jnp.sin(ang)], axis=0)
    return jnp.tile(cos, (1, 2)), jnp.tile(sin, (1, 2))


def _inproj(xt, mod, w_ext, cos, sin, gate_bias, n_ctx):
    t = xt.shape[0]
    tm = TOK_TILE
    gb = jnp.pad(gate_bias.reshape(1, -1), ((0, 0), (0, LANES - 4 * B_HEADS)))
    row = lambda w: pl.BlockSpec((tm, w), lambda i: (i, 0))
    full = lambda a: pl.BlockSpec(a.shape, lambda i: (0,) * a.ndim)
    outs = [(A_WIDTH, BF16)] * 3 + [(B_WIDTH, F32)] * 4 + [(LANES, F32), (C_WIDTH, F32)]
    return pl.pallas_call(
        functools.partial(_inproj_kernel, n_ctx_tiles=n_ctx // tm),
        out_shape=[jax.ShapeDtypeStruct((t, w), dt) for w, dt in outs],
        grid=(t // tm,),
        in_specs=[row(D_MODEL), full(mod), full(w_ext), row(LANES), row(LANES), full(gb)],
        out_specs=[row(w) for w, _ in outs],
        compiler_params=_cparams(("arbitrary",)),
        name="in_projection",
    )(xt, mod, w_ext, cos, sin, gb)


def _attn_kernel(lam_ref, q_ref, k_ref, v_ref, nw_ref, o_ref, *, n_kv, tk, post_scale):
    q = q_ref[...]
    tq = q.shape[0]
    lane = lax.broadcasted_iota(jnp.int32, q.shape, 1)
    zero = jnp.zeros_like(q)
    qs = (jnp.where(lane < A_QK_DIM, q, zero), jnp.where(lane >= A_QK_DIM, q, zero))

    def body(j, carry):
        start = pl.multiple_of(j * tk, tk)
        kb = k_ref[pl.ds(start, tk), :]
        vb = v_ref[pl.ds(start, tk), :]
        new = []
        for c in range(2):
            m, l, acc = carry[c]
            s = lax.dot_general(qs[c], kb, (((1,), (1,)), ((), ())), preferred_element_type=F32)
            m_new = jnp.maximum(m, jnp.max(s, axis=-1, keepdims=True))
            alpha = jnp.exp(m - m_new)
            p = jnp.exp(s - m_new)
            l_new = alpha * l + jnp.sum(p, axis=-1, keepdims=True)
            acc_new = alpha * acc + jnp.dot(p.astype(BF16), vb, preferred_element_type=F32)
            new.append((m_new, l_new, acc_new))
        return tuple(new)

    init = tuple((jnp.full((tq, 1), -jnp.inf, F32), jnp.zeros((tq, 1), F32), jnp.zeros((tq, A_V_DIM), F32))
                 for _ in range(2))
    (_, l0, a0), (_, l1, a1) = lax.fori_loop(0, n_kv, body, init)
    o = a0 / l0 - lam_ref[0] * (a1 / l1)
    y = o * lax.rsqrt(jnp.mean(o * o, axis=-1, keepdims=True) + EPS)
    o_ref[...] = y * nw_ref[...] * post_scale


def _attention(lam, q, k, v, norm_w, q_row0, n_q, n_keys, post_scale):
    tq = min(ATT_TQ, n_q)
    tk = min(ATT_TK, n_keys)
    q_blk0 = q_row0 // tq
    return pl.pallas_call(
        functools.partial(_attn_kernel, n_kv=n_keys // tk, tk=tk, post_scale=post_scale),
        out_shape=jax.ShapeDtypeStruct((n_q, A_WIDTH), F32),
        grid=(A_HEADS, n_q // tq),
        in_specs=[pl.BlockSpec(memory_space=pltpu.SMEM),
                  pl.BlockSpec((tq, LANES), lambda h, i: (i + q_blk0, h)),
                  pl.BlockSpec((n_keys, LANES), lambda h, i: (0, h)),
                  pl.BlockSpec((n_keys, LANES), lambda h, i: (0, h)),
                  pl.BlockSpec((1, LANES), lambda h, i: (0, h))],
        out_specs=pl.BlockSpec((tq, LANES), lambda h, i: (i, h)),
        compiler_params=_cparams(("arbitrary", "arbitrary")),
        name="diff_attention",
    )(lam, q, k, v, norm_w.reshape(1, A_WIDTH))


def _mlstm_chunk(d, q_t, k_t, v_t, g_all, s_ref, m_ref, tri_f, tri_mask, lane_half, row_half, blockmask, aug):
    L = MLSTM_CHUNK
    ls = jax.nn.log_sigmoid(g_all)
    b_all = jnp.dot(tri_f, ls, precision=HI, preferred_element_type=F32)
    gsum = jnp.sum(ls, axis=0, keepdims=True)
    g_t = g_all.T
    b_t = b_all.T
    outs = []
    for p in range(2):
        sl = slice(p * LANES, (p + 1) * LANES)
        qp, kp, vp = q_t[:, sl], k_t[:, sl], v_t[:, sl]
        v_aug = jnp.concatenate([vp, aug], axis=1)
        state = s_ref[d, p]
        q2 = jnp.dot(qp, state, precision=HI, preferred_element_type=F32)
        head_out, w_cols, decs, injs = [], [], [], []
        for hh in range(2):
            h = 2 * p + hh
            il = 4 * d + h
            fl = 8 + 4 * d + h
            b_col = b_all[:, fl:fl + 1]
            i_col = g_all[:, il:il + 1]
            r_row = g_t[il:il + 1, :] - b_t[fl:fl + 1, :]
            gs = gsum[:, fl:fl + 1]
            m_prev = m_ref[il:il + 1, 0:1]
            a_col = gs - b_col + i_col
            m_loc = jnp.max(a_col, axis=0, keepdims=True)
            w_cols.append(jnp.exp(a_col - m_loc))
            dmat = jnp.where(tri_mask, b_col + r_row, -jnp.inf)
            inter = b_col + m_prev
            m_col = jnp.maximum(inter, jnp.max(dmat, axis=-1, keepdims=True))
            dm = jnp.exp(dmat - m_col)
            qm = jnp.where(lane_half == hh, qp, 0.0)
            sqk = lax.dot_general(qm, kp, (((1,), (1,)), ((), ())), precision=HI, preferred_element_type=F32)
            r = jnp.dot(sqk * dm, v_aug, precision=HI, preferred_element_type=F32)
            e_int = jnp.exp(inter - m_col)
            num = r[:, :LANES] + e_int * q2[:, :LANES]
            den = r[:, LANES:LANES + 1] + e_int * q2[:, LANES + hh:LANES + hh + 1]
            head_out.append(num / jnp.maximum(jnp.abs(den), jnp.exp(-m_col)))
            m_new = jnp.maximum(gs + m_prev, m_loc)
            decs.append(jnp.exp(gs + m_prev - m_new))
            injs.append(jnp.exp(m_loc - m_new))
            m_ref[il:il + 1, :] = jnp.broadcast_to(m_new, (1, LANES))
        outs.append(jnp.where(lane_half == 0, head_out[0], head_out[1]))
        kw = kp * jnp.where(lane_half == 0, w_cols[0], w_cols[1])
        upd = lax.dot_general(kw, v_aug, (((0,), (0,)), ((), ())), precision=HI, preferred_element_type=F32)
        dec = jnp.where(row_half == 0, decs[0], decs[1])
        inj = jnp.where(row_half == 0, injs[0], injs[1])
        s_ref[d, p] = dec * state + inj * jnp.where(blockmask, upd, 0.0)
    return jnp.concatenate(outs, axis=1)


def _mlstm_kernel(qf, kf, vf, gf, qb, kb, vb, gb, of_ref, ob_ref, s_ref, m_ref, *, chunks):
    L = MLSTM_CHUNK

    @pl.when(pl.program_id(0) == 0)
    def _():
        s_ref[...] = jnp.zeros_like(s_ref)
        m_ref[...] = jnp.zeros_like(m_ref)

    r = lax.broadcasted_iota(jnp.int32, (L, L), 0)
    c = lax.broadcasted_iota(jnp.int32, (L, L), 1)
    masks = (r >= c, r <= c)
    lane_half = lax.broadcasted_iota(jnp.int32, (L, LANES), 1) // B_HEAD_DIM
    row_half = lax.broadcasted_iota(jnp.int32, (LANES, 1), 0) // B_HEAD_DIM
    rr = lax.broadcasted_iota(jnp.int32, (LANES, 2 * LANES), 0) // B_HEAD_DIM
    cc = lax.broadcasted_iota(jnp.int32, (LANES, 2 * LANES), 1)
    blockmask = jnp.where(cc < LANES, cc // B_HEAD_DIM, cc - LANES) == rr
    aug = jnp.where(lax.broadcasted_iota(jnp.int32, (L, LANES), 1) < 2, 1.0, 0.0).astype(F32)
    io = ((qf, kf, vf, gf, of_ref), (qb, kb, vb, gb, ob_ref))

    def body(j, carry):
        for d in range(2):
            q_r, k_r, v_r, g_r, o_r = io[d]
            ci = j if d == 0 else chunks - 1 - j
            sl = pl.ds(pl.multiple_of(ci * L, L), L)
            o_r[sl, :] = _mlstm_chunk(d, q_r[sl, :], k_r[sl, :], v_r[sl, :], g_r[sl, :], s_ref, m_ref,
                                      masks[d].astype(F32), masks[d], lane_half, row_half, blockmask, aug)
        return carry

    lax.fori_loop(0, chunks, body, 0)


def _mlstm(mq, mk, mv, g, n_ctx):
    t = mq.shape[0]
    tm = TOK_TILE
    n_tiles = t // tm
    n_ctx_tiles = n_ctx // tm
    fwd = lambda i: (i, 0)
    bwd = lambda i: (jnp.where(i < n_ctx_tiles, n_ctx_tiles - 1 - i, n_tiles - 1 - (i - n_ctx_tiles)), 0)
    spec = lambda w, im: pl.BlockSpec((tm, w), im)
    ins = [spec(B_WIDTH, fwd)] * 3 + [spec(LANES, fwd)] + [spec(B_WIDTH, bwd)] * 3 + [spec(LANES, bwd)]
    return pl.pallas_call(
        functools.partial(_mlstm_kernel, chunks=tm // MLSTM_CHUNK),
        out_shape=[jax.ShapeDtypeStruct((t, B_WIDTH), F32)] * 2,
        grid=(n_tiles,),
        in_specs=ins,
        out_specs=[spec(B_WIDTH, fwd), spec(B_WIDTH, bwd)],
        scratch_shapes=[pltpu.VMEM((2, 2, LANES, 2 * LANES), F32), pltpu.VMEM((8, LANES), F32)],
        compiler_params=_cparams(("arbitrary",)),
        name="mlstm_scan",
    )(mq, mk, mv, g, mq, mk, mv, g)


def _dft_cs(n):
    j = np.arange(n, dtype=np.int64)
    ang = 2.0 * np.pi * ((j[:, None] * j[None, :]) % n).astype(np.float64) / n
    s = 1.0 / math.sqrt(n)
    return np.cos(ang) * s, np.sin(ang) * s


def _chan_mats():
    c, s = _dft_cs(C_GROUP_DIM)
    eye = np.eye(C_GROUPS)
    return np.concatenate([np.kron(eye, c), np.kron(eye, s)], axis=0).astype(np.float32)


def _fft1_kernel(fa_ref, u_ref, z_ref):
    z_ref[...] = jnp.dot(fa_ref[...], u_ref[...], precision=HI, preferred_element_type=F32)


def _fft2_kernel(z_ref, twc_ref, tws_ref, fb_ref, ch_ref, o_ref, *, kb):
    n2 = fb_ref.shape[0] // 2
    for j in range(kb):
        zr = z_ref[0, j]
        zi = z_ref[1, j]
        tc = twc_ref[j]
        ts = tws_ref[j]
        st = jnp.concatenate([zr * tc + zi * ts, zi * tc - zr * ts], axis=0)
        a = jnp.dot(fb_ref[...], st, precision=HI, preferred_element_type=F32)
        ari = jnp.concatenate([a[:n2], a[n2:]], axis=1)
        o_ref[:, j * C_WIDTH:(j + 1) * C_WIDTH] = jnp.dot(ari, ch_ref[...], precision=HI,
                                                         preferred_element_type=F32)


def _fourier_lat(u):
    n = u.shape[0]
    n1 = int(round(math.sqrt(n)))
    assert n1 * n1 == n
    n2 = n1
    c, s = _dft_cs(n1)
    fa = jnp.asarray(np.concatenate([c, -s], axis=0).astype(np.float32))
    fb = jnp.asarray(np.block([[c, s], [-s, c]]).astype(np.float32))
    k1 = np.arange(n1, dtype=np.int64)
    ang = 2.0 * np.pi * ((k1[:, None] * k1[None, :]) % n).astype(np.float64) / n
    twc = jnp.asarray(np.cos(ang).astype(np.float32)).reshape(n1, n2, 1)
    tws = jnp.asarray(np.sin(ang).astype(np.float32)).reshape(n1, n2, 1)
    ch = jnp.asarray(_chan_mats())
    cols = n2 * C_WIDTH
    tb = min(2048, cols)
    z = pl.pallas_call(
        _fft1_kernel,
        out_shape=jax.ShapeDtypeStruct((2 * n1, cols), F32),
        grid=(cols // tb,),
        in_specs=[pl.BlockSpec((2 * n1, n1), lambda i: (0, 0)), pl.BlockSpec((n1, tb), lambda i: (0, i))],
        out_specs=pl.BlockSpec((2 * n1, tb), lambda i: (0, i)),
        compiler_params=_cparams(("arbitrary",)),
        name="fourier_stage1",
    )(fa, u.reshape(n1, cols))
    kb = 8
    out = pl.pallas_call(
        functools.partial(_fft2_kernel, kb=kb),
        out_shape=jax.ShapeDtypeStruct((n2, n1 * C_WIDTH), F32),
        grid=(n1 // kb,),
        in_specs=[pl.BlockSpec((2, kb, n2, C_WIDTH), lambda i: (0, i, 0, 0)),
                  pl.BlockSpec((kb, n2, 1), lambda i: (i, 0, 0)),
                  pl.BlockSpec((kb, n2, 1), lambda i: (i, 0, 0)),
                  pl.BlockSpec((2 * n2, 2 * n2), lambda i: (0, 0)),
                  pl.BlockSpec((2 * C_WIDTH, C_WIDTH), lambda i: (0, 0))],
        out_specs=pl.BlockSpec((n2, kb * C_WIDTH), lambda i: (0, i)),
        compiler_params=_cparams(("arbitrary",)),
        name="fourier_stage2",
    )(z.reshape(2, n1, n2, C_WIDTH), twc, tws, fb, ch)
    return out.reshape(n, C_WIDTH)


def _fft_small_kernel(fa_ref, u_ref, ch_ref, o_ref):
    n = u_ref.shape[0]
    a = jnp.dot(fa_ref[...], u_ref[...], precision=HI, preferred_element_type=F32)
    ari = jnp.concatenate([a[:n], a[n:]], axis=1)
    o_ref[...] = jnp.dot(ari, ch_ref[...], precision=HI, preferred_element_type=F32)


def _fourier_small(u):
    n = u.shape[0]
    c, s = _dft_cs(n)
    fa = jnp.asarray(np.concatenate([c, -s], axis=0).astype(np.float32))
    return pl.pallas_call(
        _fft_small_kernel,
        out_shape=jax.ShapeDtypeStruct((n, C_WIDTH), F32),
        compiler_params=_cparams(None),
        name="fourier_small",
    )(fa, u, jnp.asarray(_chan_mats()))


def _layer_norm(z, w, b):
    mu = jnp.mean(z, axis=-1, keepdims=True)
    zc = z - mu
    var = jnp.mean(zc * zc, axis=-1, keepdims=True)
    return zc * lax.rsqrt(var + EPS) * w + b


def _outproj_kernel(al_ref, ac_ref, fl_ref, fc_ref, hf_ref, hb_ref, bo_ref, x_ref, mod_ref, mnw_ref, seg_ref,
                    w_ref, lw_ref, lb_ref, o_ref, *, n_ctx_tiles, alpha):
    is_ctx = pl.program_id(0) < n_ctx_tiles
    attn = jnp.where(is_ctx, ac_ref[...], al_ref[...])
    four = jnp.where(is_ctx, fc_ref[...], fl_ref[...])
    h = hf_ref[...] + hb_ref[...]
    ms = jnp.dot(h * h, seg_ref[...], precision=HI, preferred_element_type=F32)
    y = h * lax.rsqrt(ms + EPS) * mnw_ref[...] * jax.nn.sigmoid(bo_ref[...])
    mix = (jnp.dot(attn.astype(BF16), w_ref[0:A_WIDTH, :], preferred_element_type=F32)
           + jnp.dot(y.astype(BF16), w_ref[A_WIDTH:A_WIDTH + B_WIDTH, :], preferred_element_type=F32)
           + jnp.dot(four.astype(BF16), w_ref[A_WIDTH + B_WIDTH:, :], preferred_element_type=F32))
    g1 = _mod_rows(mod_ref, 2, is_ctx)
    o_ref[...] = _layer_norm(alpha * x_ref[...] + g1 * mix, lw_ref[...], lb_ref[...])


def _outproj(attn_l, attn_c, four_l, four_c, hf, hb, bo, xt, mod, mnw, w_out_bf, ln_w, ln_b, n_ctx, with_ctx, alpha):
    tm = TOK_TILE
    nct = n_ctx // tm
    n_lat_tiles = attn_l.shape[0] // tm
    if with_ctx:
        n_tiles, uoff, n_ctx_tiles = n_lat_tiles + nct, 0, nct
        lat = lambda i: (jnp.maximum(i - nct, 0), 0)
    else:
        n_tiles, uoff, n_ctx_tiles = n_lat_tiles, nct, 0
        lat = lambda i: (i, 0)
    uni = lambda i: (i + uoff, 0)
    ctxm = lambda i: (jnp.minimum(i, nct - 1), 0)
    seg = jnp.asarray(np.kron(np.eye(B_HEADS), np.full((B_HEAD_DIM, B_HEAD_DIM), 1.0 / B_HEAD_DIM)).astype(np.float32))
    full = lambda a: pl.BlockSpec(a.shape, lambda i: (0,) * a.ndim)
    mnw2, lw2, lb2 = mnw.reshape(1, -1), ln_w.reshape(1, -1), ln_b.reshape(1, -1)
    return pl.pallas_call(
        functools.partial(_outproj_kernel, n_ctx_tiles=n_ctx_tiles, alpha=alpha),
        out_shape=jax.ShapeDtypeStruct((n_tiles * tm, D_MODEL), F32),
        grid=(n_tiles,),
        in_specs=[pl.BlockSpec((tm, A_WIDTH), lat), pl.BlockSpec((tm, A_WIDTH), ctxm),
                  pl.BlockSpec((tm, C_WIDTH), lat), pl.BlockSpec((tm, C_WIDTH), ctxm),
                  pl.BlockSpec((tm, B_WIDTH), uni), pl.BlockSpec((tm, B_WIDTH), uni),
                  pl.BlockSpec((tm, B_WIDTH), uni), pl.BlockSpec((tm, D_MODEL), uni),
                  full(mod), full(mnw2), full(seg), full(w_out_bf), full(lw2), full(lb2)],
        out_specs=pl.BlockSpec((tm, D_MODEL), lambda i: (i, 0)),
        compiler_params=_cparams(("arbitrary",)),
        name="out_projection",
    )(attn_l, attn_c, four_l, four_c, hf, hb, bo, xt, mod, mnw2, seg, w_out_bf, lw2, lb2)


def _router_kernel(x_ref, mod_ref, wr_ref, br_ref, idx_ref, gate_ref, rank_ref, cnt_ref, run_ref, *, n_ctx_tiles):
    i = pl.program_id(0)

    @pl.when(i == 0)
    def _():
        run_ref[...] = jnp.zeros_like(run_ref)

    is_ctx = i < n_ctx_tiles
    f_in = x_ref[...] * (1.0 + _mod_rows(mod_ref, 4, is_ctx)) + _mod_rows(mod_ref, 3, is_ctx)
    logits = jnp.dot(f_in, wr_ref[...], precision=HI, preferred_element_type=F32) + br_ref[...]
    tm = logits.shape[0]
    eidx = lax.broadcasted_iota(jnp.int32, logits.shape, 1).astype(F32)
    lane = lax.broadcasted_iota(jnp.int32, (tm, LANES), 1)
    work = logits
    vals, sels = [], []
    for _ in range(TOP_K):
        mx = jnp.max(work, axis=-1, keepdims=True)
        sel = jnp.min(jnp.where(work == mx, eidx, float(N_EXPERTS)), axis=-1, keepdims=True)
        vals.append(mx)
        sels.append(sel)
        work = jnp.where(eidx == sel, -jnp.inf, work)
    es = [jnp.exp(v - vals[0]) for v in vals]
    tot = es[0] + es[1] + es[2] + es[3]
    onehots = [(eidx == s).astype(F32) for s in sels]
    oh_all = onehots[0] + onehots[1] + onehots[2] + onehots[3]
    r = lax.broadcasted_iota(jnp.int32, (tm, tm), 0)
    c = lax.broadcasted_iota(jnp.int32, (tm, tm), 1)
    before = jnp.dot((r > c).astype(BF16), oh_all.astype(BF16), preferred_element_type=F32) + run_ref[0:1, :]
    idx_o = jnp.zeros((tm, LANES), jnp.int32)
    gate_o = jnp.zeros((tm, LANES), F32)
    rank_o = jnp.zeros((tm, LANES), jnp.int32)
    for k in range(TOP_K):
        rank = jnp.sum(onehots[k] * before, axis=-1, keepdims=True).astype(jnp.int32)
        idx_o = jnp.where(lane == k, sels[k].astype(jnp.int32), idx_o)
        gate_o = jnp.where(lane == k, es[k] / tot, gate_o)
        rank_o = jnp.where(lane == k, rank, rank_o)
    idx_ref[...] = idx_o
    gate_ref[...] = gate_o
    rank_ref[...] = rank_o
    run_new = run_ref[0:1, :] + jnp.sum(oh_all, axis=0, keepdims=True)
    run_ref[...] = jnp.broadcast_to(run_new, run_ref.shape)
    cnt_ref[...] = jnp.broadcast_to(run_new, cnt_ref.shape).astype(jnp.int32)


def _router(xs, mod, w_router, b_router, n_ctx_tiles):
    t = xs.shape[0]
    tm = TOK_TILE
    br = b_router.reshape(1, -1)
    full = lambda a: pl.BlockSpec(a.shape, lambda i: (0,) * a.ndim)
    row = lambda w: pl.BlockSpec((tm, w), lambda i: (i, 0))
    return pl.pallas_call(
        functools.partial(_router_kernel, n_ctx_tiles=n_ctx_tiles),
        out_shape=[jax.ShapeDtypeStruct((t, LANES), jnp.int32), jax.ShapeDtypeStruct((t, LANES), F32),
                   jax.ShapeDtypeStruct((t, LANES), jnp.int32), jax.ShapeDtypeStruct((8, N_EXPERTS), jnp.int32)],
        grid=(t // tm,),
        in_specs=[row(D_MODEL), full(mod), full(w_router), full(br)],
        out_specs=[row(LANES), row(LANES), row(LANES), pl.BlockSpec((8, N_EXPERTS), lambda i: (0, 0))],
        scratch_shapes=[pltpu.VMEM((8, N_EXPERTS), F32)],
        compiler_params=_cparams(("arbitrary",)),
        name="moe_router",
    )(xs, mod, w_router, br)


ROW_CHUNKS = D_MODEL // LANES


def _to_chunk_rows(ref, val):
    m = val.shape[0]
    for c in range(ROW_CHUNKS):
        ref[pl.ds(c, m, stride=ROW_CHUNKS), :] = val[:, c * LANES:(c + 1) * LANES]


def _from_chunk_rows(ref, m):
    return jnp.concatenate([ref[pl.ds(c, m, stride=ROW_CHUNKS), :] for c in range(ROW_CHUNKS)], axis=1)


def _dispatch_kernel(pstart_ref, pend_ref, dest_ref, x_ref, mod_ref, xb_ref, fbuf, zbuf, sem, zsem, *, n_ctx_tiles, bm):
    i = pl.program_id(0)
    tm = x_ref.shape[0]

    @pl.when(i == 0)
    def _():
        zbuf[...] = jnp.zeros_like(zbuf)

        def zcopy(e):
            row0 = pl.multiple_of((pend_ref[e] - bm) * ROW_CHUNKS, bm * ROW_CHUNKS)
            return pltpu.make_async_copy(zbuf, xb_ref.at[pl.ds(row0, bm * ROW_CHUNKS)], zsem)

        def zstart(e, c):
            @pl.when(pend_ref[e] > pstart_ref[e])
            def _():
                zcopy(e).start()
            return c

        def zwait(e, c):
            @pl.when(pend_ref[e] > pstart_ref[e])
            def _():
                zcopy(e).wait()
            return c

        lax.fori_loop(0, N_EXPERTS, zstart, 0)
        lax.fori_loop(0, N_EXPERTS, zwait, 0)

        def tcopy(b):
            row0 = pl.multiple_of(b * (bm * ROW_CHUNKS), bm * ROW_CHUNKS)
            return pltpu.make_async_copy(zbuf, xb_ref.at[pl.ds(row0, bm * ROW_CHUNKS)], zsem)

        def tstart(b, c):
            tcopy(b).start()
            return c

        def twait(b, c):
            tcopy(b).wait()
            return c

        tail0 = pend_ref[N_EXPERTS - 1] // bm
        n_blocks = xb_ref.shape[0] // (bm * ROW_CHUNKS)
        lax.fori_loop(tail0, n_blocks, tstart, 0)
        lax.fori_loop(tail0, n_blocks, twait, 0)

    is_ctx = i < n_ctx_tiles
    f = x_ref[...] * (1.0 + _mod_rows(mod_ref, 4, is_ctx)) + _mod_rows(mod_ref, 3, is_ctx)
    _to_chunk_rows(fbuf, f)

    def row_copy(a):
        src = pl.multiple_of((a // TOP_K) * ROW_CHUNKS, ROW_CHUNKS)
        dst = pl.multiple_of(dest_ref[a] * ROW_CHUNKS, ROW_CHUNKS)
        return pltpu.make_async_copy(fbuf.at[pl.ds(src, ROW_CHUNKS)], xb_ref.at[pl.ds(dst, ROW_CHUNKS)], sem)

    def start(a, c):
        row_copy(a).start()
        return c

    def wait(a, c):
        row_copy(a).wait()
        return c

    lax.fori_loop(0, tm * TOP_K, start, 0)
    lax.fori_loop(0, tm * TOP_K, wait, 0)


def _dispatch(pad_start, pad_end, dest_flat, xs, mod, n_rows, n_ctx_tiles):
    t = xs.shape[0]
    tm = TOK_TILE
    gs = pltpu.PrefetchScalarGridSpec(
        num_scalar_prefetch=2,
        grid=(t // tm,),
        in_specs=[pl.BlockSpec((tm * TOP_K,), lambda i, *_: (i,), memory_space=pltpu.SMEM),
                  pl.BlockSpec((tm, D_MODEL), lambda i, *_: (i, 0)),
                  pl.BlockSpec(mod.shape, lambda i, *_: (0, 0))],
        out_specs=pl.BlockSpec(memory_space=pl.ANY),
        scratch_shapes=[pltpu.VMEM((tm * ROW_CHUNKS, LANES), F32), pltpu.VMEM((MOE_BM * ROW_CHUNKS, LANES), F32),
                        pltpu.SemaphoreType.DMA, pltpu.SemaphoreType.DMA],
    )
    return pl.pallas_call(
        functools.partial(_dispatch_kernel, n_ctx_tiles=n_ctx_tiles, bm=MOE_BM),
        out_shape=jax.ShapeDtypeStruct((n_rows * ROW_CHUNKS, LANES), F32),
        grid_spec=gs,
        compiler_params=_cparams(("arbitrary",)),
        name="moe_dispatch",
    )(pad_start, pad_end, dest_flat, xs, mod)


def _expert_kernel(bexp_ref, bvalid_ref, x_ref, wgu_ref, bgu_ref, wd_ref, bd_ref, y_ref, wgu_bf, wd_bf):
    i = pl.program_id(0)
    e = bexp_ref[i]
    prev = bexp_ref[jnp.maximum(i - 1, 0)]

    @pl.when(jnp.logical_or(i == 0, e != prev))
    def _():
        wgu_bf[...] = wgu_ref[...].astype(BF16)
        wd_bf[...] = wd_ref[...].astype(BF16)

    @pl.when(bvalid_ref[i] > 0)
    def _():
        bm = y_ref.shape[0] // ROW_CHUNKS
        x = _from_chunk_rows(x_ref, bm).astype(BF16)
        gu = jnp.dot(x, wgu_bf[...], preferred_element_type=F32) + bgu_ref[...]
        gate = jnp.minimum(gu[:, :D_EXPERT], SWIGLU_LIMIT)
        up = jnp.clip(gu[:, D_EXPERT:], -SWIGLU_LIMIT, SWIGLU_LIMIT)
        act = gate * jax.nn.sigmoid(SWIGLU_ALPHA * gate) * (up + 1.0)
        _to_chunk_rows(y_ref, jnp.dot(act.astype(BF16), wd_bf[...], preferred_element_type=F32) + bd_ref[...])

    @pl.when(bvalid_ref[i] == 0)
    def _():
        y_ref[...] = jnp.zeros_like(y_ref)


def _experts(block_exp, block_valid, xb, layer, w_gate_up, b_gate_up, w_down, b_down):
    n_rows = xb.shape[0] // ROW_CHUNKS
    bm = MOE_BM
    depth = w_gate_up.shape[0]
    bgu = b_gate_up.reshape(depth, N_EXPERTS, 1, 2 * D_EXPERT)
    bd = b_down.reshape(depth, N_EXPERTS, 1, D_MODEL)
    gs = pltpu.PrefetchScalarGridSpec(
        num_scalar_prefetch=2,
        grid=(n_rows // bm,),
        in_specs=[pl.BlockSpec((bm * ROW_CHUNKS, LANES), lambda i, be, bv: (i, 0)),
                  pl.BlockSpec((None, None, D_MODEL, 2 * D_EXPERT), lambda i, be, bv: (layer, be[i], 0, 0)),
                  pl.BlockSpec((None, None, 1, 2 * D_EXPERT), lambda i, be, bv: (layer, be[i], 0, 0)),
                  pl.BlockSpec((None, None, D_EXPERT, D_MODEL), lambda i, be, bv: (layer, be[i], 0, 0)),
                  pl.BlockSpec((None, None, 1, D_MODEL), lambda i, be, bv: (layer, be[i], 0, 0))],
        out_specs=pl.BlockSpec((bm * ROW_CHUNKS, LANES), lambda i, be, bv: (i, 0)),
        scratch_shapes=[pltpu.VMEM((D_MODEL, 2 * D_EXPERT), BF16), pltpu.VMEM((D_EXPERT, D_MODEL), BF16)],
    )
    return pl.pallas_call(
        _expert_kernel,
        out_shape=jax.ShapeDtypeStruct((n_rows * ROW_CHUNKS, LANES), F32),
        grid_spec=gs,
        compiler_params=_cparams(("arbitrary",)),
        name="moe_experts",
    )(block_exp, block_valid, xb, w_gate_up, bgu, w_down, bd)


def _combine_kernel(dest_ref, gate_ref, x_ref, mod_ref, lw_ref, lb_ref, yb_ref, o_ref, ybuf, sem, *, n_ctx_tiles, alpha):
    tm = x_ref.shape[0]

    def row_copy(a):
        src = pl.multiple_of(dest_ref[a] * ROW_CHUNKS, ROW_CHUNKS)
        dst = pl.multiple_of((a // TOP_K) * ROW_CHUNKS, ROW_CHUNKS)
        return pltpu.make_async_copy(yb_ref.at[pl.ds(src, ROW_CHUNKS)],
                                     ybuf.at[a % TOP_K, pl.ds(dst, ROW_CHUNKS)], sem)

    def start(a, c):
        row_copy(a).start()
        return c

    def wait(a, c):
        row_copy(a).wait()
        return c

    lax.fori_loop(0, tm * TOP_K, start, 0)
    lax.fori_loop(0, tm * TOP_K, wait, 0)
    gates = gate_ref[...]
    y = gates[:, 0:1] * _from_chunk_rows(ybuf.at[0], tm)
    for k in range(1, TOP_K):
        y = y + gates[:, k:k + 1] * _from_chunk_rows(ybuf.at[k], tm)
    is_ctx = pl.program_id(0) < n_ctx_tiles
    g2 = _mod_rows(mod_ref, 5, is_ctx)
    o_ref[...] = _layer_norm(alpha * x_ref[...] + g2 * y, lw_ref[...], lb_ref[...])


def _combine(dest_flat, gates, xs, mod, ln_w, ln_b, yb, n_ctx_tiles, alpha):
    t = xs.shape[0]
    tm = TOK_TILE
    lw2, lb2 = ln_w.reshape(1, -1), ln_b.reshape(1, -1)
    full = lambda a: pl.BlockSpec(a.shape, lambda i: (0,) * a.ndim)
    return pl.pallas_call(
        functools.partial(_combine_kernel, n_ctx_tiles=n_ctx_tiles, alpha=alpha),
        out_shape=jax.ShapeDtypeStruct((t, D_MODEL), F32),
        grid=(t // tm,),
        in_specs=[pl.BlockSpec((tm * TOP_K,), lambda i: (i,), memory_space=pltpu.SMEM),
                  pl.BlockSpec((tm, LANES), lambda i: (i, 0)),
                  pl.BlockSpec((tm, D_MODEL), lambda i: (i, 0)),
                  full(mod), full(lw2), full(lb2),
                  pl.BlockSpec(memory_space=pl.ANY)],
        out_specs=pl.BlockSpec((tm, D_MODEL), lambda i: (i, 0)),
        scratch_shapes=[pltpu.VMEM((TOP_K, tm * ROW_CHUNKS, LANES), F32), pltpu.SemaphoreType.DMA],
        compiler_params=_cparams(("arbitrary",)),
        name="moe_combine",
    )(dest_flat, gates, xs, mod, lw2, lb2, yb)


def _moe(xs, mod, layer, w_router, b_router, w_gate_up, b_gate_up, w_down, b_down, ln_w, ln_b, n_ctx_tiles, alpha):
    t = xs.shape[0]
    bm = MOE_BM
    idx, gates, rank, cnt = _router(xs, mod, w_router, b_router, n_ctx_tiles)
    counts = cnt[0]
    padded = (counts + bm - 1) // bm * bm
    pad_end = jnp.cumsum(padded).astype(jnp.int32)
    pad_start = pad_end - padded
    dest = pad_start[idx[:, :TOP_K]] + rank[:, :TOP_K]
    dest_flat = dest.reshape(-1).astype(jnp.int32)
    n_blocks = -(-(t * TOP_K) // bm) + N_EXPERTS
    blk0 = jnp.arange(n_blocks, dtype=jnp.int32) * bm
    block_exp = jnp.minimum(jnp.searchsorted(pad_end, blk0, side='right'), N_EXPERTS - 1).astype(jnp.int32)
    block_valid = (blk0 < pad_end[-1]).astype(jnp.int32)
    xb = _dispatch(pad_start, pad_end, dest_flat, xs, mod, n_blocks * bm, n_ctx_tiles)
    yb = _experts(block_exp, block_valid, xb, layer, w_gate_up, b_gate_up, w_down, b_down)
    return _combine(dest_flat, gates, xs, mod, ln_w, ln_b, yb, n_ctx_tiles, alpha)


def kernel(x, c, ctx, c_ctx, w_ada, b_ada, w_in, mlstm_gate_bias, diff_lambda, diff_norm_w, mlstm_norm_w, w_out,
           ln1_w, ln1_b, w_router, b_router, w_gate_up, b_gate_up, w_down, b_down, ln2_w, ln2_b):
    assert x.shape[0] == 1 and ctx.shape[0] == 1
    depth = w_ada.shape[0]
    n_lat, n_ctx = x.shape[1], ctx.shape[1]
    assert n_ctx % TOK_TILE == 0 and n_lat % TOK_TILE == 0
    nct = n_ctx // TOK_TILE
    alpha = (2 * depth) ** 0.25

    vecs = jnp.zeros((8, D_MODEL), F32).at[0].set(c[0]).at[1].set(c_ctx)
    mods = _ada(vecs, w_ada, b_ada)
    cos, sin = _rope_tables(n_lat, n_ctx)
    xt = jnp.concatenate([ctx[0], x[0]], axis=0)

    for l in range(depth):
        need_ctx = l < depth - 1
        lam_init = 0.8 - 0.6 * math.exp(-0.3 * l)
        mod = mods[l]
        q, k, v, mq, mk, mv, mo, g, cu = _inproj(xt, mod, _prep_w_in(w_in[l]), cos, sin, mlstm_gate_bias[l], n_ctx)

        dl = diff_lambda[l].astype(F32)
        lam = (jnp.exp(jnp.sum(dl[0] * dl[1])) - jnp.exp(jnp.sum(dl[2] * dl[3])) + lam_init).reshape(1)
        post = 1.0 - lam_init
        attn_l = _attention(lam, q, k, v, diff_norm_w[l], n_ctx, n_lat, n_ctx + n_lat, post)
        four_l = _fourier_lat(cu[n_ctx:])
        if need_ctx:
            attn_c = _attention(lam, q, k, v, diff_norm_w[l], 0, n_ctx, n_ctx, post)
            four_c = _fourier_small(cu[:n_ctx])
        else:
            attn_c, four_c = attn_l, four_l
        hf, hb = _mlstm(mq, mk, mv, g, n_ctx)
        xs = _outproj(attn_l, attn_c, four_l, four_c, hf, hb, mo, xt, mod, mlstm_norm_w[l], w_out[l].astype(BF16),
                      ln1_w[l], ln1_b[l], n_ctx, need_ctx, alpha)
        xt = _moe(xs, mod, l, w_router[l], b_router[l], w_gate_up, b_gate_up, w_down, b_down, ln2_w[l], ln2_b[l],
                  nct if need_ctx else 0, alpha)
    return xt[None]
```

```python
import functools
import math

import numpy as np
import jax
import jax.numpy as jnp
from jax import lax
from jax.experimental import pallas as pl
from jax.experimental.pallas import tpu as pltpu

F32 = jnp.float32
BF16 = jnp.bfloat16
HI = lax.Precision.HIGHEST

D_MODEL = 1024
GRID_W = 64
HEAD_DIM = 64
A_HEADS = 4
A_QK_DIM = HEAD_DIM
A_V_DIM = 2 * HEAD_DIM
A_WIDTH = A_HEADS * A_V_DIM
A_SCALE = A_QK_DIM ** -0.5
B_HEADS = 4
B_HEAD_DIM = HEAD_DIM
B_WIDTH = B_HEADS * B_HEAD_DIM
C_GROUPS = 4
C_GROUP_DIM = HEAD_DIM
C_WIDTH = C_GROUPS * C_GROUP_DIM
IN_SIZES = (A_WIDTH, A_WIDTH, A_WIDTH, B_WIDTH, B_WIDTH, B_WIDTH, B_WIDTH, 4 * B_HEADS, C_WIDTH)
ROPE_BASE = 10000.0
MLSTM_CHUNK = 64
N_EXPERTS = 32
TOP_K = 4
D_EXPERT = D_MODEL
SWIGLU_LIMIT = 7.0
SWIGLU_ALPHA = 1.702
EPS = 1e-5

LANES = 128
VMEM_LIMIT = 56 * 1024 * 1024
TOK_TILE = 256
ATT_TQ = 512
ATT_UNROLL = 4
MOE_BM = 256

_OQ, _OQR, _OK, _OKR = 0, 512, 1024, 1536
_OMQ, _OMK, _OMV, _OMO, _OG, _OCU = 2048, 2304, 2560, 2816, 3072, 3200
Q_SCALE = A_SCALE * math.log2(math.e)
VT_ROWS = A_V_DIM + 16


def _cparams(sem, vmem=VMEM_LIMIT):
    return pltpu.CompilerParams(dimension_semantics=sem, vmem_limit_bytes=vmem)


def _ada_kernel(v_ref, w_ref, b_ref, o_ref):
    v = v_ref[...]
    s = v * jax.nn.sigmoid(v)
    o_ref[...] = jnp.dot(s, w_ref[...], precision=HI, preferred_element_type=F32) + b_ref[...]


def _ada(vecs, w_ada, b_ada):
    depth, d, d6 = w_ada.shape
    bn = 1024
    return pl.pallas_call(
        _ada_kernel,
        out_shape=jax.ShapeDtypeStruct((depth, 8, d6), F32),
        grid=(depth, d6 // bn),
        in_specs=[pl.BlockSpec((8, d), lambda l, j: (0, 0)),
                  pl.BlockSpec((None, d, bn), lambda l, j: (l, 0, j)),
                  pl.BlockSpec((None, 1, bn), lambda l, j: (l, 0, j))],
        out_specs=pl.BlockSpec((None, 8, bn), lambda l, j: (l, 0, j)),
        compiler_params=_cparams(("arbitrary", "arbitrary")),
        name="ada_modulation",
    )(vecs, w_ada, b_ada.reshape(depth, 1, d6))


def _mod_rows(mod_ref, k, is_ctx):
    lat = mod_ref[0:1, k * D_MODEL:(k + 1) * D_MODEL]
    ctx = mod_ref[1:2, k * D_MODEL:(k + 1) * D_MODEL]
    return jnp.where(is_ctx, ctx, lat)


def _inproj_kernel(x_ref, mod_ref, w_ref, wvt_ref, cos_ref, sin_ref, gb_ref,
                   q_ref, k_ref, vt_ref, mq_ref, mk_ref, mv_ref, mo_ref, g_ref, cu_ref, *, n_ctx_tiles):
    is_ctx = pl.program_id(0) < n_ctx_tiles
    sh = _mod_rows(mod_ref, 0, is_ctx)
    sc = _mod_rows(mod_ref, 1, is_ctx)
    hm = (x_ref[...] * (1.0 + sc) + sh).astype(BF16)

    def proj(off, width):
        return jnp.dot(hm, w_ref[:, off:off + width], preferred_element_type=F32)

    cos = cos_ref[...]
    sin = sin_ref[...]
    for j in range(A_HEADS):
        o = j * LANES
        q = proj(_OQ + o, LANES) * cos + proj(_OQR + o, LANES) * sin
        q_ref[:, o:o + LANES] = (q * Q_SCALE).astype(BF16)
        k = proj(_OK + o, LANES) * cos + proj(_OKR + o, LANES) * sin
        k_ref[:, o:o + LANES] = k.astype(BF16)
    vt = lax.dot_general(wvt_ref[...], hm, (((1,), (1,)), ((), ())), preferred_element_type=F32).astype(BF16)
    for j in range(A_HEADS):
        vt_ref[j, 0:A_V_DIM, :] = vt[j * A_V_DIM:(j + 1) * A_V_DIM, :]
        vt_ref[j, A_V_DIM:, :] = jnp.ones((VT_ROWS - A_V_DIM, vt.shape[1]), BF16)
    mq_ref[...] = proj(_OMQ, B_WIDTH)
    mk_ref[...] = proj(_OMK, B_WIDTH) * (B_HEAD_DIM ** -0.5)
    mv_ref[...] = proj(_OMV, B_WIDTH)
    mo_ref[...] = proj(_OMO, B_WIDTH)
    g_ref[...] = proj(_OG, LANES) + gb_ref[...]
    cu_ref[...] = proj(_OCU, C_WIDTH)


def _rot_cols(w):
    d, width = w.shape
    wr = w.reshape(d, width // 64, 2, 2, 16)
    rot = jnp.stack([-wr[:, :, :, 1, :], wr[:, :, :, 0, :]], axis=3)
    return rot.reshape(d, width)


def _prep_w_in(w_in):
    offs = np.cumsum((0,) + IN_SIZES)
    aq, ak, av, bq, bk, bv, bo, bg, cu = [w_in[:, offs[i]:offs[i + 1]] for i in range(9)]
    bg = jnp.pad(bg, ((0, 0), (0, LANES - bg.shape[1])))
    w = jnp.concatenate([aq, _rot_cols(aq), ak, _rot_cols(ak), bq, bk, bv, bo, bg, cu], axis=1)
    return w.astype(BF16), av.T.astype(BF16)


def _rope_tables(n_lat, n_ctx):
    rows = n_lat // GRID_W
    row = jnp.broadcast_to(jnp.arange(rows, dtype=F32)[:, None], (rows, GRID_W)).reshape(-1)
    col = jnp.broadcast_to(jnp.arange(GRID_W, dtype=F32)[None, :], (rows, GRID_W)).reshape(-1)
    axis_dim = A_QK_DIM // 2
    inv_freq = ROPE_BASE ** (-jnp.arange(0, axis_dim, 2, dtype=F32) / axis_dim)
    ang_r = row[:, None] * inv_freq
    ang_c = col[:, None] * inv_freq
    ang = jnp.concatenate([ang_r, ang_r, ang_c, ang_c], axis=-1)
    cos = jnp.concatenate([jnp.ones((n_ctx, A_QK_DIM), F32), jnp.cos(ang)], axis=0)
    sin = jnp.concatenate([jnp.zeros((n_ctx, A_QK_DIM), F32), jnp.sin(ang)], axis=0)
    return jnp.tile(cos, (1, 2)), jnp.tile(sin, (1, 2))


def _inproj(xt, mod, w_ext, w_vt, cos, sin, gate_bias, n_ctx):
    t = xt.shape[0]
    tm = TOK_TILE
    gb = jnp.pad(gate_bias.reshape(1, -1), ((0, 0), (0, LANES - 4 * B_HEADS)))
    row = lambda w: pl.BlockSpec((tm, w), lambda i: (i, 0))
    full = lambda a: pl.BlockSpec(a.shape, lambda i: (0,) * a.ndim)
    sds = jax.ShapeDtypeStruct
    rows_out = [(A_WIDTH, BF16)] * 2 + [(B_WIDTH, F32)] * 4 + [(LANES, F32), (C_WIDTH, F32)]
    out_shape = [sds((t, w), dt) for w, dt in rows_out]
    out_specs = [row(w) for w, _ in rows_out]
    out_shape.insert(2, sds((t // tm, A_HEADS, VT_ROWS, tm), BF16))
    out_specs.insert(2, pl.BlockSpec((None, A_HEADS, VT_ROWS, tm), lambda i: (i, 0, 0, 0)))
    return pl.pallas_call(
        functools.partial(_inproj_kernel, n_ctx_tiles=n_ctx // tm),
        out_shape=out_shape,
        grid=(t // tm,),
        in_specs=[row(D_MODEL), full(mod), full(w_ext), full(w_vt), row(LANES), row(LANES), full(gb)],
        out_specs=out_specs,
        compiler_params=_cparams(("arbitrary",)),
        name="in_projection",
    )(xt, mod, w_ext, w_vt, cos, sin, gb)


def _attn_kernel(lam_ref, q_ref, k_ref, vt_ref, nw_ref, o_ref, acc_ref, s_ref, *, n_kv, post_scale):
    q = q_ref[...]
    tq = q.shape[0]
    tk = vt_ref.shape[-1]
    lane = lax.broadcasted_iota(jnp.int32, q.shape, 1)
    zero = jnp.zeros_like(q)
    q2 = jnp.concatenate([jnp.where(lane < A_QK_DIM, q, zero), jnp.where(lane >= A_QK_DIM, q, zero)], axis=0)
    acc_ref[...] = jnp.zeros_like(acc_ref)

    def scores(j, slot):
        kb = k_ref[pl.ds(pl.multiple_of(j * tk, tk), tk), :]
        st = lax.dot_general(kb, q2, (((1,), (1,)), ((), ())), preferred_element_type=F32)
        s_ref[slot] = st
        return jnp.max(st, axis=0, keepdims=True)

    def consume(j, slot, cmax, m):
        m_new = jnp.maximum(m, cmax)
        alpha = jnp.exp2(m - m_new)
        for h in range(2):
            cs = slice(h * tq, (h + 1) * tq)
            p = jnp.exp2(s_ref[slot, :, cs] - m_new[:, cs]).astype(BF16)
            r = jnp.dot(vt_ref[j], p, preferred_element_type=F32)
            acc_ref[:, cs] = alpha[:, cs] * acc_ref[:, cs] + r
        return m_new

    cm0 = scores(0, 0)

    def group(t, carry):
        cm, m = carry
        a = ATT_UNROLL * t
        for i in range(ATT_UNROLL):
            cm_next = scores(a + i + 1, (i + 1) % 2)
            m = consume(a + i, i % 2, cm, m)
            cm = cm_next
        return cm, m

    assert (n_kv - 1) % ATT_UNROLL == 0
    cm_last, m = lax.fori_loop(0, (n_kv - 1) // ATT_UNROLL, group, (cm0, jnp.full((1, 2 * tq), -jnp.inf, F32)))
    consume(n_kv - 1, 0, cm_last, m)
    l = acc_ref[A_V_DIM:A_V_DIM + 1, :]
    acc = acc_ref[0:A_V_DIM, :]
    ot = acc[:, :tq] * (1.0 / l[:, :tq]) - acc[:, tq:] * (lam_ref[0] / l[:, tq:])
    yt = ot * lax.rsqrt(jnp.mean(ot * ot, axis=0, keepdims=True) + EPS)
    o_ref[...] = yt.T * nw_ref[...] * post_scale


def _attention(lam, q, k, vt, norm_w, q_row0, n_q, n_keys, post_scale):
    tq = min(ATT_TQ, n_q)
    tk = vt.shape[-1]
    n_kv = n_keys // tk
    if q_row0 % tq:
        q, q_row0 = q[q_row0:q_row0 + n_q], 0
    q_blk0 = q_row0 // tq
    return pl.pallas_call(
        functools.partial(_attn_kernel, n_kv=n_kv, post_scale=post_scale),
        out_shape=jax.ShapeDtypeStruct((n_q, A_WIDTH), F32),
        grid=(A_HEADS, n_q // tq),
        in_specs=[pl.BlockSpec(memory_space=pltpu.SMEM),
                  pl.BlockSpec((tq, LANES), lambda h, i: (i + q_blk0, h)),
                  pl.BlockSpec((n_keys, LANES), lambda h, i: (0, h)),
                  pl.BlockSpec((n_kv, None, VT_ROWS, tk), lambda h, i: (0, h, 0, 0)),
                  pl.BlockSpec((1, LANES), lambda h, i: (0, h))],
        out_specs=pl.BlockSpec((tq, LANES), lambda h, i: (i, h)),
        scratch_shapes=[pltpu.VMEM((VT_ROWS, 2 * tq), F32), pltpu.VMEM((2, tk, 2 * tq), F32)],
        compiler_params=_cparams(("arbitrary", "arbitrary")),
        name="diff_attention",
    )(lam, q, k, vt, norm_w.reshape(1, A_WIDTH))


def _mlstm_chunk(d, q_t, k_t, v_t, g_all, s_ref, m_ref, tri_f, tri_mask, lane_half, row_half, blockmask, aug):
    L = MLSTM_CHUNK
    ls = jax.nn.log_sigmoid(g_all)
    b_all = jnp.dot(tri_f, ls, precision=HI, preferred_element_type=F32)
    gsum = jnp.sum(ls, axis=0, keepdims=True)
    g_t = g_all.T
    b_t = b_all.T
    outs = []
    for p in range(2):
        sl = slice(p * LANES, (p + 1) * LANES)
        qp, kp, vp = q_t[:, sl], k_t[:, sl], v_t[:, sl]
        v_aug = jnp.concatenate([vp, aug], axis=1)
        state = s_ref[d, p]
        q2 = jnp.dot(qp, state, precision=HI, preferred_element_type=F32)
        head_out, w_cols, decs, injs = [], [], [], []
        for hh in range(2):
            h = 2 * p + hh
            il = 4 * d + h
            fl = 8 + 4 * d + h
            b_col = b_all[:, fl:fl + 1]
            i_col = g_all[:, il:il + 1]
            r_row = g_t[il:il + 1, :] - b_t[fl:fl + 1, :]
            gs = gsum[:, fl:fl + 1]
            m_prev = m_ref[il:il + 1, 0:1]
            a_col = gs - b_col + i_col
            m_loc = jnp.max(a_col, axis=0, keepdims=True)
            w_cols.append(jnp.exp(a_col - m_loc))
            dmat = jnp.where(tri_mask, b_col + r_row, -jnp.inf)
            inter = b_col + m_prev
            m_col = jnp.maximum(inter, jnp.max(dmat, axis=-1, keepdims=True))
            dm = jnp.exp(dmat - m_col)
            qm = jnp.where(lane_half == hh, qp, 0.0)
            sqk = lax.dot_general(qm, kp, (((1,), (1,)), ((), ())), precision=HI, preferred_element_type=F32)
            r = jnp.dot(sqk * dm, v_aug, precision=HI, preferred_element_type=F32)
            e_int = jnp.exp(inter - m_col)
            num = r[:, :LANES] + e_int * q2[:, :LANES]
            den = r[:, LANES:LANES + 1] + e_int * q2[:, LANES + hh:LANES + hh + 1]
            head_out.append(num / jnp.maximum(jnp.abs(den), jnp.exp(-m_col)))
            m_new = jnp.maximum(gs + m_prev, m_loc)
            decs.append(jnp.exp(gs + m_prev - m_new))
            injs.append(jnp.exp(m_loc - m_new))
            m_ref[il:il + 1, :] = jnp.broadcast_to(m_new, (1, LANES))
        outs.append(jnp.where(lane_half == 0, head_out[0], head_out[1]))
        kw = kp * jnp.where(lane_half == 0, w_cols[0], w_cols[1])
        upd = lax.dot_general(kw, v_aug, (((0,), (0,)), ((), ())), precision=HI, preferred_element_type=F32)
        dec = jnp.where(row_half == 0, decs[0], decs[1])
        inj = jnp.where(row_half == 0, injs[0], injs[1])
        s_ref[d, p] = dec * state + inj * jnp.where(blockmask, upd, 0.0)
    return jnp.concatenate(outs, axis=1)


def _mlstm_kernel(qf, kf, vf, gf, qb, kb, vb, gb, of_ref, ob_ref, s_ref, m_ref, *, chunks):
    L = MLSTM_CHUNK

    @pl.when(pl.program_id(0) == 0)
    def _():
        s_ref[...] = jnp.zeros_like(s_ref)
        m_ref[...] = jnp.zeros_like(m_ref)

    r = lax.broadcasted_iota(jnp.int32, (L, L), 0)
    c = lax.broadcasted_iota(jnp.int32, (L, L), 1)
    masks = (r >= c, r <= c)
    lane_half = lax.broadcasted_iota(jnp.int32, (L, LANES), 1) // B_HEAD_DIM
    row_half = lax.broadcasted_iota(jnp.int32, (LANES, 1), 0) // B_HEAD_DIM
    rr = lax.broadcasted_iota(jnp.int32, (LANES, 2 * LANES), 0) // B_HEAD_DIM
    cc = lax.broadcasted_iota(jnp.int32, (LANES, 2 * LANES), 1)
    blockmask = jnp.where(cc < LANES, cc // B_HEAD_DIM, cc - LANES) == rr
    aug = jnp.where(lax.broadcasted_iota(jnp.int32, (L, LANES), 1) < 2, 1.0, 0.0).astype(F32)
    io = ((qf, kf, vf, gf, of_ref), (qb, kb, vb, gb, ob_ref))

    def body(j, carry):
        for d in range(2):
            q_r, k_r, v_r, g_r, o_r = io[d]
            ci = j if d == 0 else chunks - 1 - j
            sl = pl.ds(pl.multiple_of(ci * L, L), L)
            o_r[sl, :] = _mlstm_chunk(d, q_r[sl, :], k_r[sl, :], v_r[sl, :], g_r[sl, :], s_ref, m_ref,
                                      masks[d].astype(F32), masks[d], lane_half, row_half, blockmask, aug)
        return carry

    lax.fori_loop(0, chunks, body, 0)


def _mlstm(mq, mk, mv, g, n_ctx):
    t = mq.shape[0]
    tm = TOK_TILE
    n_tiles = t // tm
    n_ctx_tiles = n_ctx // tm
    fwd = lambda i: (i, 0)
    bwd = lambda i: (jnp.where(i < n_ctx_tiles, n_ctx_tiles - 1 - i, n_tiles - 1 - (i - n_ctx_tiles)), 0)
    spec = lambda w, im: pl.BlockSpec((tm, w), im)
    ins = [spec(B_WIDTH, fwd)] * 3 + [spec(LANES, fwd)] + [spec(B_WIDTH, bwd)] * 3 + [spec(LANES, bwd)]
    return pl.pallas_call(
        functools.partial(_mlstm_kernel, chunks=tm // MLSTM_CHUNK),
        out_shape=[jax.ShapeDtypeStruct((t, B_WIDTH), F32)] * 2,
        grid=(n_tiles,),
        in_specs=ins,
        out_specs=[spec(B_WIDTH, fwd), spec(B_WIDTH, bwd)],
        scratch_shapes=[pltpu.VMEM((2, 2, LANES, 2 * LANES), F32), pltpu.VMEM((8, LANES), F32)],
        compiler_params=_cparams(("arbitrary",)),
        name="mlstm_scan",
    )(mq, mk, mv, g, mq, mk, mv, g)


def _dft_cs(n):
    j = np.arange(n, dtype=np.int64)
    ang = 2.0 * np.pi * ((j[:, None] * j[None, :]) % n).astype(np.float64) / n
    s = 1.0 / math.sqrt(n)
    return np.cos(ang) * s, np.sin(ang) * s


def _chan_mats():
    c, s = _dft_cs(C_GROUP_DIM)
    eye = np.eye(C_GROUPS)
    return np.concatenate([np.kron(eye, c), np.kron(eye, s)], axis=0).astype(np.float32)


def _fft1_kernel(fa_ref, u_ref, z_ref):
    z_ref[...] = jnp.dot(fa_ref[...], u_ref[...], precision=HI, preferred_element_type=F32)


def _fft2_kernel(z_ref, twc_ref, tws_ref, fb_ref, ch_ref, o_ref, *, kb):
    n2 = fb_ref.shape[0] // 2
    for j in range(kb):
        zr = z_ref[0, j]
        zi = z_ref[1, j]
        tc = twc_ref[j]
        ts = tws_ref[j]
        st = jnp.concatenate([zr * tc + zi * ts, zi * tc - zr * ts], axis=0)
        a = jnp.dot(fb_ref[...], st, precision=HI, preferred_element_type=F32)
        ari = jnp.concatenate([a[:n2], a[n2:]], axis=1)
        o_ref[:, j * C_WIDTH:(j + 1) * C_WIDTH] = jnp.dot(ari, ch_ref[...], precision=HI,
                                                         preferred_element_type=F32)


def _fourier_lat(u):
    n = u.shape[0]
    n1 = int(round(math.sqrt(n)))
    assert n1 * n1 == n
    n2 = n1
    c, s = _dft_cs(n1)
    fa = jnp.asarray(np.concatenate([c, -s], axis=0).astype(np.float32))
    fb = jnp.asarray(np.block([[c, s], [-s, c]]).astype(np.float32))
    k1 = np.arange(n1, dtype=np.int64)
    ang = 2.0 * np.pi * ((k1[:, None] * k1[None, :]) % n).astype(np.float64) / n
    twc = jnp.asarray(np.cos(ang).astype(np.float32)).reshape(n1, n2, 1)
    tws = jnp.asarray(np.sin(ang).astype(np.float32)).reshape(n1, n2, 1)
    ch = jnp.asarray(_chan_mats())
    cols = n2 * C_WIDTH
    tb = min(2048, cols)
    z = pl.pallas_call(
        _fft1_kernel,
        out_shape=jax.ShapeDtypeStruct((2 * n1, cols), F32),
        grid=(cols // tb,),
        in_specs=[pl.BlockSpec((2 * n1, n1), lambda i: (0, 0)), pl.BlockSpec((n1, tb), lambda i: (0, i))],
        out_specs=pl.BlockSpec((2 * n1, tb), lambda i: (0, i)),
        compiler_params=_cparams(("arbitrary",)),
        name="fourier_stage1",
    )(fa, u.reshape(n1, cols))
    kb = 8
    out = pl.pallas_call(
        functools.partial(_fft2_kernel, kb=kb),
        out_shape=jax.ShapeDtypeStruct((n2, n1 * C_WIDTH), F32),
        grid=(n1 // kb,),
        in_specs=[pl.BlockSpec((2, kb, n2, C_WIDTH), lambda i: (0, i, 0, 0)),
                  pl.BlockSpec((kb, n2, 1), lambda i: (i, 0, 0)),
                  pl.BlockSpec((kb, n2, 1), lambda i: (i, 0, 0)),
                  pl.BlockSpec((2 * n2, 2 * n2), lambda i: (0, 0)),
                  pl.BlockSpec((2 * C_WIDTH, C_WIDTH), lambda i: (0, 0))],
        out_specs=pl.BlockSpec((n2, kb * C_WIDTH), lambda i: (0, i)),
        compiler_params=_cparams(("arbitrary",)),
        name="fourier_stage2",
    )(z.reshape(2, n1, n2, C_WIDTH), twc, tws, fb, ch)
    return out.reshape(n, C_WIDTH)


def _fft_small_kernel(fa_ref, u_ref, ch_ref, o_ref):
    n = u_ref.shape[0]
    a = jnp.dot(fa_ref[...], u_ref[...], precision=HI, preferred_element_type=F32)
    ari = jnp.concatenate([a[:n], a[n:]], axis=1)
    o_ref[...] = jnp.dot(ari, ch_ref[...], precision=HI, preferred_element_type=F32)


def _fourier_small(u):
    n = u.shape[0]
    c, s = _dft_cs(n)
    fa = jnp.asarray(np.concatenate([c, -s], axis=0).astype(np.float32))
    return pl.pallas_call(
        _fft_small_kernel,
        out_shape=jax.ShapeDtypeStruct((n, C_WIDTH), F32),
        compiler_params=_cparams(None),
        name="fourier_small",
    )(fa, u, jnp.asarray(_chan_mats()))


def _layer_norm(z, w, b):
    mu = jnp.mean(z, axis=-1, keepdims=True)
    zc = z - mu
    var = jnp.mean(zc * zc, axis=-1, keepdims=True)
    return zc * lax.rsqrt(var + EPS) * w + b


def _outproj_kernel(al_ref, ac_ref, fl_ref, fc_ref, hf_ref, hb_ref, bo_ref, x_ref, mod_ref, mnw_ref, seg_ref,
                    w_ref, lw_ref, lb_ref, o_ref, *, n_ctx_tiles, alpha):
    is_ctx = pl.program_id(0) < n_ctx_tiles
    attn = jnp.where(is_ctx, ac_ref[...], al_ref[...])
    four = jnp.where(is_ctx, fc_ref[...], fl_ref[...])
    h = hf_ref[...] + hb_ref[...]
    ms = jnp.dot(h * h, seg_ref[...], precision=HI, preferred_element_type=F32)
    y = h * lax.rsqrt(ms + EPS) * mnw_ref[...] * jax.nn.sigmoid(bo_ref[...])
    mix = (jnp.dot(attn.astype(BF16), w_ref[0:A_WIDTH, :], preferred_element_type=F32)
           + jnp.dot(y.astype(BF16), w_ref[A_WIDTH:A_WIDTH + B_WIDTH, :], preferred_element_type=F32)
           + jnp.dot(four.astype(BF16), w_ref[A_WIDTH + B_WIDTH:, :], preferred_element_type=F32))
    g1 = _mod_rows(mod_ref, 2, is_ctx)
    o_ref[...] = _layer_norm(alpha * x_ref[...] + g1 * mix, lw_ref[...], lb_ref[...])


def _outproj(attn_l, attn_c, four_l, four_c, hf, hb, bo, xt, mod, mnw, w_out_bf, ln_w, ln_b, n_ctx, with_ctx, alpha):
    tm = TOK_TILE
    nct = n_ctx // tm
    n_lat_tiles = attn_l.shape[0] // tm
    if with_ctx:
        n_tiles, uoff, n_ctx_tiles = n_lat_tiles + nct, 0, nct
        lat = lambda i: (jnp.maximum(i - nct, 0), 0)
    else:
        n_tiles, uoff, n_ctx_tiles = n_lat_tiles, nct, 0
        lat = lambda i: (i, 0)
    uni = lambda i: (i + uoff, 0)
    ctxm = lambda i: (jnp.minimum(i, nct - 1), 0)
    seg = jnp.asarray(np.kron(np.eye(B_HEADS), np.full((B_HEAD_DIM, B_HEAD_DIM), 1.0 / B_HEAD_DIM)).astype(np.float32))
    full = lambda a: pl.BlockSpec(a.shape, lambda i: (0,) * a.ndim)
    mnw2, lw2, lb2 = mnw.reshape(1, -1), ln_w.reshape(1, -1), ln_b.reshape(1, -1)
    return pl.pallas_call(
        functools.partial(_outproj_kernel, n_ctx_tiles=n_ctx_tiles, alpha=alpha),
        out_shape=jax.ShapeDtypeStruct((n_tiles * tm, D_MODEL), F32),
        grid=(n_tiles,),
        in_specs=[pl.BlockSpec((tm, A_WIDTH), lat), pl.BlockSpec((tm, A_WIDTH), ctxm),
                  pl.BlockSpec((tm, C_WIDTH), lat), pl.BlockSpec((tm, C_WIDTH), ctxm),
                  pl.BlockSpec((tm, B_WIDTH), uni), pl.BlockSpec((tm, B_WIDTH), uni),
                  pl.BlockSpec((tm, B_WIDTH), uni), pl.BlockSpec((tm, D_MODEL), uni),
                  full(mod), full(mnw2), full(seg), full(w_out_bf), full(lw2), full(lb2)],
        out_specs=pl.BlockSpec((tm, D_MODEL), lambda i: (i, 0)),
        compiler_params=_cparams(("arbitrary",)),
        name="out_projection",
    )(attn_l, attn_c, four_l, four_c, hf, hb, bo, xt, mod, mnw2, seg, w_out_bf, lw2, lb2)


def _router_kernel(x_ref, mod_ref, wr_ref, br_ref, idx_ref, gate_ref, rank_ref, cnt_ref, run_ref, *, n_ctx_tiles):
    i = pl.program_id(0)

    @pl.when(i == 0)
    def _():
        run_ref[...] = jnp.zeros_like(run_ref)

    is_ctx = i < n_ctx_tiles
    f_in = x_ref[...] * (1.0 + _mod_rows(mod_ref, 4, is_ctx)) + _mod_rows(mod_ref, 3, is_ctx)
    logits = jnp.dot(f_in, wr_ref[...], precision=HI, preferred_element_type=F32) + br_ref[...]
    tm = logits.shape[0]
    eidx = lax.broadcasted_iota(jnp.int32, logits.shape, 1).astype(F32)
    lane = lax.broadcasted_iota(jnp.int32, (tm, LANES), 1)
    work = logits
    vals, sels = [], []
    for _ in range(TOP_K):
        mx = jnp.max(work, axis=-1, keepdims=True)
        sel = jnp.min(jnp.where(work == mx, eidx, float(N_EXPERTS)), axis=-1, keepdims=True)
        vals.append(mx)
        sels.append(sel)
        work = jnp.where(eidx == sel, -jnp.inf, work)
    es = [jnp.exp(v - vals[0]) for v in vals]
    tot = es[0] + es[1] + es[2] + es[3]
    onehots = [(eidx == s).astype(F32) for s in sels]
    oh_all = onehots[0] + onehots[1] + onehots[2] + onehots[3]
    r = lax.broadcasted_iota(jnp.int32, (tm, tm), 0)
    c = lax.broadcasted_iota(jnp.int32, (tm, tm), 1)
    before = jnp.dot((r > c).astype(BF16), oh_all.astype(BF16), preferred_element_type=F32) + run_ref[0:1, :]
    idx_o = jnp.zeros((tm, LANES), jnp.int32)
    gate_o = jnp.zeros((tm, LANES), F32)
    rank_o = jnp.zeros((tm, LANES), jnp.int32)
    for k in range(TOP_K):
        rank = jnp.sum(onehots[k] * before, axis=-1, keepdims=True).astype(jnp.int32)
        idx_o = jnp.where(lane == k, sels[k].astype(jnp.int32), idx_o)
        gate_o = jnp.where(lane == k, es[k] / tot, gate_o)
        rank_o = jnp.where(lane == k, rank, rank_o)
    idx_ref[...] = idx_o
    gate_ref[...] = gate_o
    rank_ref[...] = rank_o
    run_new = run_ref[0:1, :] + jnp.sum(oh_all, axis=0, keepdims=True)
    run_ref[...] = jnp.broadcast_to(run_new, run_ref.shape)
    cnt_ref[...] = jnp.broadcast_to(run_new, cnt_ref.shape).astype(jnp.int32)


def _router(xs, mod, w_router, b_router, n_ctx_tiles):
    t = xs.shape[0]
    tm = TOK_TILE
    br = b_router.reshape(1, -1)
    full = lambda a: pl.BlockSpec(a.shape, lambda i: (0,) * a.ndim)
    row = lambda w: pl.BlockSpec((tm, w), lambda i: (i, 0))
    return pl.pallas_call(
        functools.partial(_router_kernel, n_ctx_tiles=n_ctx_tiles),
        out_shape=[jax.ShapeDtypeStruct((t, LANES), jnp.int32), jax.ShapeDtypeStruct((t, LANES), F32),
                   jax.ShapeDtypeStruct((t, LANES), jnp.int32), jax.ShapeDtypeStruct((8, N_EXPERTS), jnp.int32)],
        grid=(t // tm,),
        in_specs=[row(D_MODEL), full(mod), full(w_router), full(br)],
        out_specs=[row(LANES), row(LANES), row(LANES), pl.BlockSpec((8, N_EXPERTS), lambda i: (0, 0))],
        scratch_shapes=[pltpu.VMEM((8, N_EXPERTS), F32)],
        compiler_params=_cparams(("arbitrary",)),
        name="moe_router",
    )(xs, mod, w_router, br)


ROW_CHUNKS = D_MODEL // LANES
DMA_UNROLL = 8


def _to_chunk_rows(ref, val):
    m = val.shape[0]
    for c in range(ROW_CHUNKS):
        ref[pl.ds(c, m, stride=ROW_CHUNKS), :] = val[:, c * LANES:(c + 1) * LANES]


def _from_chunk_rows(ref, m):
    return jnp.concatenate([ref[pl.ds(c, m, stride=ROW_CHUNKS), :] for c in range(ROW_CHUNKS)], axis=1)


def _dispatch_kernel(pstart_ref, pend_ref, dest_ref, x_ref, mod_ref, xb_ref, fbuf, zbuf, sem, zsem, *, n_ctx_tiles, bm):
    i = pl.program_id(0)
    tm = x_ref.shape[0]

    @pl.when(i == 0)
    def _():
        zbuf[...] = jnp.zeros_like(zbuf)

        def zcopy(e):
            row0 = pl.multiple_of((pend_ref[e] - bm) * ROW_CHUNKS, bm * ROW_CHUNKS)
            return pltpu.make_async_copy(zbuf, xb_ref.at[pl.ds(row0, bm * ROW_CHUNKS)], zsem)

        def zstart(e, c):
            @pl.when(pend_ref[e] > pstart_ref[e])
            def _():
                zcopy(e).start()
            return c

        def zwait(e, c):
            @pl.when(pend_ref[e] > pstart_ref[e])
            def _():
                zcopy(e).wait()
            return c

        lax.fori_loop(0, N_EXPERTS, zstart, 0)
        lax.fori_loop(0, N_EXPERTS, zwait, 0)

        def tcopy(b):
            row0 = pl.multiple_of(b * (bm * ROW_CHUNKS), bm * ROW_CHUNKS)
            return pltpu.make_async_copy(zbuf, xb_ref.at[pl.ds(row0, bm * ROW_CHUNKS)], zsem)

        def tstart(b, c):
            tcopy(b).start()
            return c

        def twait(b, c):
            tcopy(b).wait()
            return c

        tail0 = pend_ref[N_EXPERTS - 1] // bm
        n_blocks = xb_ref.shape[0] // (bm * ROW_CHUNKS)
        lax.fori_loop(tail0, n_blocks, tstart, 0)
        lax.fori_loop(tail0, n_blocks, twait, 0)

    is_ctx = i < n_ctx_tiles
    f = x_ref[...] * (1.0 + _mod_rows(mod_ref, 4, is_ctx)) + _mod_rows(mod_ref, 3, is_ctx)
    slot = i % 2
    _to_chunk_rows(fbuf.at[slot], f)

    def start(t, c):
        src = pl.multiple_of(t * ROW_CHUNKS, ROW_CHUNKS)
        for k in range(TOP_K):
            dst = pl.multiple_of(dest_ref[t * TOP_K + k] * ROW_CHUNKS, ROW_CHUNKS)
            pltpu.make_async_copy(fbuf.at[slot, pl.ds(src, ROW_CHUNKS)], xb_ref.at[pl.ds(dst, ROW_CHUNKS)],
                                  sem.at[slot]).start(priority=k % 2)
        return c

    lax.fori_loop(0, tm, start, 0, unroll=DMA_UNROLL)

    def wait_tile(s):
        for _ in range(TOP_K):
            pltpu.make_async_copy(fbuf.at[s], xb_ref.at[pl.ds(0, tm * ROW_CHUNKS)], sem.at[s]).wait()

    @pl.when(i > 0)
    def _():
        wait_tile(1 - slot)

    @pl.when(i == pl.num_programs(0) - 1)
    def _():
        wait_tile(slot)


def _dispatch(pad_start, pad_end, dest_flat, xs, mod, n_rows, n_ctx_tiles):
    t = xs.shape[0]
    tm = TOK_TILE
    gs = pltpu.PrefetchScalarGridSpec(
        num_scalar_prefetch=2,
        grid=(t // tm,),
        in_specs=[pl.BlockSpec((tm * TOP_K,), lambda i, *_: (i,), memory_space=pltpu.SMEM),
                  pl.BlockSpec((tm, D_MODEL), lambda i, *_: (i, 0)),
                  pl.BlockSpec(mod.shape, lambda i, *_: (0, 0))],
        out_specs=pl.BlockSpec(memory_space=pl.ANY),
        scratch_shapes=[pltpu.VMEM((2, tm * ROW_CHUNKS, LANES), F32), pltpu.VMEM((MOE_BM * ROW_CHUNKS, LANES), F32),
                        pltpu.SemaphoreType.DMA((2,)), pltpu.SemaphoreType.DMA],
    )
    return pl.pallas_call(
        functools.partial(_dispatch_kernel, n_ctx_tiles=n_ctx_tiles, bm=MOE_BM),
        out_shape=jax.ShapeDtypeStruct((n_rows * ROW_CHUNKS, LANES), F32),
        grid_spec=gs,
        compiler_params=_cparams(("arbitrary",)),
        name="moe_dispatch",
    )(pad_start, pad_end, dest_flat, xs, mod)


def _expert_kernel(bexp_ref, bvalid_ref, x_ref, wgu_ref, bgu_ref, wd_ref, bd_ref, y_ref, wgu_bf, wd_bf):
    i = pl.program_id(0)
    e = bexp_ref[i]
    prev = bexp_ref[jnp.maximum(i - 1, 0)]

    @pl.when(jnp.logical_or(i == 0, e != prev))
    def _():
        wgu_bf[...] = wgu_ref[...].astype(BF16)
        wd_bf[...] = wd_ref[...].astype(BF16)

    @pl.when(bvalid_ref[i] > 0)
    def _():
        bm = y_ref.shape[0] // ROW_CHUNKS
        x = _from_chunk_rows(x_ref, bm).astype(BF16)
        gu = jnp.dot(x, wgu_bf[...], preferred_element_type=F32) + bgu_ref[...]
        gate = jnp.minimum(gu[:, :D_EXPERT], SWIGLU_LIMIT)
        up = jnp.clip(gu[:, D_EXPERT:], -SWIGLU_LIMIT, SWIGLU_LIMIT)
        act = gate * jax.nn.sigmoid(SWIGLU_ALPHA * gate) * (up + 1.0)
        _to_chunk_rows(y_ref, jnp.dot(act.astype(BF16), wd_bf[...], preferred_element_type=F32) + bd_ref[...])

    @pl.when(bvalid_ref[i] == 0)
    def _():
        y_ref[...] = jnp.zeros_like(y_ref)


def _experts(block_exp, block_valid, xb, layer, w_gate_up, b_gate_up, w_down, b_down):
    n_rows = xb.shape[0] // ROW_CHUNKS
    bm = MOE_BM
    depth = w_gate_up.shape[0]
    bgu = b_gate_up.reshape(depth, N_EXPERTS, 1, 2 * D_EXPERT)
    bd = b_down.reshape(depth, N_EXPERTS, 1, D_MODEL)
    gs = pltpu.PrefetchScalarGridSpec(
        num_scalar_prefetch=2,
        grid=(n_rows // bm,),
        in_specs=[pl.BlockSpec((bm * ROW_CHUNKS, LANES), lambda i, be, bv: (i, 0)),
                  pl.BlockSpec((None, None, D_MODEL, 2 * D_EXPERT), lambda i, be, bv: (layer, be[i], 0, 0)),
                  pl.BlockSpec((None, None, 1, 2 * D_EXPERT), lambda i, be, bv: (layer, be[i], 0, 0)),
                  pl.BlockSpec((None, None, D_EXPERT, D_MODEL), lambda i, be, bv: (layer, be[i], 0, 0)),
                  pl.BlockSpec((None, None, 1, D_MODEL), lambda i, be, bv: (layer, be[i], 0, 0))],
        out_specs=pl.BlockSpec((bm * ROW_CHUNKS, LANES), lambda i, be, bv: (i, 0)),
        scratch_shapes=[pltpu.VMEM((D_MODEL, 2 * D_EXPERT), BF16), pltpu.VMEM((D_EXPERT, D_MODEL), BF16)],
    )
    return pl.pallas_call(
        _expert_kernel,
        out_shape=jax.ShapeDtypeStruct((n_rows * ROW_CHUNKS, LANES), F32),
        grid_spec=gs,
        compiler_params=_cparams(("arbitrary",)),
        name="moe_experts",
    )(block_exp, block_valid, xb, w_gate_up, bgu, w_down, bd)


def _combine_kernel(dest_ref, dnext_ref, gate_ref, x_ref, mod_ref, lw_ref, lb_ref, yb_ref, o_ref, ybuf, sem, *,
                    n_ctx_tiles, alpha):
    i = pl.program_id(0)
    tm = x_ref.shape[0]
    slot = i % 2

    def gather(d_ref, s):
        def start(t, c):
            dst = pl.multiple_of(t * ROW_CHUNKS, ROW_CHUNKS)
            for k in range(TOP_K):
                src = pl.multiple_of(d_ref[t * TOP_K + k] * ROW_CHUNKS, ROW_CHUNKS)
                pltpu.make_async_copy(yb_ref.at[pl.ds(src, ROW_CHUNKS)], ybuf.at[s, k, pl.ds(dst, ROW_CHUNKS)],
                                      sem.at[s]).start(priority=k % 2)
            return c

        lax.fori_loop(0, tm, start, 0, unroll=DMA_UNROLL)

    @pl.when(i == 0)
    def _():
        gather(dest_ref, 0)

    @pl.when(i + 1 < pl.num_programs(0))
    def _():
        gather(dnext_ref, 1 - slot)

    for k in range(TOP_K):
        pltpu.make_async_copy(yb_ref.at[pl.ds(0, tm * ROW_CHUNKS)], ybuf.at[slot, k], sem.at[slot]).wait()
    gates = gate_ref[...]
    y = gates[:, 0:1] * _from_chunk_rows(ybuf.at[slot, 0], tm)
    for k in range(1, TOP_K):
        y = y + gates[:, k:k + 1] * _from_chunk_rows(ybuf.at[slot, k], tm)
    is_ctx = i < n_ctx_tiles
    g2 = _mod_rows(mod_ref, 5, is_ctx)
    o_ref[...] = _layer_norm(alpha * x_ref[...] + g2 * y, lw_ref[...], lb_ref[...])


def _combine(dest_flat, gates, xs, mod, ln_w, ln_b, yb, n_ctx_tiles, alpha):
    t = xs.shape[0]
    tm = TOK_TILE
    n_tiles = t // tm
    lw2, lb2 = ln_w.reshape(1, -1), ln_b.reshape(1, -1)
    full = lambda a: pl.BlockSpec(a.shape, lambda i: (0,) * a.ndim)
    return pl.pallas_call(
        functools.partial(_combine_kernel, n_ctx_tiles=n_ctx_tiles, alpha=alpha),
        out_shape=jax.ShapeDtypeStruct((t, D_MODEL), F32),
        grid=(n_tiles,),
        in_specs=[pl.BlockSpec((tm * TOP_K,), lambda i: (i,), memory_space=pltpu.SMEM),
                  pl.BlockSpec((tm * TOP_K,), lambda i: (jnp.minimum(i + 1, n_tiles - 1),), memory_space=pltpu.SMEM),
                  pl.BlockSpec((tm, LANES), lambda i: (i, 0)),
                  pl.BlockSpec((tm, D_MODEL), lambda i: (i, 0)),
                  full(mod), full(lw2), full(lb2),
                  pl.BlockSpec(memory_space=pl.ANY)],
        out_specs=pl.BlockSpec((tm, D_MODEL), lambda i: (i, 0)),
        scratch_shapes=[pltpu.VMEM((2, TOP_K, tm * ROW_CHUNKS, LANES), F32), pltpu.SemaphoreType.DMA((2,))],
        compiler_params=_cparams(("arbitrary",)),
        name="moe_combine",
    )(dest_flat, dest_flat, gates, xs, mod, lw2, lb2, yb)


def _moe(xs, mod, layer, w_router, b_router, w_gate_up, b_gate_up, w_down, b_down, ln_w, ln_b, n_ctx_tiles, alpha):
    t = xs.shape[0]
    bm = MOE_BM
    idx, gates, rank, cnt = _router(xs, mod, w_router, b_router, n_ctx_tiles)
    counts = cnt[0]
    padded = (counts + bm - 1) // bm * bm
    pad_end = jnp.cumsum(padded).astype(jnp.int32)
    pad_start = pad_end - padded
    dest = pad_start[idx[:, :TOP_K]] + rank[:, :TOP_K]
    dest_flat = dest.reshape(-1).astype(jnp.int32)
    n_blocks = -(-(t * TOP_K) // bm) + N_EXPERTS
    blk0 = jnp.arange(n_blocks, dtype=jnp.int32) * bm
    block_exp = jnp.minimum(jnp.sum(pad_end[None, :] <= blk0[:, None], axis=1), N_EXPERTS - 1).astype(jnp.int32)
    block_valid = (blk0 < pad_end[-1]).astype(jnp.int32)
    xb = _dispatch(pad_start, pad_end, dest_flat, xs, mod, n_blocks * bm, n_ctx_tiles)
    yb = _experts(block_exp, block_valid, xb, layer, w_gate_up, b_gate_up, w_down, b_down)
    return _combine(dest_flat, gates, xs, mod, ln_w, ln_b, yb, n_ctx_tiles, alpha)


def kernel(x, c, ctx, c_ctx, w_ada, b_ada, w_in, mlstm_gate_bias, diff_lambda, diff_norm_w, mlstm_norm_w, w_out,
           ln1_w, ln1_b, w_router, b_router, w_gate_up, b_gate_up, w_down, b_down, ln2_w, ln2_b):
    assert x.shape[0] == 1 and ctx.shape[0] == 1
    depth = w_ada.shape[0]
    n_lat, n_ctx = x.shape[1], ctx.shape[1]
    assert n_ctx % TOK_TILE == 0 and n_lat % TOK_TILE == 0
    nct = n_ctx // TOK_TILE
    alpha = (2 * depth) ** 0.25

    vecs = jnp.zeros((8, D_MODEL), F32).at[0].set(c[0]).at[1].set(c_ctx)
    mods = _ada(vecs, w_ada, b_ada)
    cos, sin = _rope_tables(n_lat, n_ctx)
    xt = jnp.concatenate([ctx[0], x[0]], axis=0)

    for l in range(depth):
        need_ctx = l < depth - 1
        lam_init = 0.8 - 0.6 * math.exp(-0.3 * l)
        mod = mods[l]
        w_ext, w_vt = _prep_w_in(w_in[l])
        q, k, vt, mq, mk, mv, mo, g, cu = _inproj(xt, mod, w_ext, w_vt, cos, sin, mlstm_gate_bias[l], n_ctx)

        dl = diff_lambda[l].astype(F32)
        lam = (jnp.exp(jnp.sum(dl[0] * dl[1])) - jnp.exp(jnp.sum(dl[2] * dl[3])) + lam_init).reshape(1)
        post = 1.0 - lam_init
        attn_l = _attention(lam, q, k, vt, diff_norm_w[l], n_ctx, n_lat, n_ctx + n_lat, post)
        four_l = _fourier_lat(cu[n_ctx:])
        if need_ctx:
            attn_c = _attention(lam, q, k, vt, diff_norm_w[l], 0, n_ctx, n_ctx, post)
            four_c = _fourier_small(cu[:n_ctx])
        else:
            attn_c, four_c = attn_l, four_l
        hf, hb = _mlstm(mq, mk, mv, g, n_ctx)
        xs = _outproj(attn_l, attn_c, four_l, four_c, hf, hb, mo, xt, mod, mlstm_norm_w[l], w_out[l].astype(BF16),
                      ln1_w[l], ln1_b[l], n_ctx, need_ctx, alpha)
        xt = _moe(xs, mod, l, w_router[l], b_router[l], w_gate_up, b_gate_up, w_down, b_down, ln2_w[l], ln2_b[l],
                  nct if need_ctx else 0, alpha)
    return xt[None]
```

```python
import functools
import math

import numpy as np
import jax
import jax.numpy as jnp
from jax import lax
from jax.experimental import pallas as pl
from jax.experimental.pallas import tpu as pltpu

F32 = jnp.float32
BF16 = jnp.bfloat16
HI = lax.Precision.HIGHEST

D_MODEL = 1024
GRID_W = 64
HEAD_DIM = 64
A_HEADS = 4
A_QK_DIM = HEAD_DIM
A_V_DIM = 2 * HEAD_DIM
A_WIDTH = A_HEADS * A_V_DIM
A_SCALE = A_QK_DIM ** -0.5
B_HEADS = 4
B_HEAD_DIM = HEAD_DIM
B_WIDTH = B_HEADS * B_HEAD_DIM
C_GROUPS = 4
C_GROUP_DIM = HEAD_DIM
C_WIDTH = C_GROUPS * C_GROUP_DIM
IN_SIZES = (A_WIDTH, A_WIDTH, A_WIDTH, B_WIDTH, B_WIDTH, B_WIDTH, B_WIDTH, 4 * B_HEADS, C_WIDTH)
ROPE_BASE = 10000.0
MLSTM_CHUNK = 64
N_EXPERTS = 32
TOP_K = 4
D_EXPERT = D_MODEL
SWIGLU_LIMIT = 7.0
SWIGLU_ALPHA = 1.702
EPS = 1e-5

LANES = 128
VMEM_LIMIT = 56 * 1024 * 1024
TOK_TILE = 256
ATT_TQ = 512
ATT_UNROLL = 8
MOE_BM = 256

_OQ, _OQR, _OK, _OKR = 0, 512, 1024, 1536
_OMQ, _OMK, _OMV, _OMO, _OG, _OCU = 2048, 2304, 2560, 2816, 3072, 3200
Q_SCALE = A_SCALE * math.log2(math.e)
VT_ROWS = A_V_DIM + 16


def _cparams(sem, vmem=VMEM_LIMIT):
    return pltpu.CompilerParams(dimension_semantics=sem, vmem_limit_bytes=vmem)


def _ada_kernel(v_ref, w_ref, b_ref, o_ref):
    v = v_ref[...]
    s = v * jax.nn.sigmoid(v)
    o_ref[...] = jnp.dot(s, w_ref[...], precision=HI, preferred_element_type=F32) + b_ref[...]


def _ada(vecs, w_ada, b_ada):
    depth, d, d6 = w_ada.shape
    bn = 1024
    return pl.pallas_call(
        _ada_kernel,
        out_shape=jax.ShapeDtypeStruct((depth, 8, d6), F32),
        grid=(depth, d6 // bn),
        in_specs=[pl.BlockSpec((8, d), lambda l, j: (0, 0)),
                  pl.BlockSpec((None, d, bn), lambda l, j: (l, 0, j)),
                  pl.BlockSpec((None, 1, bn), lambda l, j: (l, 0, j))],
        out_specs=pl.BlockSpec((None, 8, bn), lambda l, j: (l, 0, j)),
        compiler_params=_cparams(("arbitrary", "arbitrary")),
        name="ada_modulation",
    )(vecs, w_ada, b_ada.reshape(depth, 1, d6))


def _mod_rows(mod_ref, k, is_ctx):
    lat = mod_ref[0:1, k * D_MODEL:(k + 1) * D_MODEL]
    ctx = mod_ref[1:2, k * D_MODEL:(k + 1) * D_MODEL]
    return jnp.where(is_ctx, ctx, lat)


def _inproj_kernel(x_ref, mod_ref, w_ref, wvt_ref, cos_ref, sin_ref, gb_ref,
                   q_ref, k_ref, vt_ref, mkt_ref, mq_ref, mk_ref, mv_ref, mo_ref, g_ref, cu_ref, *, n_ctx_tiles):
    is_ctx = pl.program_id(0) < n_ctx_tiles
    sh = _mod_rows(mod_ref, 0, is_ctx)
    sc = _mod_rows(mod_ref, 1, is_ctx)
    hm = (x_ref[...] * (1.0 + sc) + sh).astype(BF16)

    def proj(off, width):
        return jnp.dot(hm, w_ref[:, off:off + width], preferred_element_type=F32)

    cos = cos_ref[...]
    sin = sin_ref[...]
    for j in range(A_HEADS):
        o = j * LANES
        q = proj(_OQ + o, LANES) * cos + proj(_OQR + o, LANES) * sin
        q_ref[:, o:o + LANES] = (q * Q_SCALE).astype(BF16)
        k = proj(_OK + o, LANES) * cos + proj(_OKR + o, LANES) * sin
        k_ref[:, o:o + LANES] = k.astype(BF16)
    tr = lax.dot_general(wvt_ref[...], hm, (((1,), (1,)), ((), ())), preferred_element_type=F32)
    mkt_ref[...] = tr[A_WIDTH:, :] * (B_HEAD_DIM ** -0.5)
    vt = tr[:A_WIDTH, :].astype(BF16)
    for j in range(A_HEADS):
        vt_ref[j, 0:A_V_DIM, :] = vt[j * A_V_DIM:(j + 1) * A_V_DIM, :]
        vt_ref[j, A_V_DIM:, :] = jnp.ones((VT_ROWS - A_V_DIM, vt.shape[1]), BF16)
    mq_ref[...] = proj(_OMQ, B_WIDTH)
    mk_ref[...] = proj(_OMK, B_WIDTH) * (B_HEAD_DIM ** -0.5)
    mv_ref[...] = proj(_OMV, B_WIDTH)
    mo_ref[...] = proj(_OMO, B_WIDTH)
    g_ref[...] = proj(_OG, LANES) + gb_ref[...]
    cu_ref[...] = proj(_OCU, C_WIDTH)


def _rot_cols(w):
    d, width = w.shape
    wr = w.reshape(d, width // 64, 2, 2, 16)
    rot = jnp.stack([-wr[:, :, :, 1, :], wr[:, :, :, 0, :]], axis=3)
    return rot.reshape(d, width)


def _prep_w_in(w_in):
    offs = np.cumsum((0,) + IN_SIZES)
    aq, ak, av, bq, bk, bv, bo, bg, cu = [w_in[:, offs[i]:offs[i + 1]] for i in range(9)]
    bg = jnp.pad(bg, ((0, 0), (0, LANES - bg.shape[1])))
    w = jnp.concatenate([aq, _rot_cols(aq), ak, _rot_cols(ak), bq, bk, bv, bo, bg, cu], axis=1)
    return w.astype(BF16), jnp.concatenate([av, bk], axis=1).T.astype(BF16)


def _rope_tables(n_lat, n_ctx):
    rows = n_lat // GRID_W
    row = jnp.broadcast_to(jnp.arange(rows, dtype=F32)[:, None], (rows, GRID_W)).reshape(-1)
    col = jnp.broadcast_to(jnp.arange(GRID_W, dtype=F32)[None, :], (rows, GRID_W)).reshape(-1)
    axis_dim = A_QK_DIM // 2
    inv_freq = ROPE_BASE ** (-jnp.arange(0, axis_dim, 2, dtype=F32) / axis_dim)
    ang_r = row[:, None] * inv_freq
    ang_c = col[:, None] * inv_freq
    ang = jnp.concatenate([ang_r, ang_r, ang_c, ang_c], axis=-1)
    cos = jnp.concatenate([jnp.ones((n_ctx, A_QK_DIM), F32), jnp.cos(ang)], axis=0)
    sin = jnp.concatenate([jnp.zeros((n_ctx, A_QK_DIM), F32), jnp.sin(ang)], axis=0)
    return jnp.tile(cos, (1, 2)), jnp.tile(sin, (1, 2))


def _inproj(xt, mod, w_ext, w_vt, cos, sin, gate_bias, n_ctx):
    t = xt.shape[0]
    tm = TOK_TILE
    gb = jnp.pad(gate_bias.reshape(1, -1), ((0, 0), (0, LANES - 4 * B_HEADS)))
    row = lambda w: pl.BlockSpec((tm, w), lambda i: (i, 0))
    full = lambda a: pl.BlockSpec(a.shape, lambda i: (0,) * a.ndim)
    sds = jax.ShapeDtypeStruct
    rows_out = [(A_WIDTH, BF16)] * 2 + [(B_WIDTH, F32)] * 4 + [(LANES, F32), (C_WIDTH, F32)]
    out_shape = [sds((t, w), dt) for w, dt in rows_out]
    out_specs = [row(w) for w, _ in rows_out]
    out_shape.insert(2, sds((t // tm, A_HEADS, VT_ROWS, tm), BF16))
    out_specs.insert(2, pl.BlockSpec((None, A_HEADS, VT_ROWS, tm), lambda i: (i, 0, 0, 0)))
    out_shape.insert(3, sds((t // tm, B_WIDTH, tm), F32))
    out_specs.insert(3, pl.BlockSpec((None, B_WIDTH, tm), lambda i: (i, 0, 0)))
    return pl.pallas_call(
        functools.partial(_inproj_kernel, n_ctx_tiles=n_ctx // tm),
        out_shape=out_shape,
        grid=(t // tm,),
        in_specs=[row(D_MODEL), full(mod), full(w_ext), full(w_vt), row(LANES), row(LANES), full(gb)],
        out_specs=out_specs,
        compiler_params=_cparams(("arbitrary",)),
        name="in_projection",
    )(xt, mod, w_ext, w_vt, cos, sin, gb)


def _attn_kernel(lam_ref, q_ref, k_ref, vt_ref, nw_ref, o_ref, acc_ref, s_ref, *, n_kv, post_scale):
    q = q_ref[...]
    tq = q.shape[0]
    tk = vt_ref.shape[-1]
    lane = lax.broadcasted_iota(jnp.int32, q.shape, 1)
    zero = jnp.zeros_like(q)
    q2 = jnp.concatenate([jnp.where(lane < A_QK_DIM, q, zero), jnp.where(lane >= A_QK_DIM, q, zero)], axis=0)
    acc_ref[...] = jnp.zeros_like(acc_ref)

    def scores(j, slot):
        kb = k_ref[pl.ds(pl.multiple_of(j * tk, tk), tk), :]
        st = lax.dot_general(kb, q2, (((1,), (1,)), ((), ())), preferred_element_type=F32)
        s_ref[slot] = st
        return jnp.max(st, axis=0, keepdims=True)

    def consume(j, slot, cmax, m):
        m_new = jnp.maximum(m, cmax)
        alpha = jnp.exp2(m - m_new)
        for h in range(2):
            cs = slice(h * tq, (h + 1) * tq)
            p = jnp.exp2(s_ref[slot, :, cs] - m_new[:, cs]).astype(BF16)
            r = jnp.dot(vt_ref[j], p, preferred_element_type=F32)
            acc_ref[:, cs] = alpha[:, cs] * acc_ref[:, cs] + r
        return m_new

    cm0 = scores(0, 0)

    def group(t, carry):
        cm, m = carry
        a = ATT_UNROLL * t
        for i in range(ATT_UNROLL):
            cm_next = scores(a + i + 1, (i + 1) % 2)
            m = consume(a + i, i % 2, cm, m)
            cm = cm_next
        return cm, m

    assert (n_kv - 1) % ATT_UNROLL == 0
    cm_last, m = lax.fori_loop(0, (n_kv - 1) // ATT_UNROLL, group, (cm0, jnp.full((1, 2 * tq), -jnp.inf, F32)))
    consume(n_kv - 1, 0, cm_last, m)
    l = acc_ref[A_V_DIM:A_V_DIM + 1, :]
    acc = acc_ref[0:A_V_DIM, :]
    ot = acc[:, :tq] * (1.0 / l[:, :tq]) - acc[:, tq:] * (lam_ref[0] / l[:, tq:])
    yt = ot * lax.rsqrt(jnp.mean(ot * ot, axis=0, keepdims=True) + EPS)
    o_ref[...] = yt.T * nw_ref[...] * post_scale


def _attention(lam, q, k, vt, norm_w, q_row0, n_q, n_keys, post_scale):
    tq = min(ATT_TQ, n_q)
    tk = vt.shape[-1]
    n_kv = n_keys // tk
    if q_row0 % tq:
        q, q_row0 = q[q_row0:q_row0 + n_q], 0
    q_blk0 = q_row0 // tq
    return pl.pallas_call(
        functools.partial(_attn_kernel, n_kv=n_kv, post_scale=post_scale),
        out_shape=jax.ShapeDtypeStruct((n_q, A_WIDTH), F32),
        grid=(A_HEADS, n_q // tq),
        in_specs=[pl.BlockSpec(memory_space=pltpu.SMEM),
                  pl.BlockSpec((tq, LANES), lambda h, i: (i + q_blk0, h)),
                  pl.BlockSpec((n_keys, LANES), lambda h, i: (0, h)),
                  pl.BlockSpec((n_kv, None, VT_ROWS, tk), lambda h, i: (0, h, 0, 0)),
                  pl.BlockSpec((1, LANES), lambda h, i: (0, h))],
        out_specs=pl.BlockSpec((tq, LANES), lambda h, i: (i, h)),
        scratch_shapes=[pltpu.VMEM((VT_ROWS, 2 * tq), F32), pltpu.VMEM((2, tk, 2 * tq), F32)],
        compiler_params=_cparams(("arbitrary", "arbitrary")),
        name="diff_attention",
    )(lam, q, k, vt, norm_w.reshape(1, A_WIDTH))


def _mlstm_kernel(qf, kf, ktf, vf, gf, qb, kb, ktb, vb, gb, of_ref, ob_ref, s_ref, m_ref, *, chunks):
    L = MLSTM_CHUNK
    tm = chunks * L
    neg = -jnp.inf

    @pl.when(pl.program_id(0) == 0)
    def _():
        s_ref[...] = jnp.zeros_like(s_ref)
        m_ref[...] = jnp.zeros_like(m_ref)

    tr = lax.broadcasted_iota(jnp.int32, (tm, tm), 0)
    tc = lax.broadcasted_iota(jnp.int32, (tm, tm), 1)
    same_chunk = tr // L == tc // L
    masks = (same_chunk & (tr >= tc), same_chunk & (tr <= tc))
    row_chunk = lax.broadcasted_iota(jnp.int32, (tm, 1), 0) // L
    col_chunk = lax.broadcasted_iota(jnp.int32, (1, tm), 1) // L
    lane_half = lax.broadcasted_iota(jnp.int32, (tm, LANES), 1) // B_HEAD_DIM
    row_half = lax.broadcasted_iota(jnp.int32, (LANES, 1), 0) // B_HEAD_DIM
    rr = lax.broadcasted_iota(jnp.int32, (LANES, 2 * LANES), 0) // B_HEAD_DIM
    cc = lax.broadcasted_iota(jnp.int32, (LANES, 2 * LANES), 1)
    blockmask = jnp.where(cc < LANES, cc // B_HEAD_DIM, cc - LANES) == rr
    aug = jnp.where(lax.broadcasted_iota(jnp.int32, (tm, LANES), 1) < 2, 1.0, 0.0).astype(F32)
    io = ((qf, kf, ktf, vf, gf, of_ref), (qb, kb, ktb, vb, gb, ob_ref))

    def by_chunk(vals, chunk_ids):
        out = vals[chunks - 1]
        for c in range(chunks - 2, -1, -1):
            out = jnp.where(chunk_ids == c, vals[c], out)
        return out

    dp = [(d, p) for d in range(2) for p in range(2)]
    dph = [(d, p, hh) for d, p in dp for hh in range(2)]

    g_tile, b_tile, g_t, b_t = {}, {}, {}, {}
    for d in range(2):
        g_tile[d] = io[d][4][...]
        b_tile[d] = jnp.dot(masks[d].astype(F32), jax.nn.log_sigmoid(g_tile[d]), precision=HI,
                            preferred_element_type=F32)
        g_t[d] = g_tile[d].T
        b_t[d] = b_tile[d].T
    qp, kp, v_aug, ktp = {}, {}, {}, {}
    for d, p in dp:
        sl = slice(p * LANES, (p + 1) * LANES)
        qp[d, p] = io[d][0][:, sl]
        kp[d, p] = io[d][1][:, sl].astype(BF16)
        ktp[d, p] = io[d][2][sl, :]
        v_aug[d, p] = jnp.concatenate([io[d][3][:, sl], aug], axis=1).astype(BF16)

    b_col, b_row, i_row, gs, a_row, m_loc, w_t = {}, {}, {}, {}, {}, {}, {}
    for d, p, hh in dph:
        h = 2 * p + hh
        il, fl = 4 * d + h, 8 + 4 * d + h
        last = L - 1 if d == 0 else 0
        b_col[d, p, hh] = b_tile[d][:, fl:fl + 1]
        b_row[d, p, hh] = b_t[d][fl:fl + 1, :]
        i_row[d, p, hh] = g_t[d][il:il + 1, :]
        gs[d, p, hh] = [b_row[d, p, hh][:, c * L + last:c * L + last + 1] for c in range(chunks)]
        a_row[d, p, hh] = by_chunk(gs[d, p, hh], col_chunk) - b_row[d, p, hh] + i_row[d, p, hh]
    for k3 in dph:
        m_loc[k3] = [jnp.max(jnp.where(col_chunk == c, a_row[k3], neg), axis=-1, keepdims=True)
                     for c in range(chunks)]
    for k3 in dph:
        w_t[k3] = jnp.exp(a_row[k3] - by_chunk(m_loc[k3], col_chunk))

    upd, sqk = {}, {}
    for d, p in dp:
        kw_t = ktp[d, p] * jnp.where(row_half == 0, w_t[d, p, 0], w_t[d, p, 1])
        for c in range(chunks):
            kw_c = jnp.where(col_chunk == c, kw_t, 0.0).astype(BF16)
            upd[d, p, c] = jnp.where(blockmask, jnp.dot(kw_c, v_aug[d, p], preferred_element_type=F32), 0.0)
    for d, p, hh in dph:
        qm = jnp.where(lane_half == hh, qp[d, p], 0.0).astype(BF16)
        sqk[d, p, hh] = lax.dot_general(qm, kp[d, p], (((1,), (1,)), ((), ())), preferred_element_type=F32)

    m_old = m_ref[...]
    orders = (range(chunks), range(chunks - 1, -1, -1))
    m_at, dec, inj, m_end = {}, {}, {}, {}
    for d, p, hh in dph:
        row = 4 * d + 2 * p + hh
        m_prev = m_old[row:row + 1, 0:1]
        for c in orders[d]:
            m_at[d, p, hh, c] = m_prev
            g_c = gs[d, p, hh][c]
            m_new = jnp.maximum(g_c + m_prev, m_loc[d, p, hh][c])
            dec[d, p, hh, c] = jnp.exp(g_c + m_prev - m_new)
            inj[d, p, hh, c] = jnp.exp(m_loc[d, p, hh][c] - m_new)
            m_prev = m_new
        m_end[row] = m_prev
    state_at, state_end = {}, {}
    for d, p in dp:
        state = s_ref[d, p]
        for c in orders[d]:
            state_at[d, p, c] = state
            state = (jnp.where(rr == 0, dec[d, p, 0, c], dec[d, p, 1, c]) * state
                     + jnp.where(rr == 0, inj[d, p, 0, c], inj[d, p, 1, c]) * upd[d, p, c])
        state_end[d, p] = state
    for d, p in dp:
        s_ref[d, p] = state_end[d, p]
    m_rows = lax.broadcasted_iota(jnp.int32, m_old.shape, 0)
    m_out = m_old
    for row, val in m_end.items():
        m_out = jnp.where(m_rows == row, val, m_out)
    m_ref[...] = m_out

    dmat = {k3: jnp.where(masks[k3[0]], b_col[k3] + (i_row[k3] - b_row[k3]), neg) for k3 in dph}
    m_row = {k3: jnp.max(dmat[k3], axis=-1, keepdims=True) for k3 in dph}
    s_loc = {k3: (sqk[k3] * jnp.exp(dmat[k3] - m_row[k3])).astype(BF16) for k3 in dph}

    r = {}
    for d, p, hh in dph:
        r[d, p, hh] = jnp.dot(s_loc[d, p, hh], v_aug[d, p], preferred_element_type=F32)
    q2 = {}
    for d, p in dp:
        q2[d, p] = jnp.concatenate(
            [jnp.dot(qp[d, p][c * L:(c + 1) * L].astype(BF16), state_at[d, p, c].astype(BF16),
                     preferred_element_type=F32) for c in range(chunks)], axis=0)

    inter = {k3: b_col[k3] + by_chunk([m_at[k3 + (c,)] for c in range(chunks)], row_chunk) for k3 in dph}
    m_col = {k3: jnp.maximum(inter[k3], m_row[k3]) for k3 in dph}
    e_loc = {k3: jnp.exp(m_row[k3] - m_col[k3]) for k3 in dph}
    e_int = {k3: jnp.exp(inter[k3] - m_col[k3]) for k3 in dph}
    floor = {k3: jnp.exp(-m_col[k3]) for k3 in dph}
    den = {(d, p, hh): e_loc[d, p, hh] * r[d, p, hh][:, LANES:LANES + 1]
           + e_int[d, p, hh] * q2[d, p][:, LANES + hh:LANES + hh + 1] for d, p, hh in dph}
    inv = {k3: 1.0 / jnp.maximum(jnp.abs(den[k3]), floor[k3]) for k3 in dph}
    a_loc = {k3: e_loc[k3] * inv[k3] for k3 in dph}
    a_int = {k3: e_int[k3] * inv[k3] for k3 in dph}
    head_out = {(d, p, hh): a_loc[d, p, hh] * r[d, p, hh][:, :LANES] + a_int[d, p, hh] * q2[d, p][:, :LANES]
                for d, p, hh in dph}
    for d in range(2):
        io[d][5][...] = jnp.concatenate(
            [jnp.where(lane_half == 0, head_out[d, p, 0], head_out[d, p, 1]) for p in range(2)], axis=1)


def _mlstm(mq, mk, mkt, mv, g, n_ctx):
    t = mq.shape[0]
    tm = TOK_TILE
    n_tiles = t // tm
    n_ctx_tiles = n_ctx // tm
    mirror = lambda i: jnp.where(i < n_ctx_tiles, n_ctx_tiles - 1 - i, n_tiles - 1 - (i - n_ctx_tiles))
    fwd = lambda i: (i, 0)
    bwd = lambda i: (mirror(i), 0)
    spec = lambda w, im: pl.BlockSpec((tm, w), im)
    tspec = lambda f: pl.BlockSpec((None, B_WIDTH, tm), lambda i: (f(i), 0, 0))
    ins = ([spec(B_WIDTH, fwd)] * 2 + [tspec(lambda i: i), spec(B_WIDTH, fwd), spec(LANES, fwd)]
           + [spec(B_WIDTH, bwd)] * 2 + [tspec(mirror), spec(B_WIDTH, bwd), spec(LANES, bwd)])
    return pl.pallas_call(
        functools.partial(_mlstm_kernel, chunks=tm // MLSTM_CHUNK),
        out_shape=[jax.ShapeDtypeStruct((t, B_WIDTH), F32)] * 2,
        grid=(n_tiles,),
        in_specs=ins,
        out_specs=[spec(B_WIDTH, fwd), spec(B_WIDTH, bwd)],
        scratch_shapes=[pltpu.VMEM((2, 2, LANES, 2 * LANES), F32), pltpu.VMEM((8, LANES), F32)],
        compiler_params=_cparams(("arbitrary",)),
        name="mlstm_scan",
    )(mq, mk, mkt, mv, g, mq, mk, mkt, mv, g)


def _dft_cs(n):
    j = np.arange(n, dtype=np.int64)
    ang = 2.0 * np.pi * ((j[:, None] * j[None, :]) % n).astype(np.float64) / n
    s = 1.0 / math.sqrt(n)
    return np.cos(ang) * s, np.sin(ang) * s


def _chan_mats():
    c, s = _dft_cs(C_GROUP_DIM)
    eye = np.eye(C_GROUPS)
    return np.concatenate([np.kron(eye, c), np.kron(eye, s)], axis=0).astype(np.float32)


def _fft1_kernel(fa_ref, u_ref, z_ref):
    z_ref[...] = jnp.dot(fa_ref[...], u_ref[...], precision=HI, preferred_element_type=F32)


def _fft2_kernel(z_ref, twc_ref, tws_ref, fb_ref, ch_ref, o_ref, *, kb):
    n2 = fb_ref.shape[0] // 2
    for j in range(kb):
        zr = z_ref[0, j]
        zi = z_ref[1, j]
        tc = twc_ref[j]
        ts = tws_ref[j]
        st = jnp.concatenate([zr * tc + zi * ts, zi * tc - zr * ts], axis=0)
        a = jnp.dot(fb_ref[...], st, precision=HI, preferred_element_type=F32)
        ari = jnp.concatenate([a[:n2], a[n2:]], axis=1)
        o_ref[:, j * C_WIDTH:(j + 1) * C_WIDTH] = jnp.dot(ari, ch_ref[...], precision=HI,
                                                         preferred_element_type=F32)


def _fourier_lat(u):
    n = u.shape[0]
    n1 = int(round(math.sqrt(n)))
    assert n1 * n1 == n
    n2 = n1
    c, s = _dft_cs(n1)
    fa = jnp.asarray(np.concatenate([c, -s], axis=0).astype(np.float32))
    fb = jnp.asarray(np.block([[c, s], [-s, c]]).astype(np.float32))
    k1 = np.arange(n1, dtype=np.int64)
    ang = 2.0 * np.pi * ((k1[:, None] * k1[None, :]) % n).astype(np.float64) / n
    twc = jnp.asarray(np.cos(ang).astype(np.float32)).reshape(n1, n2, 1)
    tws = jnp.asarray(np.sin(ang).astype(np.float32)).reshape(n1, n2, 1)
    ch = jnp.asarray(_chan_mats())
    cols = n2 * C_WIDTH
    tb = min(2048, cols)
    z = pl.pallas_call(
        _fft1_kernel,
        out_shape=jax.ShapeDtypeStruct((2 * n1, cols), F32),
        grid=(cols // tb,),
        in_specs=[pl.BlockSpec((2 * n1, n1), lambda i: (0, 0)), pl.BlockSpec((n1, tb), lambda i: (0, i))],
        out_specs=pl.BlockSpec((2 * n1, tb), lambda i: (0, i)),
        compiler_params=_cparams(("arbitrary",)),
        name="fourier_stage1",
    )(fa, u.reshape(n1, cols))
    kb = 8
    out = pl.pallas_call(
        functools.partial(_fft2_kernel, kb=kb),
        out_shape=jax.ShapeDtypeStruct((n2, n1 * C_WIDTH), F32),
        grid=(n1 // kb,),
        in_specs=[pl.BlockSpec((2, kb, n2, C_WIDTH), lambda i: (0, i, 0, 0)),
                  pl.BlockSpec((kb, n2, 1), lambda i: (i, 0, 0)),
                  pl.BlockSpec((kb, n2, 1), lambda i: (i, 0, 0)),
                  pl.BlockSpec((2 * n2, 2 * n2), lambda i: (0, 0)),
                  pl.BlockSpec((2 * C_WIDTH, C_WIDTH), lambda i: (0, 0))],
        out_specs=pl.BlockSpec((n2, kb * C_WIDTH), lambda i: (0, i)),
        compiler_params=_cparams(("arbitrary",)),
        name="fourier_stage2",
    )(z.reshape(2, n1, n2, C_WIDTH), twc, tws, fb, ch)
    return out.reshape(n, C_WIDTH)


def _fft_small_kernel(fa_ref, u_ref, ch_ref, o_ref):
    n = u_ref.shape[0]
    a = jnp.dot(fa_ref[...], u_ref[...], precision=HI, preferred_element_type=F32)
    ari = jnp.concatenate([a[:n], a[n:]], axis=1)
    o_ref[...] = jnp.dot(ari, ch_ref[...], precision=HI, preferred_element_type=F32)


def _fourier_small(u):
    n = u.shape[0]
    c, s = _dft_cs(n)
    fa = jnp.asarray(np.concatenate([c, -s], axis=0).astype(np.float32))
    return pl.pallas_call(
        _fft_small_kernel,
        out_shape=jax.ShapeDtypeStruct((n, C_WIDTH), F32),
        compiler_params=_cparams(None),
        name="fourier_small",
    )(fa, u, jnp.asarray(_chan_mats()))


def _layer_norm(z, w, b):
    mu = jnp.mean(z, axis=-1, keepdims=True)
    zc = z - mu
    var = jnp.mean(zc * zc, axis=-1, keepdims=True)
    return zc * lax.rsqrt(var + EPS) * w + b


def _outproj_kernel(al_ref, ac_ref, fl_ref, fc_ref, hf_ref, hb_ref, bo_ref, x_ref, mod_ref, mnw_ref, seg_ref,
                    w_ref, lw_ref, lb_ref, o_ref, *, n_ctx_tiles, alpha):
    is_ctx = pl.program_id(0) < n_ctx_tiles
    attn = jnp.where(is_ctx, ac_ref[...], al_ref[...])
    four = jnp.where(is_ctx, fc_ref[...], fl_ref[...])
    h = hf_ref[...] + hb_ref[...]
    ms = jnp.dot(h * h, seg_ref[...], precision=HI, preferred_element_type=F32)
    y = h * lax.rsqrt(ms + EPS) * mnw_ref[...] * jax.nn.sigmoid(bo_ref[...])
    mix = (jnp.dot(attn.astype(BF16), w_ref[0:A_WIDTH, :], preferred_element_type=F32)
           + jnp.dot(y.astype(BF16), w_ref[A_WIDTH:A_WIDTH + B_WIDTH, :], preferred_element_type=F32)
           + jnp.dot(four.astype(BF16), w_ref[A_WIDTH + B_WIDTH:, :], preferred_element_type=F32))
    g1 = _mod_rows(mod_ref, 2, is_ctx)
    o_ref[...] = _layer_norm(alpha * x_ref[...] + g1 * mix, lw_ref[...], lb_ref[...])


def _outproj(attn_l, attn_c, four_l, four_c, hf, hb, bo, xt, mod, mnw, w_out_bf, ln_w, ln_b, n_ctx, with_ctx, alpha):
    tm = TOK_TILE
    nct = n_ctx // tm
    n_lat_tiles = attn_l.shape[0] // tm
    if with_ctx:
        n_tiles, uoff, n_ctx_tiles = n_lat_tiles + nct, 0, nct
        lat = lambda i: (jnp.maximum(i - nct, 0), 0)
    else:
        n_tiles, uoff, n_ctx_tiles = n_lat_tiles, nct, 0
        lat = lambda i: (i, 0)
    uni = lambda i: (i + uoff, 0)
    ctxm = lambda i: (jnp.minimum(i, nct - 1), 0)
    seg = jnp.asarray(np.kron(np.eye(B_HEADS), np.full((B_HEAD_DIM, B_HEAD_DIM), 1.0 / B_HEAD_DIM)).astype(np.float32))
    full = lambda a: pl.BlockSpec(a.shape, lambda i: (0,) * a.ndim)
    mnw2, lw2, lb2 = mnw.reshape(1, -1), ln_w.reshape(1, -1), ln_b.reshape(1, -1)
    return pl.pallas_call(
        functools.partial(_outproj_kernel, n_ctx_tiles=n_ctx_tiles, alpha=alpha),
        out_shape=jax.ShapeDtypeStruct((n_tiles * tm, D_MODEL), F32),
        grid=(n_tiles,),
        in_specs=[pl.BlockSpec((tm, A_WIDTH), lat), pl.BlockSpec((tm, A_WIDTH), ctxm),
                  pl.BlockSpec((tm, C_WIDTH), lat), pl.BlockSpec((tm, C_WIDTH), ctxm),
                  pl.BlockSpec((tm, B_WIDTH), uni), pl.BlockSpec((tm, B_WIDTH), uni),
                  pl.BlockSpec((tm, B_WIDTH), uni), pl.BlockSpec((tm, D_MODEL), uni),
                  full(mod), full(mnw2), full(seg), full(w_out_bf), full(lw2), full(lb2)],
        out_specs=pl.BlockSpec((tm, D_MODEL), lambda i: (i, 0)),
        compiler_params=_cparams(("arbitrary",)),
        name="out_projection",
    )(attn_l, attn_c, four_l, four_c, hf, hb, bo, xt, mod, mnw2, seg, w_out_bf, lw2, lb2)


def _router_kernel(x_ref, mod_ref, wr_ref, br_ref, idx_ref, gate_ref, rank_ref, cnt_ref, run_ref, *, n_ctx_tiles):
    i = pl.program_id(0)

    @pl.when(i == 0)
    def _():
        run_ref[...] = jnp.zeros_like(run_ref)

    is_ctx = i < n_ctx_tiles
    f_in = x_ref[...] * (1.0 + _mod_rows(mod_ref, 4, is_ctx)) + _mod_rows(mod_ref, 3, is_ctx)
    logits = jnp.dot(f_in, wr_ref[...], precision=HI, preferred_element_type=F32) + br_ref[...]
    tm = logits.shape[0]
    eidx = lax.broadcasted_iota(jnp.int32, logits.shape, 1).astype(F32)
    lane = lax.broadcasted_iota(jnp.int32, (tm, LANES), 1)
    work = logits
    vals, sels = [], []
    for _ in range(TOP_K):
        mx = jnp.max(work, axis=-1, keepdims=True)
        sel = jnp.min(jnp.where(work == mx, eidx, float(N_EXPERTS)), axis=-1, keepdims=True)
        vals.append(mx)
        sels.append(sel)
        work = jnp.where(eidx == sel, -jnp.inf, work)
    es = [jnp.exp(v - vals[0]) for v in vals]
    tot = es[0] + es[1] + es[2] + es[3]
    onehots = [(eidx == s).astype(F32) for s in sels]
    oh_all = onehots[0] + onehots[1] + onehots[2] + onehots[3]
    r = lax.broadcasted_iota(jnp.int32, (tm, tm), 0)
    c = lax.broadcasted_iota(jnp.int32, (tm, tm), 1)
    before = jnp.dot((r > c).astype(BF16), oh_all.astype(BF16), preferred_element_type=F32) + run_ref[0:1, :]
    idx_o = jnp.zeros((tm, LANES), jnp.int32)
    gate_o = jnp.zeros((tm, LANES), F32)
    rank_o = jnp.zeros((tm, LANES), jnp.int32)
    for k in range(TOP_K):
        rank = jnp.sum(onehots[k] * before, axis=-1, keepdims=True).astype(jnp.int32)
        idx_o = jnp.where(lane == k, sels[k].astype(jnp.int32), idx_o)
        gate_o = jnp.where(lane == k, es[k] / tot, gate_o)
        rank_o = jnp.where(lane == k, rank, rank_o)
    idx_ref[...] = idx_o
    gate_ref[...] = gate_o
    rank_ref[...] = rank_o
    run_new = run_ref[0:1, :] + jnp.sum(oh_all, axis=0, keepdims=True)
    run_ref[...] = jnp.broadcast_to(run_new, run_ref.shape)
    cnt_ref[...] = jnp.broadcast_to(run_new, cnt_ref.shape).astype(jnp.int32)


def _router(xs, mod, w_router, b_router, n_ctx_tiles):
    t = xs.shape[0]
    tm = TOK_TILE
    br = b_router.reshape(1, -1)
    full = lambda a: pl.BlockSpec(a.shape, lambda i: (0,) * a.ndim)
    row = lambda w: pl.BlockSpec((tm, w), lambda i: (i, 0))
    return pl.pallas_call(
        functools.partial(_router_kernel, n_ctx_tiles=n_ctx_tiles),
        out_shape=[jax.ShapeDtypeStruct((t, LANES), jnp.int32), jax.ShapeDtypeStruct((t, LANES), F32),
                   jax.ShapeDtypeStruct((t, LANES), jnp.int32), jax.ShapeDtypeStruct((8, N_EXPERTS), jnp.int32)],
        grid=(t // tm,),
        in_specs=[row(D_MODEL), full(mod), full(w_router), full(br)],
        out_specs=[row(LANES), row(LANES), row(LANES), pl.BlockSpec((8, N_EXPERTS), lambda i: (0, 0))],
        scratch_shapes=[pltpu.VMEM((8, N_EXPERTS), F32)],
        compiler_params=_cparams(("arbitrary",)),
        name="moe_router",
    )(xs, mod, w_router, br)


ROW_CHUNKS = D_MODEL // LANES
DMA_UNROLL = 8


def _to_chunk_rows(ref, val):
    m = val.shape[0]
    for c in range(ROW_CHUNKS):
        ref[pl.ds(c, m, stride=ROW_CHUNKS), :] = val[:, c * LANES:(c + 1) * LANES]


def _from_chunk_rows(ref, m):
    return jnp.concatenate([ref[pl.ds(c, m, stride=ROW_CHUNKS), :] for c in range(ROW_CHUNKS)], axis=1)


def _dispatch_kernel(pstart_ref, pend_ref, dest_ref, x_ref, mod_ref, xb_ref, fbuf, zbuf, sem, zsem, *, n_ctx_tiles, bm):
    i = pl.program_id(0)
    tm = x_ref.shape[0]

    @pl.when(i == 0)
    def _():
        zbuf[...] = jnp.zeros_like(zbuf)

        def zcopy(e):
            row0 = pl.multiple_of((pend_ref[e] - bm) * ROW_CHUNKS, bm * ROW_CHUNKS)
            return pltpu.make_async_copy(zbuf, xb_ref.at[pl.ds(row0, bm * ROW_CHUNKS)], zsem)

        def zstart(e, c):
            @pl.when(pend_ref[e] > pstart_ref[e])
            def _():
                zcopy(e).start()
            return c

        def zwait(e, c):
            @pl.when(pend_ref[e] > pstart_ref[e])
            def _():
                zcopy(e).wait()
            return c

        lax.fori_loop(0, N_EXPERTS, zstart, 0)
        lax.fori_loop(0, N_EXPERTS, zwait, 0)

        def tcopy(b):
            row0 = pl.multiple_of(b * (bm * ROW_CHUNKS), bm * ROW_CHUNKS)
            return pltpu.make_async_copy(zbuf, xb_ref.at[pl.ds(row0, bm * ROW_CHUNKS)], zsem)

        def tstart(b, c):
            tcopy(b).start()
            return c

        def twait(b, c):
            tcopy(b).wait()
            return c

        tail0 = pend_ref[N_EXPERTS - 1] // bm
        n_blocks = xb_ref.shape[0] // (bm * ROW_CHUNKS)
        lax.fori_loop(tail0, n_blocks, tstart, 0)
        lax.fori_loop(tail0, n_blocks, twait, 0)

    is_ctx = i < n_ctx_tiles
    f = x_ref[...] * (1.0 + _mod_rows(mod_ref, 4, is_ctx)) + _mod_rows(mod_ref, 3, is_ctx)
    slot = i % 2
    _to_chunk_rows(fbuf.at[slot], f)

    def start(t, c):
        src = pl.multiple_of(t * ROW_CHUNKS, ROW_CHUNKS)
        for k in range(TOP_K):
            dst = pl.multiple_of(dest_ref[t * TOP_K + k] * ROW_CHUNKS, ROW_CHUNKS)
            pltpu.make_async_copy(fbuf.at[slot, pl.ds(src, ROW_CHUNKS)], xb_ref.at[pl.ds(dst, ROW_CHUNKS)],
                                  sem.at[slot]).start(priority=k % 2)
        return c

    lax.fori_loop(0, tm, start, 0, unroll=DMA_UNROLL)

    def wait_tile(s):
        for _ in range(TOP_K):
            pltpu.make_async_copy(fbuf.at[s], xb_ref.at[pl.ds(0, tm * ROW_CHUNKS)], sem.at[s]).wait()

    @pl.when(i > 0)
    def _():
        wait_tile(1 - slot)

    @pl.when(i == pl.num_programs(0) - 1)
    def _():
        wait_tile(slot)


def _dispatch(pad_start, pad_end, dest_flat, xs, mod, n_rows, n_ctx_tiles):
    t = xs.shape[0]
    tm = TOK_TILE
    gs = pltpu.PrefetchScalarGridSpec(
        num_scalar_prefetch=2,
        grid=(t // tm,),
        in_specs=[pl.BlockSpec((tm * TOP_K,), lambda i, *_: (i,), memory_space=pltpu.SMEM),
                  pl.BlockSpec((tm, D_MODEL), lambda i, *_: (i, 0)),
                  pl.BlockSpec(mod.shape, lambda i, *_: (0, 0))],
        out_specs=pl.BlockSpec(memory_space=pl.ANY),
        scratch_shapes=[pltpu.VMEM((2, tm * ROW_CHUNKS, LANES), F32), pltpu.VMEM((MOE_BM * ROW_CHUNKS, LANES), F32),
                        pltpu.SemaphoreType.DMA((2,)), pltpu.SemaphoreType.DMA],
    )
    return pl.pallas_call(
        functools.partial(_dispatch_kernel, n_ctx_tiles=n_ctx_tiles, bm=MOE_BM),
        out_shape=jax.ShapeDtypeStruct((n_rows * ROW_CHUNKS, LANES), F32),
        grid_spec=gs,
        compiler_params=_cparams(("arbitrary",)),
        name="moe_dispatch",
    )(pad_start, pad_end, dest_flat, xs, mod)


def _expert_kernel(bexp_ref, bvalid_ref, x_ref, wgu_ref, bgu_ref, wd_ref, bd_ref, y_ref, wgu_bf, wd_bf):
    i = pl.program_id(0)
    e = bexp_ref[i]
    prev = bexp_ref[jnp.maximum(i - 1, 0)]

    @pl.when(jnp.logical_or(i == 0, e != prev))
    def _():
        wgu_bf[...] = wgu_ref[...].astype(BF16)
        wd_bf[...] = wd_ref[...].astype(BF16)

    @pl.when(bvalid_ref[i] > 0)
    def _():
        bm = y_ref.shape[0] // ROW_CHUNKS
        x = _from_chunk_rows(x_ref, bm).astype(BF16)
        gu = jnp.dot(x, wgu_bf[...], preferred_element_type=F32) + bgu_ref[...]
        gate = jnp.minimum(gu[:, :D_EXPERT], SWIGLU_LIMIT)
        up = jnp.clip(gu[:, D_EXPERT:], -SWIGLU_LIMIT, SWIGLU_LIMIT)
        act = gate * jax.nn.sigmoid(SWIGLU_ALPHA * gate) * (up + 1.0)
        _to_chunk_rows(y_ref, jnp.dot(act.astype(BF16), wd_bf[...], preferred_element_type=F32) + bd_ref[...])

    @pl.when(bvalid_ref[i] == 0)
    def _():
        y_ref[...] = jnp.zeros_like(y_ref)


def _experts(block_exp, block_valid, xb, layer, w_gate_up, b_gate_up, w_down, b_down):
    n_rows = xb.shape[0] // ROW_CHUNKS
    bm = MOE_BM
    depth = w_gate_up.shape[0]
    bgu = b_gate_up.reshape(depth, N_EXPERTS, 1, 2 * D_EXPERT)
    bd = b_down.reshape(depth, N_EXPERTS, 1, D_MODEL)
    gs = pltpu.PrefetchScalarGridSpec(
        num_scalar_prefetch=2,
        grid=(n_rows // bm,),
        in_specs=[pl.BlockSpec((bm * ROW_CHUNKS, LANES), lambda i, be, bv: (i, 0)),
                  pl.BlockSpec((None, None, D_MODEL, 2 * D_EXPERT), lambda i, be, bv: (layer, be[i], 0, 0)),
                  pl.BlockSpec((None, None, 1, 2 * D_EXPERT), lambda i, be, bv: (layer, be[i], 0, 0)),
                  pl.BlockSpec((None, None, D_EXPERT, D_MODEL), lambda i, be, bv: (layer, be[i], 0, 0)),
                  pl.BlockSpec((None, None, 1, D_MODEL), lambda i, be, bv: (layer, be[i], 0, 0))],
        out_specs=pl.BlockSpec((bm * ROW_CHUNKS, LANES), lambda i, be, bv: (i, 0)),
        scratch_shapes=[pltpu.VMEM((D_MODEL, 2 * D_EXPERT), BF16), pltpu.VMEM((D_EXPERT, D_MODEL), BF16)],
    )
    return pl.pallas_call(
        _expert_kernel,
        out_shape=jax.ShapeDtypeStruct((n_rows * ROW_CHUNKS, LANES), F32),
        grid_spec=gs,
        compiler_params=_cparams(("arbitrary",)),
        name="moe_experts",
    )(block_exp, block_valid, xb, w_gate_up, bgu, w_down, bd)


def _combine_kernel(dest_ref, dnext_ref, gate_ref, x_ref, mod_ref, lw_ref, lb_ref, yb_ref, o_ref, ybuf, sem, *,
                    n_ctx_tiles, alpha):
    i = pl.program_id(0)
    tm = x_ref.shape[0]
    slot = i % 2

    def gather(d_ref, s):
        def start(t, c):
            dst = pl.multiple_of(t * ROW_CHUNKS, ROW_CHUNKS)
            for k in range(TOP_K):
                src = pl.multiple_of(d_ref[t * TOP_K + k] * ROW_CHUNKS, ROW_CHUNKS)
                pltpu.make_async_copy(yb_ref.at[pl.ds(src, ROW_CHUNKS)], ybuf.at[s, k, pl.ds(dst, ROW_CHUNKS)],
                                      sem.at[s]).start(priority=k % 2)
            return c

        lax.fori_loop(0, tm, start, 0, unroll=DMA_UNROLL)

    @pl.when(i == 0)
    def _():
        gather(dest_ref, 0)

    @pl.when(i + 1 < pl.num_programs(0))
    def _():
        gather(dnext_ref, 1 - slot)

    for k in range(TOP_K):
        pltpu.make_async_copy(yb_ref.at[pl.ds(0, tm * ROW_CHUNKS)], ybuf.at[slot, k], sem.at[slot]).wait()
    gates = gate_ref[...]
    y = gates[:, 0:1] * _from_chunk_rows(ybuf.at[slot, 0], tm)
    for k in range(1, TOP_K):
        y = y + gates[:, k:k + 1] * _from_chunk_rows(ybuf.at[slot, k], tm)
    is_ctx = i < n_ctx_tiles
    g2 = _mod_rows(mod_ref, 5, is_ctx)
    o_ref[...] = _layer_norm(alpha * x_ref[...] + g2 * y, lw_ref[...], lb_ref[...])


def _combine(dest_flat, gates, xs, mod, ln_w, ln_b, yb, n_ctx_tiles, alpha):
    t = xs.shape[0]
    tm = TOK_TILE
    n_tiles = t // tm
    lw2, lb2 = ln_w.reshape(1, -1), ln_b.reshape(1, -1)
    full = lambda a: pl.BlockSpec(a.shape, lambda i: (0,) * a.ndim)
    return pl.pallas_call(
        functools.partial(_combine_kernel, n_ctx_tiles=n_ctx_tiles, alpha=alpha),
        out_shape=jax.ShapeDtypeStruct((t, D_MODEL), F32),
        grid=(n_tiles,),
        in_specs=[pl.BlockSpec((tm * TOP_K,), lambda i: (i,), memory_space=pltpu.SMEM),
                  pl.BlockSpec((tm * TOP_K,), lambda i: (jnp.minimum(i + 1, n_tiles - 1),), memory_space=pltpu.SMEM),
                  pl.BlockSpec((tm, LANES), lambda i: (i, 0)),
                  pl.BlockSpec((tm, D_MODEL), lambda i: (i, 0)),
                  full(mod), full(lw2), full(lb2),
                  pl.BlockSpec(memory_space=pl.ANY)],
        out_specs=pl.BlockSpec((tm, D_MODEL), lambda i: (i, 0)),
        scratch_shapes=[pltpu.VMEM((2, TOP_K, tm * ROW_CHUNKS, LANES), F32), pltpu.SemaphoreType.DMA((2,))],
        compiler_params=_cparams(("arbitrary",)),
        name="moe_combine",
    )(dest_flat, dest_flat, gates, xs, mod, lw2, lb2, yb)


def _moe(xs, mod, layer, w_router, b_router, w_gate_up, b_gate_up, w_down, b_down, ln_w, ln_b, n_ctx_tiles, alpha):
    t = xs.shape[0]
    bm = MOE_BM
    idx, gates, rank, cnt = _router(xs, mod, w_router, b_router, n_ctx_tiles)
    counts = cnt[0]
    padded = (counts + bm - 1) // bm * bm
    pad_end = jnp.cumsum(padded).astype(jnp.int32)
    pad_start = pad_end - padded
    dest = pad_start[idx[:, :TOP_K]] + rank[:, :TOP_K]
    dest_flat = dest.reshape(-1).astype(jnp.int32)
    n_blocks = -(-(t * TOP_K) // bm) + N_EXPERTS
    blk0 = jnp.arange(n_blocks, dtype=jnp.int32) * bm
    block_exp = jnp.minimum(jnp.sum(pad_end[None, :] <= blk0[:, None], axis=1), N_EXPERTS - 1).astype(jnp.int32)
    block_valid = (blk0 < pad_end[-1]).astype(jnp.int32)
    xb = _dispatch(pad_start, pad_end, dest_flat, xs, mod, n_blocks * bm, n_ctx_tiles)
    yb = _experts(block_exp, block_valid, xb, layer, w_gate_up, b_gate_up, w_down, b_down)
    return _combine(dest_flat, gates, xs, mod, ln_w, ln_b, yb, n_ctx_tiles, alpha)


def kernel(x, c, ctx, c_ctx, w_ada, b_ada, w_in, mlstm_gate_bias, diff_lambda, diff_norm_w, mlstm_norm_w, w_out,
           ln1_w, ln1_b, w_router, b_router, w_gate_up, b_gate_up, w_down, b_down, ln2_w, ln2_b):
    assert x.shape[0] == 1 and ctx.shape[0] == 1
    depth = w_ada.shape[0]
    n_lat, n_ctx = x.shape[1], ctx.shape[1]
    assert n_ctx % TOK_TILE == 0 and n_lat % TOK_TILE == 0
    nct = n_ctx // TOK_TILE
    alpha = (2 * depth) ** 0.25

    vecs = jnp.zeros((8, D_MODEL), F32).at[0].set(c[0]).at[1].set(c_ctx)
    mods = _ada(vecs, w_ada, b_ada)
    cos, sin = _rope_tables(n_lat, n_ctx)
    xt = jnp.concatenate([ctx[0], x[0]], axis=0)

    for l in range(depth):
        need_ctx = l < depth - 1
        lam_init = 0.8 - 0.6 * math.exp(-0.3 * l)
        mod = mods[l]
        w_ext, w_vt = _prep_w_in(w_in[l])
        q, k, vt, mkt, mq, mk, mv, mo, g, cu = _inproj(xt, mod, w_ext, w_vt, cos, sin, mlstm_gate_bias[l], n_ctx)

        dl = diff_lambda[l].astype(F32)
        lam = (jnp.exp(jnp.sum(dl[0] * dl[1])) - jnp.exp(jnp.sum(dl[2] * dl[3])) + lam_init).reshape(1)
        post = 1.0 - lam_init
        attn_l = _attention(lam, q, k, vt, diff_norm_w[l], n_ctx, n_lat, n_ctx + n_lat, post)
        four_l = _fourier_lat(cu[n_ctx:])
        if need_ctx:
            attn_c = _attention(lam, q, k, vt, diff_norm_w[l], 0, n_ctx, n_ctx, post)
            four_c = _fourier_small(cu[:n_ctx])
        else:
            attn_c, four_c = attn_l, four_l
        hf, hb = _mlstm(mq, mk, mkt, mv, g, n_ctx)
        xs = _outproj(attn_l, attn_c, four_l, four_c, hf, hb, mo, xt, mod, mlstm_norm_w[l], w_out[l].astype(BF16),
                      ln1_w[l], ln1_b[l], n_ctx, need_ctx, alpha)
        xt = _moe(xs, mod, l, w_router[l], b_router[l], w_gate_up, b_gate_up, w_down, b_down, ln2_w[l], ln2_b[l],
                  nct if need_ctx else 0, alpha)
    return xt[None]
```

```python
import functools
import math

import numpy as np
import jax
import jax.numpy as jnp
from jax import lax
from jax.experimental import pallas as pl
from jax.experimental.pallas import tpu as pltpu

F32 = jnp.float32
BF16 = jnp.bfloat16
HI = lax.Precision.HIGHEST

D_MODEL = 1024
GRID_W = 64
HEAD_DIM = 64
A_HEADS = 4
A_QK_DIM = HEAD_DIM
A_V_DIM = 2 * HEAD_DIM
A_WIDTH = A_HEADS * A_V_DIM
A_SCALE = A_QK_DIM ** -0.5
B_HEADS = 4
B_HEAD_DIM = HEAD_DIM
B_WIDTH = B_HEADS * B_HEAD_DIM
C_GROUPS = 4
C_GROUP_DIM = HEAD_DIM
C_WIDTH = C_GROUPS * C_GROUP_DIM
IN_SIZES = (A_WIDTH, A_WIDTH, A_WIDTH, B_WIDTH, B_WIDTH, B_WIDTH, B_WIDTH, 4 * B_HEADS, C_WIDTH)
ROPE_BASE = 10000.0
MLSTM_CHUNK = 64
N_EXPERTS = 32
TOP_K = 4
D_EXPERT = D_MODEL
SWIGLU_LIMIT = 7.0
SWIGLU_ALPHA = 1.702
EPS = 1e-5

LANES = 128
VMEM_LIMIT = 56 * 1024 * 1024
TOK_TILE = 256
ATT_TQ = 512
ATT_COLS = 256
ATT_UNROLL = 8
MOE_BM = 256

_OQ, _OQR, _OK, _OKR = 0, 512, 1024, 1536
_OMQ, _OMK, _OMV, _OMO, _OG, _OCU = 2048, 2304, 2560, 2816, 3072, 3200
Q_SCALE = A_SCALE * math.log2(math.e)
VT_ROWS = A_V_DIM + 16


def _cparams(sem, vmem=VMEM_LIMIT):
    return pltpu.CompilerParams(dimension_semantics=sem, vmem_limit_bytes=vmem)


def _ada_kernel(v_ref, w_ref, b_ref, o_ref):
    v = v_ref[...]
    s = v * jax.nn.sigmoid(v)
    o_ref[...] = jnp.dot(s, w_ref[...], precision=HI, preferred_element_type=F32) + b_ref[...]


def _ada(vecs, w_ada, b_ada):
    depth, d, d6 = w_ada.shape
    bn = 1024
    return pl.pallas_call(
        _ada_kernel,
        out_shape=jax.ShapeDtypeStruct((depth, 8, d6), F32),
        grid=(depth, d6 // bn),
        in_specs=[pl.BlockSpec((8, d), lambda l, j: (0, 0)),
                  pl.BlockSpec((None, d, bn), lambda l, j: (l, 0, j)),
                  pl.BlockSpec((None, 1, bn), lambda l, j: (l, 0, j))],
        out_specs=pl.BlockSpec((None, 8, bn), lambda l, j: (l, 0, j)),
        compiler_params=_cparams(("arbitrary", "arbitrary")),
        name="ada_modulation",
    )(vecs, w_ada, b_ada.reshape(depth, 1, d6))


def _mod_rows(mod_ref, k, is_ctx):
    lat = mod_ref[0:1, k * D_MODEL:(k + 1) * D_MODEL]
    ctx = mod_ref[1:2, k * D_MODEL:(k + 1) * D_MODEL]
    return jnp.where(is_ctx, ctx, lat)


def _inproj_kernel(x_ref, mod_ref, w_ref, wvt_ref, cos_ref, sin_ref, gb_ref,
                   q_ref, k_ref, vt_ref, mkt_ref, mq_ref, mk_ref, mv_ref, mo_ref, g_ref, cu_ref, *, n_ctx_tiles):
    is_ctx = pl.program_id(0) < n_ctx_tiles
    sh = _mod_rows(mod_ref, 0, is_ctx)
    sc = _mod_rows(mod_ref, 1, is_ctx)
    hm = (x_ref[...] * (1.0 + sc) + sh).astype(BF16)

    def proj(off, width):
        return jnp.dot(hm, w_ref[:, off:off + width], preferred_element_type=F32)

    cos = cos_ref[...]
    sin = sin_ref[...]
    for j in range(A_HEADS):
        o = j * LANES
        q = proj(_OQ + o, LANES) * cos + proj(_OQR + o, LANES) * sin
        q_ref[:, o:o + LANES] = (q * Q_SCALE).astype(BF16)
        k = proj(_OK + o, LANES) * cos + proj(_OKR + o, LANES) * sin
        k_ref[:, o:o + LANES] = k.astype(BF16)
    tr = lax.dot_general(wvt_ref[...], hm, (((1,), (1,)), ((), ())), preferred_element_type=F32)
    mkt_ref[...] = tr[A_WIDTH:, :] * (B_HEAD_DIM ** -0.5)
    vt = tr[:A_WIDTH, :].astype(BF16)
    for j in range(A_HEADS):
        vt_ref[j, 0:A_V_DIM, :] = vt[j * A_V_DIM:(j + 1) * A_V_DIM, :]
        vt_ref[j, A_V_DIM:, :] = jnp.ones((VT_ROWS - A_V_DIM, vt.shape[1]), BF16)
    mq_ref[...] = proj(_OMQ, B_WIDTH)
    mk_ref[...] = proj(_OMK, B_WIDTH) * (B_HEAD_DIM ** -0.5)
    mv_ref[...] = proj(_OMV, B_WIDTH)
    mo_ref[...] = proj(_OMO, B_WIDTH)
    g_ref[...] = proj(_OG, LANES) + gb_ref[...]
    cu_ref[...] = proj(_OCU, C_WIDTH)


def _rot_cols(w):
    d, width = w.shape
    wr = w.reshape(d, width // 64, 2, 2, 16)
    rot = jnp.stack([-wr[:, :, :, 1, :], wr[:, :, :, 0, :]], axis=3)
    return rot.reshape(d, width)


def _prep_w_in(w_in):
    offs = np.cumsum((0,) + IN_SIZES)
    aq, ak, av, bq, bk, bv, bo, bg, cu = [w_in[:, offs[i]:offs[i + 1]] for i in range(9)]
    bg = jnp.pad(bg, ((0, 0), (0, LANES - bg.shape[1])))
    w = jnp.concatenate([aq, _rot_cols(aq), ak, _rot_cols(ak), bq, bk, bv, bo, bg, cu], axis=1)
    return w.astype(BF16), jnp.concatenate([av, bk], axis=1).T.astype(BF16)


def _rope_tables(n_lat, n_ctx):
    rows = n_lat // GRID_W
    axis_dim = A_QK_DIM // 2
    inv_freq = ROPE_BASE ** (-jnp.arange(0, axis_dim, 2, dtype=F32) / axis_dim)
    ang_r = jnp.arange(rows, dtype=F32)[:, None] * inv_freq
    ang_c = jnp.arange(GRID_W, dtype=F32)[:, None] * inv_freq

    def table(fn, ctx_val):
        fr = jnp.broadcast_to(fn(ang_r)[:, None, :], (rows, GRID_W, axis_dim // 2))
        fc = jnp.broadcast_to(fn(ang_c)[None, :, :], (rows, GRID_W, axis_dim // 2))
        lat = jnp.concatenate([fr, fr, fc, fc] * 2, axis=-1).reshape(n_lat, 2 * A_QK_DIM)
        return jnp.concatenate([jnp.full((n_ctx, 2 * A_QK_DIM), ctx_val, F32), lat], axis=0)

    return table(jnp.cos, 1.0), table(jnp.sin, 0.0)


def _inproj(xt, mod, w_ext, w_vt, cos, sin, gate_bias, n_ctx):
    t = xt.shape[0]
    tm = TOK_TILE
    gb = jnp.pad(gate_bias.reshape(1, -1), ((0, 0), (0, LANES - 4 * B_HEADS)))
    row = lambda w: pl.BlockSpec((tm, w), lambda i: (i, 0))
    full = lambda a: pl.BlockSpec(a.shape, lambda i: (0,) * a.ndim)
    sds = jax.ShapeDtypeStruct
    rows_out = [(A_WIDTH, BF16)] * 2 + [(B_WIDTH, F32)] * 4 + [(LANES, F32), (C_WIDTH, F32)]
    out_shape = [sds((t, w), dt) for w, dt in rows_out]
    out_specs = [row(w) for w, _ in rows_out]
    out_shape.insert(2, sds((t // tm, A_HEADS, VT_ROWS, tm), BF16))
    out_specs.insert(2, pl.BlockSpec((None, A_HEADS, VT_ROWS, tm), lambda i: (i, 0, 0, 0)))
    out_shape.insert(3, sds((t // tm, B_WIDTH, tm), F32))
    out_specs.insert(3, pl.BlockSpec((None, B_WIDTH, tm), lambda i: (i, 0, 0)))
    return pl.pallas_call(
        functools.partial(_inproj_kernel, n_ctx_tiles=n_ctx // tm),
        out_shape=out_shape,
        grid=(t // tm,),
        in_specs=[row(D_MODEL), full(mod), full(w_ext), full(w_vt), row(LANES), row(LANES), full(gb)],
        out_specs=out_specs,
        compiler_params=_cparams(("arbitrary",)),
        name="in_projection",
    )(xt, mod, w_ext, w_vt, cos, sin, gb)


def _attn_kernel(lam_ref, q_ref, k_ref, vt_ref, nw_ref, o_ref, acc_ref, s_ref, *, n_kv, post_scale):
    q = q_ref[...]
    tq = q.shape[0]
    tk = vt_ref.shape[-1]
    lane = lax.broadcasted_iota(jnp.int32, q.shape, 1)
    zero = jnp.zeros_like(q)
    q2 = jnp.concatenate([jnp.where(lane < A_QK_DIM, q, zero), jnp.where(lane >= A_QK_DIM, q, zero)], axis=0)
    acc_ref[...] = jnp.zeros_like(acc_ref)

    def scores(j, slot):
        kb = k_ref[pl.ds(pl.multiple_of(j * tk, tk), tk), :]
        st = lax.dot_general(kb, q2, (((1,), (1,)), ((), ())), preferred_element_type=F32)
        s_ref[slot] = st
        return jnp.max(st, axis=0, keepdims=True)

    def consume(j, slot, cmax, m):
        m_new = jnp.maximum(m, cmax)
        alpha = jnp.exp2(m - m_new)
        for h in range(2 * tq // ATT_COLS):
            cs = slice(h * ATT_COLS, (h + 1) * ATT_COLS)
            p = jnp.exp2(s_ref[slot, :, cs] - m_new[:, cs]).astype(BF16)
            r = jnp.dot(vt_ref[j], p, preferred_element_type=F32)
            acc_ref[:, cs] = alpha[:, cs] * acc_ref[:, cs] + r
        return m_new

    cm0 = scores(0, 0)

    def group(t, carry):
        cm, m = carry
        a = ATT_UNROLL * t
        for i in range(ATT_UNROLL):
            cm_next = scores(a + i + 1, (i + 1) % 2)
            m = consume(a + i, i % 2, cm, m)
            cm = cm_next
        return cm, m

    assert (n_kv - 1) % ATT_UNROLL == 0
    cm_last, m = lax.fori_loop(0, (n_kv - 1) // ATT_UNROLL, group, (cm0, jnp.full((1, 2 * tq), -jnp.inf, F32)))
    consume(n_kv - 1, 0, cm_last, m)
    l = acc_ref[A_V_DIM:A_V_DIM + 1, :]
    acc = acc_ref[0:A_V_DIM, :]
    ot = acc[:, :tq] * (1.0 / l[:, :tq]) - acc[:, tq:] * (lam_ref[0] / l[:, tq:])
    yt = ot * lax.rsqrt(jnp.mean(ot * ot, axis=0, keepdims=True) + EPS)
    o_ref[...] = yt.T * nw_ref[...] * post_scale


def _attention(lam, q, k, vt, norm_w, q_row0, n_q, n_keys, post_scale):
    tq = min(ATT_TQ, n_q)
    tk = vt.shape[-1]
    n_kv = n_keys // tk
    if q_row0 % tq:
        q, q_row0 = q[q_row0:q_row0 + n_q], 0
    q_blk0 = q_row0 // tq
    return pl.pallas_call(
        functools.partial(_attn_kernel, n_kv=n_kv, post_scale=post_scale),
        out_shape=jax.ShapeDtypeStruct((n_q, A_WIDTH), F32),
        grid=(A_HEADS, n_q // tq),
        in_specs=[pl.BlockSpec(memory_space=pltpu.SMEM),
                  pl.BlockSpec((tq, LANES), lambda h, i: (i + q_blk0, h)),
                  pl.BlockSpec((n_keys, LANES), lambda h, i: (0, h)),
                  pl.BlockSpec((n_kv, None, VT_ROWS, tk), lambda h, i: (0, h, 0, 0)),
                  pl.BlockSpec((1, LANES), lambda h, i: (0, h))],
        out_specs=pl.BlockSpec((tq, LANES), lambda h, i: (i, h)),
        scratch_shapes=[pltpu.VMEM((VT_ROWS, 2 * tq), F32), pltpu.VMEM((2, tk, 2 * tq), F32)],
        compiler_params=_cparams(("arbitrary", "arbitrary")),
        name="diff_attention",
    )(lam, q, k, vt, norm_w.reshape(1, A_WIDTH))


def _mlstm_kernel(qf, kf, ktf, vf, gf, qb, kb, ktb, vb, gb, of_ref, ob_ref, s_ref, m_ref, *, chunks):
    L = MLSTM_CHUNK
    tm = chunks * L
    neg = -jnp.inf

    @pl.when(pl.program_id(0) == 0)
    def _():
        s_ref[...] = jnp.zeros_like(s_ref)
        m_ref[...] = jnp.zeros_like(m_ref)

    tr = lax.broadcasted_iota(jnp.int32, (tm, tm), 0)
    tc = lax.broadcasted_iota(jnp.int32, (tm, tm), 1)
    same_chunk = tr // L == tc // L
    masks = (same_chunk & (tr >= tc), same_chunk & (tr <= tc))
    row_chunk = lax.broadcasted_iota(jnp.int32, (tm, 1), 0) // L
    col_chunk = lax.broadcasted_iota(jnp.int32, (1, tm), 1) // L
    lane_half = lax.broadcasted_iota(jnp.int32, (tm, LANES), 1) // B_HEAD_DIM
    row_half = lax.broadcasted_iota(jnp.int32, (LANES, 1), 0) // B_HEAD_DIM
    rr = lax.broadcasted_iota(jnp.int32, (LANES, 2 * LANES), 0) // B_HEAD_DIM
    cc = lax.broadcasted_iota(jnp.int32, (LANES, 2 * LANES), 1)
    blockmask = jnp.where(cc < LANES, cc // B_HEAD_DIM, cc - LANES) == rr
    aug = jnp.where(lax.broadcasted_iota(jnp.int32, (tm, LANES), 1) < 2, 1.0, 0.0).astype(F32)
    io = ((qf, kf, ktf, vf, gf, of_ref), (qb, kb, ktb, vb, gb, ob_ref))

    def by_chunk(vals, chunk_ids):
        out = vals[chunks - 1]
        for c in range(chunks - 2, -1, -1):
            out = jnp.where(chunk_ids == c, vals[c], out)
        return out

    dp = [(d, p) for d in range(2) for p in range(2)]
    dph = [(d, p, hh) for d, p in dp for hh in range(2)]

    g_tile, b_tile, g_t, b_t = {}, {}, {}, {}
    for d in range(2):
        g_tile[d] = io[d][4][...]
        b_tile[d] = jnp.dot(masks[d].astype(F32), jax.nn.log_sigmoid(g_tile[d]), precision=HI,
                            preferred_element_type=F32)
        g_t[d] = g_tile[d].T
        b_t[d] = b_tile[d].T
    qp, kp, v_aug, ktp = {}, {}, {}, {}
    for d, p in dp:
        sl = slice(p * LANES, (p + 1) * LANES)
        qp[d, p] = io[d][0][:, sl]
        kp[d, p] = io[d][1][:, sl].astype(BF16)
        ktp[d, p] = io[d][2][sl, :]
        v_aug[d, p] = jnp.concatenate([io[d][3][:, sl], aug], axis=1).astype(BF16)

    b_col, b_row, i_row, gs, a_row, m_loc, w_t = {}, {}, {}, {}, {}, {}, {}
    for d, p, hh in dph:
        h = 2 * p + hh
        il, fl = 4 * d + h, 8 + 4 * d + h
        last = L - 1 if d == 0 else 0
        b_col[d, p, hh] = b_tile[d][:, fl:fl + 1]
        b_row[d, p, hh] = b_t[d][fl:fl + 1, :]
        i_row[d, p, hh] = g_t[d][il:il + 1, :]
        gs[d, p, hh] = [b_row[d, p, hh][:, c * L + last:c * L + last + 1] for c in range(chunks)]
        a_row[d, p, hh] = by_chunk(gs[d, p, hh], col_chunk) - b_row[d, p, hh] + i_row[d, p, hh]
    for k3 in dph:
        m_loc[k3] = [jnp.max(jnp.where(col_chunk == c, a_row[k3], neg), axis=-1, keepdims=True)
                     for c in range(chunks)]
    for k3 in dph:
        w_t[k3] = jnp.exp(a_row[k3] - by_chunk(m_loc[k3], col_chunk))

    upd, sqk = {}, {}
    for d, p in dp:
        kw_t = ktp[d, p] * jnp.where(row_half == 0, w_t[d, p, 0], w_t[d, p, 1])
        for c in range(chunks):
            kw_c = jnp.where(col_chunk == c, kw_t, 0.0).astype(BF16)
            upd[d, p, c] = jnp.where(blockmask, jnp.dot(kw_c, v_aug[d, p], preferred_element_type=F32), 0.0)
    for d, p, hh in dph:
        qm = jnp.where(lane_half == hh, qp[d, p], 0.0).astype(BF16)
        sqk[d, p, hh] = lax.dot_general(qm, kp[d, p], (((1,), (1,)), ((), ())), preferred_element_type=F32)

    m_old = m_ref[...]
    orders = (range(chunks), range(chunks - 1, -1, -1))
    m_at, dec, inj, m_end = {}, {}, {}, {}
    for d, p, hh in dph:
        row = 4 * d + 2 * p + hh
        m_prev = m_old[row:row + 1, 0:1]
        for c in orders[d]:
            m_at[d, p, hh, c] = m_prev
            g_c = gs[d, p, hh][c]
            m_new = jnp.maximum(g_c + m_prev, m_loc[d, p, hh][c])
            dec[d, p, hh, c] = jnp.exp(g_c + m_prev - m_new)
            inj[d, p, hh, c] = jnp.exp(m_loc[d, p, hh][c] - m_new)
            m_prev = m_new
        m_end[row] = m_prev
    state_at, state_end = {}, {}
    for d, p in dp:
        state = s_ref[d, p]
        for c in orders[d]:
            state_at[d, p, c] = state
            state = (jnp.where(rr == 0, dec[d, p, 0, c], dec[d, p, 1, c]) * state
                     + jnp.where(rr == 0, inj[d, p, 0, c], inj[d, p, 1, c]) * upd[d, p, c])
        state_end[d, p] = state
    for d, p in dp:
        s_ref[d, p] = state_end[d, p]
    m_rows = lax.broadcasted_iota(jnp.int32, m_old.shape, 0)
    m_out = m_old
    for row, val in m_end.items():
        m_out = jnp.where(m_rows == row, val, m_out)
    m_ref[...] = m_out

    dmat = {k3: jnp.where(masks[k3[0]], b_col[k3] + (i_row[k3] - b_row[k3]), neg) for k3 in dph}
    m_row = {k3: jnp.max(dmat[k3], axis=-1, keepdims=True) for k3 in dph}
    s_loc = {k3: (sqk[k3] * jnp.exp(dmat[k3] - m_row[k3])).astype(BF16) for k3 in dph}

    r = {}
    for d, p, hh in dph:
        r[d, p, hh] = jnp.dot(s_loc[d, p, hh], v_aug[d, p], preferred_element_type=F32)
    q2 = {}
    for d, p in dp:
        q2[d, p] = jnp.concatenate(
            [jnp.dot(qp[d, p][c * L:(c + 1) * L].astype(BF16), state_at[d, p, c].astype(BF16),
                     preferred_element_type=F32) for c in range(chunks)], axis=0)

    inter = {k3: b_col[k3] + by_chunk([m_at[k3 + (c,)] for c in range(chunks)], row_chunk) for k3 in dph}
    m_col = {k3: jnp.maximum(inter[k3], m_row[k3]) for k3 in dph}
    e_loc = {k3: jnp.exp(m_row[k3] - m_col[k3]) for k3 in dph}
    e_int = {k3: jnp.exp(inter[k3] - m_col[k3]) for k3 in dph}
    floor = {k3: jnp.exp(-m_col[k3]) for k3 in dph}
    den = {(d, p, hh): e_loc[d, p, hh] * r[d, p, hh][:, LANES:LANES + 1]
           + e_int[d, p, hh] * q2[d, p][:, LANES + hh:LANES + hh + 1] for d, p, hh in dph}
    inv = {k3: 1.0 / jnp.maximum(jnp.abs(den[k3]), floor[k3]) for k3 in dph}
    a_loc = {k3: e_loc[k3] * inv[k3] for k3 in dph}
    a_int = {k3: e_int[k3] * inv[k3] for k3 in dph}
    head_out = {(d, p, hh): a_loc[d, p, hh] * r[d, p, hh][:, :LANES] + a_int[d, p, hh] * q2[d, p][:, :LANES]
                for d, p, hh in dph}
    for d in range(2):
        io[d][5][...] = jnp.concatenate(
            [jnp.where(lane_half == 0, head_out[d, p, 0], head_out[d, p, 1]) for p in range(2)], axis=1)


def _mlstm(mq, mk, mkt, mv, g, n_ctx):
    t = mq.shape[0]
    tm = TOK_TILE
    n_tiles = t // tm
    n_ctx_tiles = n_ctx // tm
    mirror = lambda i: jnp.where(i < n_ctx_tiles, n_ctx_tiles - 1 - i, n_tiles - 1 - (i - n_ctx_tiles))
    fwd = lambda i: (i, 0)
    bwd = lambda i: (mirror(i), 0)
    spec = lambda w, im: pl.BlockSpec((tm, w), im)
    tspec = lambda f: pl.BlockSpec((None, B_WIDTH, tm), lambda i: (f(i), 0, 0))
    ins = ([spec(B_WIDTH, fwd)] * 2 + [tspec(lambda i: i), spec(B_WIDTH, fwd), spec(LANES, fwd)]
           + [spec(B_WIDTH, bwd)] * 2 + [tspec(mirror), spec(B_WIDTH, bwd), spec(LANES, bwd)])
    return pl.pallas_call(
        functools.partial(_mlstm_kernel, chunks=tm // MLSTM_CHUNK),
        out_shape=[jax.ShapeDtypeStruct((t, B_WIDTH), F32)] * 2,
        grid=(n_tiles,),
        in_specs=ins,
        out_specs=[spec(B_WIDTH, fwd), spec(B_WIDTH, bwd)],
        scratch_shapes=[pltpu.VMEM((2, 2, LANES, 2 * LANES), F32), pltpu.VMEM((8, LANES), F32)],
        compiler_params=_cparams(("arbitrary",)),
        name="mlstm_scan",
    )(mq, mk, mkt, mv, g, mq, mk, mkt, mv, g)


def _dft_cs(n):
    j = np.arange(n, dtype=np.int64)
    ang = 2.0 * np.pi * ((j[:, None] * j[None, :]) % n).astype(np.float64) / n
    s = 1.0 / math.sqrt(n)
    return np.cos(ang) * s, np.sin(ang) * s


def _chan_mats():
    c, s = _dft_cs(C_GROUP_DIM)
    eye = np.eye(C_GROUPS)
    return np.concatenate([np.kron(eye, c), np.kron(eye, s)], axis=0).astype(np.float32)


def _fft1_kernel(fa_ref, u_ref, z_ref):
    z_ref[...] = jnp.dot(fa_ref[...], u_ref[...], precision=HI, preferred_element_type=F32)


def _fft2_kernel(z_ref, twc_ref, tws_ref, fb_ref, ch_ref, o_ref, *, kb):
    n2 = fb_ref.shape[0] // 2
    for j in range(kb):
        zr = z_ref[0, j]
        zi = z_ref[1, j]
        tc = twc_ref[j]
        ts = tws_ref[j]
        st = jnp.concatenate([zr * tc + zi * ts, zi * tc - zr * ts], axis=0)
        a = jnp.dot(fb_ref[...], st, precision=HI, preferred_element_type=F32)
        ari = jnp.concatenate([a[:n2], a[n2:]], axis=1)
        o_ref[:, j * C_WIDTH:(j + 1) * C_WIDTH] = jnp.dot(ari, ch_ref[...], precision=HI,
                                                         preferred_element_type=F32)


def _fourier_lat(u):
    n = u.shape[0]
    n1 = int(round(math.sqrt(n)))
    assert n1 * n1 == n
    n2 = n1
    c, s = _dft_cs(n1)
    fa = jnp.asarray(np.concatenate([c, -s], axis=0).astype(np.float32))
    fb = jnp.asarray(np.block([[c, s], [-s, c]]).astype(np.float32))
    k1 = np.arange(n1, dtype=np.int64)
    ang = 2.0 * np.pi * ((k1[:, None] * k1[None, :]) % n).astype(np.float64) / n
    twc = jnp.asarray(np.cos(ang).astype(np.float32)).reshape(n1, n2, 1)
    tws = jnp.asarray(np.sin(ang).astype(np.float32)).reshape(n1, n2, 1)
    ch = jnp.asarray(_chan_mats())
    cols = n2 * C_WIDTH
    tb = min(2048, cols)
    z = pl.pallas_call(
        _fft1_kernel,
        out_shape=jax.ShapeDtypeStruct((2 * n1, cols), F32),
        grid=(cols // tb,),
        in_specs=[pl.BlockSpec((2 * n1, n1), lambda i: (0, 0)), pl.BlockSpec((n1, tb), lambda i: (0, i))],
        out_specs=pl.BlockSpec((2 * n1, tb), lambda i: (0, i)),
        compiler_params=_cparams(("arbitrary",)),
        name="fourier_stage1",
    )(fa, u.reshape(n1, cols))
    kb = 8
    out = pl.pallas_call(
        functools.partial(_fft2_kernel, kb=kb),
        out_shape=jax.ShapeDtypeStruct((n2, n1 * C_WIDTH), F32),
        grid=(n1 // kb,),
        in_specs=[pl.BlockSpec((2, kb, n2, C_WIDTH), lambda i: (0, i, 0, 0)),
                  pl.BlockSpec((kb, n2, 1), lambda i: (i, 0, 0)),
                  pl.BlockSpec((kb, n2, 1), lambda i: (i, 0, 0)),
                  pl.BlockSpec((2 * n2, 2 * n2), lambda i: (0, 0)),
                  pl.BlockSpec((2 * C_WIDTH, C_WIDTH), lambda i: (0, 0))],
        out_specs=pl.BlockSpec((n2, kb * C_WIDTH), lambda i: (0, i)),
        compiler_params=_cparams(("arbitrary",)),
        name="fourier_stage2",
    )(z.reshape(2, n1, n2, C_WIDTH), twc, tws, fb, ch)
    return out.reshape(n, C_WIDTH)


def _fft_small_kernel(fa_ref, u_ref, ch_ref, o_ref):
    n = u_ref.shape[0]
    a = jnp.dot(fa_ref[...], u_ref[...], precision=HI, preferred_element_type=F32)
    ari = jnp.concatenate([a[:n], a[n:]], axis=1)
    o_ref[...] = jnp.dot(ari, ch_ref[...], precision=HI, preferred_element_type=F32)


def _fourier_small(u):
    n = u.shape[0]
    c, s = _dft_cs(n)
    fa = jnp.asarray(np.concatenate([c, -s], axis=0).astype(np.float32))
    return pl.pallas_call(
        _fft_small_kernel,
        out_shape=jax.ShapeDtypeStruct((n, C_WIDTH), F32),
        compiler_params=_cparams(None),
        name="fourier_small",
    )(fa, u, jnp.asarray(_chan_mats()))


def _layer_norm(z, w, b):
    mu = jnp.mean(z, axis=-1, keepdims=True)
    zc = z - mu
    var = jnp.mean(zc * zc, axis=-1, keepdims=True)
    return zc * lax.rsqrt(var + EPS) * w + b


def _outproj_kernel(al_ref, ac_ref, fl_ref, fc_ref, hf_ref, hb_ref, bo_ref, x_ref, mod_ref, mnw_ref, seg_ref,
                    w_ref, lw_ref, lb_ref, o_ref, *, n_ctx_tiles, alpha):
    is_ctx = pl.program_id(0) < n_ctx_tiles
    attn = jnp.where(is_ctx, ac_ref[...], al_ref[...])
    four = jnp.where(is_ctx, fc_ref[...], fl_ref[...])
    h = hf_ref[...] + hb_ref[...]
    ms = jnp.dot(h * h, seg_ref[...], precision=HI, preferred_element_type=F32)
    y = h * lax.rsqrt(ms + EPS) * mnw_ref[...] * jax.nn.sigmoid(bo_ref[...])
    mix = (jnp.dot(attn.astype(BF16), w_ref[0:A_WIDTH, :], preferred_element_type=F32)
           + jnp.dot(y.astype(BF16), w_ref[A_WIDTH:A_WIDTH + B_WIDTH, :], preferred_element_type=F32)
           + jnp.dot(four.astype(BF16), w_ref[A_WIDTH + B_WIDTH:, :], preferred_element_type=F32))
    g1 = _mod_rows(mod_ref, 2, is_ctx)
    o_ref[...] = _layer_norm(alpha * x_ref[...] + g1 * mix, lw_ref[...], lb_ref[...])


def _outproj(attn_l, attn_c, four_l, four_c, hf, hb, bo, xt, mod, mnw, w_out_bf, ln_w, ln_b, n_ctx, with_ctx, alpha):
    tm = TOK_TILE
    nct = n_ctx // tm
    n_lat_tiles = attn_l.shape[0] // tm
    if with_ctx:
        n_tiles, uoff, n_ctx_tiles = n_lat_tiles + nct, 0, nct
        lat = lambda i: (jnp.maximum(i - nct, 0), 0)
    else:
        n_tiles, uoff, n_ctx_tiles = n_lat_tiles, nct, 0
        lat = lambda i: (i, 0)
    uni = lambda i: (i + uoff, 0)
    ctxm = lambda i: (jnp.minimum(i, nct - 1), 0)
    seg = jnp.asarray(np.kron(np.eye(B_HEADS), np.full((B_HEAD_DIM, B_HEAD_DIM), 1.0 / B_HEAD_DIM)).astype(np.float32))
    full = lambda a: pl.BlockSpec(a.shape, lambda i: (0,) * a.ndim)
    mnw2, lw2, lb2 = mnw.reshape(1, -1), ln_w.reshape(1, -1), ln_b.reshape(1, -1)
    return pl.pallas_call(
        functools.partial(_outproj_kernel, n_ctx_tiles=n_ctx_tiles, alpha=alpha),
        out_shape=jax.ShapeDtypeStruct((n_tiles * tm, D_MODEL), F32),
        grid=(n_tiles,),
        in_specs=[pl.BlockSpec((tm, A_WIDTH), lat), pl.BlockSpec((tm, A_WIDTH), ctxm),
                  pl.BlockSpec((tm, C_WIDTH), lat), pl.BlockSpec((tm, C_WIDTH), ctxm),
                  pl.BlockSpec((tm, B_WIDTH), uni), pl.BlockSpec((tm, B_WIDTH), uni),
                  pl.BlockSpec((tm, B_WIDTH), uni), pl.BlockSpec((tm, D_MODEL), uni),
                  full(mod), full(mnw2), full(seg), full(w_out_bf), full(lw2), full(lb2)],
        out_specs=pl.BlockSpec((tm, D_MODEL), lambda i: (i, 0)),
        compiler_params=_cparams(("arbitrary",)),
        name="out_projection",
    )(attn_l, attn_c, four_l, four_c, hf, hb, bo, xt, mod, mnw2, seg, w_out_bf, lw2, lb2)


def _router_kernel(x_ref, mod_ref, wrt_ref, br_ref, idx_ref, gate_ref, rank_ref, cnt_ref, run_ref, *, n_ctx_tiles):
    i = pl.program_id(0)

    @pl.when(i == 0)
    def _():
        run_ref[...] = jnp.zeros_like(run_ref)

    is_ctx = i < n_ctx_tiles
    f_in = x_ref[...] * (1.0 + _mod_rows(mod_ref, 4, is_ctx)) + _mod_rows(mod_ref, 3, is_ctx)
    logits = lax.dot_general(wrt_ref[...], f_in, (((1,), (1,)), ((), ())), precision=HI,
                             preferred_element_type=F32) + br_ref[...]
    tm = logits.shape[1]
    eidx = lax.broadcasted_iota(jnp.int32, logits.shape, 0).astype(F32)
    work = logits
    vals, sels = [], []
    for _ in range(TOP_K):
        mx = jnp.max(work, axis=0, keepdims=True)
        sel = jnp.min(jnp.where(work == mx, eidx, float(N_EXPERTS)), axis=0, keepdims=True)
        vals.append(mx)
        sels.append(sel)
        work = jnp.where(eidx == sel, -jnp.inf, work)
    es = [jnp.exp(v - vals[0]) for v in vals]
    tot = es[0] + es[1] + es[2] + es[3]
    onehots = [(eidx == s).astype(F32) for s in sels]
    oh_all = onehots[0] + onehots[1] + onehots[2] + onehots[3]
    r = lax.broadcasted_iota(jnp.int32, (tm, tm), 0)
    c = lax.broadcasted_iota(jnp.int32, (tm, tm), 1)
    before = jnp.dot(oh_all.astype(BF16), (r < c).astype(BF16), preferred_element_type=F32) + run_ref[:, 0:1]
    krow = lax.broadcasted_iota(jnp.int32, (8, tm), 0)
    idx_o = jnp.zeros((8, tm), jnp.int32)
    gate_o = jnp.zeros((8, tm), F32)
    rank_o = jnp.zeros((8, tm), jnp.int32)
    for k in range(TOP_K):
        rank = jnp.sum(onehots[k] * before, axis=0, keepdims=True).astype(jnp.int32)
        idx_o = jnp.where(krow == k, sels[k].astype(jnp.int32), idx_o)
        gate_o = jnp.where(krow == k, es[k] / tot, gate_o)
        rank_o = jnp.where(krow == k, rank, rank_o)
    idx_ref[...] = idx_o
    gate_ref[...] = gate_o
    rank_ref[...] = rank_o
    run_new = run_ref[...] + jnp.sum(oh_all, axis=1, keepdims=True)
    run_ref[...] = run_new
    cnt_ref[...] = run_new.astype(jnp.int32)


def _router(xs, mod, w_router, b_router, n_ctx_tiles):
    t = xs.shape[0]
    tm = TOK_TILE
    wrt = w_router.T
    br = b_router.reshape(-1, 1)
    full = lambda a: pl.BlockSpec(a.shape, lambda i: (0,) * a.ndim)
    col = pl.BlockSpec((8, tm), lambda i: (0, i))
    return pl.pallas_call(
        functools.partial(_router_kernel, n_ctx_tiles=n_ctx_tiles),
        out_shape=[jax.ShapeDtypeStruct((8, t), jnp.int32), jax.ShapeDtypeStruct((8, t), F32),
                   jax.ShapeDtypeStruct((8, t), jnp.int32), jax.ShapeDtypeStruct((N_EXPERTS, LANES), jnp.int32)],
        grid=(t // tm,),
        in_specs=[pl.BlockSpec((tm, D_MODEL), lambda i: (i, 0)), full(mod), full(wrt), full(br)],
        out_specs=[col, col, col, pl.BlockSpec((N_EXPERTS, LANES), lambda i: (0, 0))],
        scratch_shapes=[pltpu.VMEM((N_EXPERTS, LANES), F32)],
        compiler_params=_cparams(("arbitrary",)),
        name="moe_router",
    )(xs, mod, wrt, br)


ROW_CHUNKS = D_MODEL // LANES
DMA_UNROLL = 8


def _to_chunk_rows(ref, val):
    m = val.shape[0]
    for c in range(ROW_CHUNKS):
        ref[pl.ds(c, m, stride=ROW_CHUNKS), :] = val[:, c * LANES:(c + 1) * LANES]


def _from_chunk_rows(ref, m):
    return jnp.concatenate([ref[pl.ds(c, m, stride=ROW_CHUNKS), :] for c in range(ROW_CHUNKS)], axis=1)


def _dispatch_kernel(pstart_ref, pend_ref, dest_ref, x_ref, mod_ref, xb_ref, fbuf, zbuf, sem, zsem, *, n_ctx_tiles, bm):
    i = pl.program_id(0)
    tm = x_ref.shape[0]

    @pl.when(i == 0)
    def _():
        zbuf[...] = jnp.zeros_like(zbuf)

        def zcopy(e):
            row0 = pl.multiple_of((pend_ref[e] - bm) * ROW_CHUNKS, bm * ROW_CHUNKS)
            return pltpu.make_async_copy(zbuf, xb_ref.at[pl.ds(row0, bm * ROW_CHUNKS)], zsem)

        def zstart(e, c):
            @pl.when(pend_ref[e] > pstart_ref[e])
            def _():
                zcopy(e).start()
            return c

        def zwait(e, c):
            @pl.when(pend_ref[e] > pstart_ref[e])
            def _():
                zcopy(e).wait()
            return c

        lax.fori_loop(0, N_EXPERTS, zstart, 0)
        lax.fori_loop(0, N_EXPERTS, zwait, 0)

        def tcopy(b):
            row0 = pl.multiple_of(b * (bm * ROW_CHUNKS), bm * ROW_CHUNKS)
            return pltpu.make_async_copy(zbuf, xb_ref.at[pl.ds(row0, bm * ROW_CHUNKS)], zsem)

        def tstart(b, c):
            tcopy(b).start()
            return c

        def twait(b, c):
            tcopy(b).wait()
            return c

        tail0 = pend_ref[N_EXPERTS - 1] // bm
        n_blocks = xb_ref.shape[0] // (bm * ROW_CHUNKS)
        lax.fori_loop(tail0, n_blocks, tstart, 0)
        lax.fori_loop(tail0, n_blocks, twait, 0)

    is_ctx = i < n_ctx_tiles
    f = x_ref[...] * (1.0 + _mod_rows(mod_ref, 4, is_ctx)) + _mod_rows(mod_ref, 3, is_ctx)
    slot = i % 2
    _to_chunk_rows(fbuf.at[slot], f)

    def start(t, c):
        src = pl.multiple_of(t * ROW_CHUNKS, ROW_CHUNKS)
        for k in range(TOP_K):
            dst = pl.multiple_of(dest_ref[t * TOP_K + k] * ROW_CHUNKS, ROW_CHUNKS)
            pltpu.make_async_copy(fbuf.at[slot, pl.ds(src, ROW_CHUNKS)], xb_ref.at[pl.ds(dst, ROW_CHUNKS)],
                                  sem.at[slot]).start(priority=k % 2)
        return c

    lax.fori_loop(0, tm, start, 0, unroll=DMA_UNROLL)

    def wait_tile(s):
        for _ in range(TOP_K):
            pltpu.make_async_copy(fbuf.at[s], xb_ref.at[pl.ds(0, tm * ROW_CHUNKS)], sem.at[s]).wait()

    @pl.when(i > 0)
    def _():
        wait_tile(1 - slot)

    @pl.when(i == pl.num_programs(0) - 1)
    def _():
        wait_tile(slot)


def _dispatch(pad_start, pad_end, dest_flat, xs, mod, n_rows, n_ctx_tiles):
    t = xs.shape[0]
    tm = TOK_TILE
    gs = pltpu.PrefetchScalarGridSpec(
        num_scalar_prefetch=2,
        grid=(t // tm,),
        in_specs=[pl.BlockSpec((tm * TOP_K,), lambda i, *_: (i,), memory_space=pltpu.SMEM),
                  pl.BlockSpec((tm, D_MODEL), lambda i, *_: (i, 0)),
                  pl.BlockSpec(mod.shape, lambda i, *_: (0, 0))],
        out_specs=pl.BlockSpec(memory_space=pl.ANY),
        scratch_shapes=[pltpu.VMEM((2, tm * ROW_CHUNKS, LANES), F32), pltpu.VMEM((MOE_BM * ROW_CHUNKS, LANES), F32),
                        pltpu.SemaphoreType.DMA((2,)), pltpu.SemaphoreType.DMA],
    )
    return pl.pallas_call(
        functools.partial(_dispatch_kernel, n_ctx_tiles=n_ctx_tiles, bm=MOE_BM),
        out_shape=jax.ShapeDtypeStruct((n_rows * ROW_CHUNKS, LANES), F32),
        grid_spec=gs,
        compiler_params=_cparams(("arbitrary",)),
        name="moe_dispatch",
    )(pad_start, pad_end, dest_flat, xs, mod)


def _expert_kernel(bexp_ref, bvalid_ref, x_ref, wgu_ref, bgu_ref, wd_ref, bd_ref, y_ref, wgu_bf, wd_bf):
    i = pl.program_id(0)
    e = bexp_ref[i]
    prev = bexp_ref[jnp.maximum(i - 1, 0)]

    @pl.when(jnp.logical_or(i == 0, e != prev))
    def _():
        wgu_bf[...] = wgu_ref[...].astype(BF16)
        wd_bf[...] = wd_ref[...].astype(BF16)

    @pl.when(bvalid_ref[i] > 0)
    def _():
        bm = y_ref.shape[0] // ROW_CHUNKS
        x = _from_chunk_rows(x_ref, bm).astype(BF16)
        gu = jnp.dot(x, wgu_bf[...], preferred_element_type=F32) + bgu_ref[...]
        gate = jnp.minimum(gu[:, :D_EXPERT], SWIGLU_LIMIT)
        up = jnp.clip(gu[:, D_EXPERT:], -SWIGLU_LIMIT, SWIGLU_LIMIT)
        act = gate * jax.nn.sigmoid(SWIGLU_ALPHA * gate) * (up + 1.0)
        _to_chunk_rows(y_ref, jnp.dot(act.astype(BF16), wd_bf[...], preferred_element_type=F32) + bd_ref[...])

    @pl.when(bvalid_ref[i] == 0)
    def _():
        y_ref[...] = jnp.zeros_like(y_ref)


def _experts(block_exp, block_valid, xb, layer, w_gate_up, b_gate_up, w_down, b_down):
    n_rows = xb.shape[0] // ROW_CHUNKS
    bm = MOE_BM
    depth = w_gate_up.shape[0]
    bgu = b_gate_up.reshape(depth, N_EXPERTS, 1, 2 * D_EXPERT)
    bd = b_down.reshape(depth, N_EXPERTS, 1, D_MODEL)
    gs = pltpu.PrefetchScalarGridSpec(
        num_scalar_prefetch=2,
        grid=(n_rows // bm,),
        in_specs=[pl.BlockSpec((bm * ROW_CHUNKS, LANES), lambda i, be, bv: (i, 0)),
                  pl.BlockSpec((None, None, D_MODEL, 2 * D_EXPERT), lambda i, be, bv: (layer, be[i], 0, 0)),
                  pl.BlockSpec((None, None, 1, 2 * D_EXPERT), lambda i, be, bv: (layer, be[i], 0, 0)),
                  pl.BlockSpec((None, None, D_EXPERT, D_MODEL), lambda i, be, bv: (layer, be[i], 0, 0)),
                  pl.BlockSpec((None, None, 1, D_MODEL), lambda i, be, bv: (layer, be[i], 0, 0))],
        out_specs=pl.BlockSpec((bm * ROW_CHUNKS, LANES), lambda i, be, bv: (i, 0)),
        scratch_shapes=[pltpu.VMEM((D_MODEL, 2 * D_EXPERT), BF16), pltpu.VMEM((D_EXPERT, D_MODEL), BF16)],
    )
    return pl.pallas_call(
        _expert_kernel,
        out_shape=jax.ShapeDtypeStruct((n_rows * ROW_CHUNKS, LANES), F32),
        grid_spec=gs,
        compiler_params=_cparams(("arbitrary",)),
        name="moe_experts",
    )(block_exp, block_valid, xb, w_gate_up, bgu, w_down, bd)


def _combine_kernel(dest_ref, dnext_ref, gate_ref, x_ref, mod_ref, lw_ref, lb_ref, yb_ref, o_ref, ybuf, sem, *,
                    n_ctx_tiles, alpha):
    i = pl.program_id(0)
    tm = x_ref.shape[0]
    slot = i % 2

    def gather(d_ref, s):
        def start(t, c):
            dst = pl.multiple_of(t * ROW_CHUNKS, ROW_CHUNKS)
            for k in range(TOP_K):
                src = pl.multiple_of(d_ref[t * TOP_K + k] * ROW_CHUNKS, ROW_CHUNKS)
                pltpu.make_async_copy(yb_ref.at[pl.ds(src, ROW_CHUNKS)], ybuf.at[s, k, pl.ds(dst, ROW_CHUNKS)],
                                      sem.at[s]).start(priority=k % 2)
            return c

        lax.fori_loop(0, tm, start, 0, unroll=DMA_UNROLL)

    @pl.when(i == 0)
    def _():
        gather(dest_ref, 0)

    @pl.when(i + 1 < pl.num_programs(0))
    def _():
        gather(dnext_ref, 1 - slot)

    for k in range(TOP_K):
        pltpu.make_async_copy(yb_ref.at[pl.ds(0, tm * ROW_CHUNKS)], ybuf.at[slot, k], sem.at[slot]).wait()
    gates = gate_ref[...]
    y = gates[:, 0:1] * _from_chunk_rows(ybuf.at[slot, 0], tm)
    for k in range(1, TOP_K):
        y = y + gates[:, k:k + 1] * _from_chunk_rows(ybuf.at[slot, k], tm)
    is_ctx = i < n_ctx_tiles
    g2 = _mod_rows(mod_ref, 5, is_ctx)
    o_ref[...] = _layer_norm(alpha * x_ref[...] + g2 * y, lw_ref[...], lb_ref[...])


def _combine(dest_flat, gates, xs, mod, ln_w, ln_b, yb, n_ctx_tiles, alpha):
    t = xs.shape[0]
    tm = TOK_TILE
    n_tiles = t // tm
    lw2, lb2 = ln_w.reshape(1, -1), ln_b.reshape(1, -1)
    full = lambda a: pl.BlockSpec(a.shape, lambda i: (0,) * a.ndim)
    return pl.pallas_call(
        functools.partial(_combine_kernel, n_ctx_tiles=n_ctx_tiles, alpha=alpha),
        out_shape=jax.ShapeDtypeStruct((t, D_MODEL), F32),
        grid=(n_tiles,),
        in_specs=[pl.BlockSpec((tm * TOP_K,), lambda i: (i,), memory_space=pltpu.SMEM),
                  pl.BlockSpec((tm * TOP_K,), lambda i: (jnp.minimum(i + 1, n_tiles - 1),), memory_space=pltpu.SMEM),
                  pl.BlockSpec((tm, TOP_K), lambda i: (i, 0)),
                  pl.BlockSpec((tm, D_MODEL), lambda i: (i, 0)),
                  full(mod), full(lw2), full(lb2),
                  pl.BlockSpec(memory_space=pl.ANY)],
        out_specs=pl.BlockSpec((tm, D_MODEL), lambda i: (i, 0)),
        scratch_shapes=[pltpu.VMEM((2, TOP_K, tm * ROW_CHUNKS, LANES), F32), pltpu.SemaphoreType.DMA((2,))],
        compiler_params=_cparams(("arbitrary",)),
        name="moe_combine",
    )(dest_flat, dest_flat, gates, xs, mod, lw2, lb2, yb)


def _moe(xs, mod, layer, w_router, b_router, w_gate_up, b_gate_up, w_down, b_down, ln_w, ln_b, n_ctx_tiles, alpha):
    t = xs.shape[0]
    bm = MOE_BM
    idx, gates, rank, cnt = _router(xs, mod, w_router, b_router, n_ctx_tiles)
    counts = cnt[:, 0]
    idx, gates, rank = idx[:TOP_K].T, gates[:TOP_K].T, rank[:TOP_K].T
    padded = (counts + bm - 1) // bm * bm
    pad_end = jnp.cumsum(padded).astype(jnp.int32)
    pad_start = pad_end - padded
    dest = pad_start[idx] + rank
    dest_flat = dest.reshape(-1).astype(jnp.int32)
    n_blocks = -(-(t * TOP_K) // bm) + N_EXPERTS
    blk0 = jnp.arange(n_blocks, dtype=jnp.int32) * bm
    block_exp = jnp.minimum(jnp.sum(pad_end[None, :] <= blk0[:, None], axis=1), N_EXPERTS - 1).astype(jnp.int32)
    block_valid = (blk0 < pad_end[-1]).astype(jnp.int32)
    xb = _dispatch(pad_start, pad_end, dest_flat, xs, mod, n_blocks * bm, n_ctx_tiles)
    yb = _experts(block_exp, block_valid, xb, layer, w_gate_up, b_gate_up, w_down, b_down)
    return _combine(dest_flat, gates, xs, mod, ln_w, ln_b, yb, n_ctx_tiles, alpha)


def kernel(x, c, ctx, c_ctx, w_ada, b_ada, w_in, mlstm_gate_bias, diff_lambda, diff_norm_w, mlstm_norm_w, w_out,
           ln1_w, ln1_b, w_router, b_router, w_gate_up, b_gate_up, w_down, b_down, ln2_w, ln2_b):
    assert x.shape[0] == 1 and ctx.shape[0] == 1
    depth = w_ada.shape[0]
    n_lat, n_ctx = x.shape[1], ctx.shape[1]
    assert n_ctx % TOK_TILE == 0 and n_lat % TOK_TILE == 0
    nct = n_ctx // TOK_TILE
    alpha = (2 * depth) ** 0.25

    vecs = jnp.zeros((8, D_MODEL), F32).at[0].set(c[0]).at[1].set(c_ctx)
    mods = _ada(vecs, w_ada, b_ada)
    cos, sin = _rope_tables(n_lat, n_ctx)
    xt = jnp.concatenate([ctx[0], x[0]], axis=0)

    for l in range(depth):
        need_ctx = l < depth - 1
        lam_init = 0.8 - 0.6 * math.exp(-0.3 * l)
        mod = mods[l]
        w_ext, w_vt = _prep_w_in(w_in[l])
        q, k, vt, mkt, mq, mk, mv, mo, g, cu = _inproj(xt, mod, w_ext, w_vt, cos, sin, mlstm_gate_bias[l], n_ctx)

        dl = diff_lambda[l].astype(F32)
        lam = (jnp.exp(jnp.sum(dl[0] * dl[1])) - jnp.exp(jnp.sum(dl[2] * dl[3])) + lam_init).reshape(1)
        post = 1.0 - lam_init
        attn_l = _attention(lam, q, k, vt, diff_norm_w[l], n_ctx, n_lat, n_ctx + n_lat, post)
        four_l = _fourier_lat(cu[n_ctx:])
        if need_ctx:
            attn_c = _attention(lam, q, k, vt, diff_norm_w[l], 0, n_ctx, n_ctx, post)
            four_c = _fourier_small(cu[:n_ctx])
        else:
            attn_c, four_c = attn_l, four_l
        hf, hb = _mlstm(mq, mk, mkt, mv, g, n_ctx)
        xs = _outproj(attn_l, attn_c, four_l, four_c, hf, hb, mo, xt, mod, mlstm_norm_w[l], w_out[l].astype(BF16),
                      ln1_w[l], ln1_b[l], n_ctx, need_ctx, alpha)
        xt = _moe(xs, mod, l, w_router[l], b_router[l], w_gate_up, b_gate_up, w_down, b_down, ln2_w[l], ln2_b[l],
                  nct if need_ctx else 0, alpha)
    return xt[None]
```

```python
import functools
import math

import numpy as np
import jax
import jax.numpy as jnp
from jax import lax
from jax.experimental import pallas as pl
from jax.experimental.pallas import tpu as pltpu

F32 = jnp.float32
BF16 = jnp.bfloat16
HI = lax.Precision.HIGHEST

D_MODEL = 1024
GRID_W = 64
HEAD_DIM = 64
A_HEADS = 4
A_QK_DIM = HEAD_DIM
A_V_DIM = 2 * HEAD_DIM
A_WIDTH = A_HEADS * A_V_DIM
A_SCALE = A_QK_DIM ** -0.5
B_HEADS = 4
B_HEAD_DIM = HEAD_DIM
B_WIDTH = B_HEADS * B_HEAD_DIM
C_GROUPS = 4
C_GROUP_DIM = HEAD_DIM
C_WIDTH = C_GROUPS * C_GROUP_DIM
IN_SIZES = (A_WIDTH, A_WIDTH, A_WIDTH, B_WIDTH, B_WIDTH, B_WIDTH, B_WIDTH, 4 * B_HEADS, C_WIDTH)
ROPE_BASE = 10000.0
MLSTM_CHUNK = 64
N_EXPERTS = 32
TOP_K = 4
D_EXPERT = D_MODEL
SWIGLU_LIMIT = 7.0
SWIGLU_ALPHA = 1.702
EPS = 1e-5

LANES = 128
VMEM_LIMIT = 56 * 1024 * 1024
TOK_TILE = 256
ATT_TQ = 512
ATT_COLS = 256
ATT_UNROLL = 8
MOE_BM = 256

_OQ, _OQR, _OK, _OKR = 0, 512, 1024, 1536
_OMQ, _OMK, _OMV, _OMO, _OG, _OCU = 2048, 2304, 2560, 2816, 3072, 3200
Q_SCALE = A_SCALE * math.log2(math.e)
VT_ROWS = A_V_DIM + 16


def _cparams(sem, vmem=VMEM_LIMIT):
    return pltpu.CompilerParams(dimension_semantics=sem, vmem_limit_bytes=vmem)


def _ada_kernel(v_ref, w_ref, b_ref, o_ref):
    v = v_ref[...]
    s = v * jax.nn.sigmoid(v)
    o_ref[...] = jnp.dot(s, w_ref[...], precision=HI, preferred_element_type=F32) + b_ref[...]


def _ada(vecs, w_ada, b_ada):
    depth, d, d6 = w_ada.shape
    bn = 1024
    return pl.pallas_call(
        _ada_kernel,
        out_shape=jax.ShapeDtypeStruct((depth, 8, d6), F32),
        grid=(depth, d6 // bn),
        in_specs=[pl.BlockSpec((8, d), lambda l, j: (0, 0)),
                  pl.BlockSpec((None, d, bn), lambda l, j: (l, 0, j)),
                  pl.BlockSpec((None, 1, bn), lambda l, j: (l, 0, j))],
        out_specs=pl.BlockSpec((None, 8, bn), lambda l, j: (l, 0, j)),
        compiler_params=_cparams(("arbitrary", "arbitrary")),
        name="ada_modulation",
    )(vecs, w_ada, b_ada.reshape(depth, 1, d6))


def _mod_rows(mod_ref, k, is_ctx):
    lat = mod_ref[0:1, k * D_MODEL:(k + 1) * D_MODEL]
    ctx = mod_ref[1:2, k * D_MODEL:(k + 1) * D_MODEL]
    return jnp.where(is_ctx, ctx, lat)


def _inproj_kernel(x_ref, mod_ref, w_ref, wvt_ref, cos_ref, sin_ref, gb_ref,
                   q_ref, k_ref, vt_ref, mkt_ref, mq_ref, mk_ref, mv_ref, mo_ref, g_ref, cu_ref, *, n_ctx_tiles):
    is_ctx = pl.program_id(0) < n_ctx_tiles
    sh = _mod_rows(mod_ref, 0, is_ctx)
    sc = _mod_rows(mod_ref, 1, is_ctx)
    hm = (x_ref[...] * (1.0 + sc) + sh).astype(BF16)

    def proj(off, width):
        return jnp.dot(hm, w_ref[:, off:off + width], preferred_element_type=F32)

    cos = cos_ref[...]
    sin = sin_ref[...]
    for j in range(A_HEADS):
        o = j * LANES
        q = proj(_OQ + o, LANES) * cos + proj(_OQR + o, LANES) * sin
        q_ref[:, o:o + LANES] = (q * Q_SCALE).astype(BF16)
        k = proj(_OK + o, LANES) * cos + proj(_OKR + o, LANES) * sin
        k_ref[:, o:o + LANES] = k.astype(BF16)
    tr = lax.dot_general(wvt_ref[...], hm, (((1,), (1,)), ((), ())), preferred_element_type=F32)
    mkt_ref[...] = tr[A_WIDTH:, :] * (B_HEAD_DIM ** -0.5)
    vt = tr[:A_WIDTH, :].astype(BF16)
    for j in range(A_HEADS):
        vt_ref[j, 0:A_V_DIM, :] = vt[j * A_V_DIM:(j + 1) * A_V_DIM, :]
        vt_ref[j, A_V_DIM:, :] = jnp.ones((VT_ROWS - A_V_DIM, vt.shape[1]), BF16)
    mq_ref[...] = proj(_OMQ, B_WIDTH)
    mk_ref[...] = proj(_OMK, B_WIDTH) * (B_HEAD_DIM ** -0.5)
    mv_ref[...] = proj(_OMV, B_WIDTH)
    mo_ref[...] = proj(_OMO, B_WIDTH)
    g_ref[...] = proj(_OG, LANES) + gb_ref[...]
    cu_ref[...] = proj(_OCU, C_WIDTH)


def _rot_cols(w):
    d, width = w.shape
    wr = w.reshape(d, width // 64, 2, 2, 16)
    rot = jnp.stack([-wr[:, :, :, 1, :], wr[:, :, :, 0, :]], axis=3)
    return rot.reshape(d, width)


def _prep_w_in(w_in):
    offs = np.cumsum((0,) + IN_SIZES)
    aq, ak, av, bq, bk, bv, bo, bg, cu = [w_in[:, offs[i]:offs[i + 1]] for i in range(9)]
    bg = jnp.pad(bg, ((0, 0), (0, LANES - bg.shape[1])))
    w = jnp.concatenate([aq, _rot_cols(aq), ak, _rot_cols(ak), bq, bk, bv, bo, bg, cu], axis=1)
    return w.astype(BF16), jnp.concatenate([av, bk], axis=1).T.astype(BF16)


def _rope_tables(n_lat, n_ctx):
    rows = n_lat // GRID_W
    axis_dim = A_QK_DIM // 2
    inv_freq = ROPE_BASE ** (-jnp.arange(0, axis_dim, 2, dtype=F32) / axis_dim)
    ang_r = jnp.arange(rows, dtype=F32)[:, None] * inv_freq
    ang_c = jnp.arange(GRID_W, dtype=F32)[:, None] * inv_freq

    def table(fn, ctx_val):
        fr = jnp.broadcast_to(fn(ang_r)[:, None, :], (rows, GRID_W, axis_dim // 2))
        fc = jnp.broadcast_to(fn(ang_c)[None, :, :], (rows, GRID_W, axis_dim // 2))
        lat = jnp.concatenate([fr, fr, fc, fc] * 2, axis=-1).reshape(n_lat, 2 * A_QK_DIM)
        return jnp.concatenate([jnp.full((n_ctx, 2 * A_QK_DIM), ctx_val, F32), lat], axis=0)

    return table(jnp.cos, 1.0), table(jnp.sin, 0.0)


def _inproj(xt, mod, w_ext, w_vt, cos, sin, gate_bias, n_ctx):
    t = xt.shape[0]
    tm = TOK_TILE
    gb = jnp.pad(gate_bias.reshape(1, -1), ((0, 0), (0, LANES - 4 * B_HEADS)))
    row = lambda w: pl.BlockSpec((tm, w), lambda i: (i, 0))
    full = lambda a: pl.BlockSpec(a.shape, lambda i: (0,) * a.ndim)
    sds = jax.ShapeDtypeStruct
    rows_out = [(A_WIDTH, BF16)] * 2 + [(B_WIDTH, F32)] * 4 + [(LANES, F32), (C_WIDTH, F32)]
    out_shape = [sds((t, w), dt) for w, dt in rows_out]
    out_specs = [row(w) for w, _ in rows_out]
    out_shape.insert(2, sds((t // tm, A_HEADS, VT_ROWS, tm), BF16))
    out_specs.insert(2, pl.BlockSpec((None, A_HEADS, VT_ROWS, tm), lambda i: (i, 0, 0, 0)))
    out_shape.insert(3, sds((t // tm, B_WIDTH, tm), F32))
    out_specs.insert(3, pl.BlockSpec((None, B_WIDTH, tm), lambda i: (i, 0, 0)))
    return pl.pallas_call(
        functools.partial(_inproj_kernel, n_ctx_tiles=n_ctx // tm),
        out_shape=out_shape,
        grid=(t // tm,),
        in_specs=[row(D_MODEL), full(mod), full(w_ext), full(w_vt), row(LANES), row(LANES), full(gb)],
        out_specs=out_specs,
        compiler_params=_cparams(("arbitrary",)),
        name="in_projection",
    )(xt, mod, w_ext, w_vt, cos, sin, gb)


def _attn_kernel(lam_ref, q_ref, k_ref, vt_ref, nw_ref, o_ref, acc_ref, s_ref, *, n_kv, post_scale):
    q = q_ref[...]
    tq = q.shape[0]
    tk = vt_ref.shape[-1]
    lane = lax.broadcasted_iota(jnp.int32, q.shape, 1)
    zero = jnp.zeros_like(q)
    q2 = jnp.concatenate([jnp.where(lane < A_QK_DIM, q, zero), jnp.where(lane >= A_QK_DIM, q, zero)], axis=0)
    acc_ref[...] = jnp.zeros_like(acc_ref)

    def scores(j, slot):
        kb = k_ref[pl.ds(pl.multiple_of(j * tk, tk), tk), :]
        st = lax.dot_general(kb, q2, (((1,), (1,)), ((), ())), preferred_element_type=F32)
        s_ref[slot] = st
        return jnp.max(st, axis=0, keepdims=True)

    def consume(j, slot, cmax, m):
        m_new = jnp.maximum(m, cmax)
        alpha = jnp.exp2(m - m_new)
        for h in range(2 * tq // ATT_COLS):
            cs = slice(h * ATT_COLS, (h + 1) * ATT_COLS)
            p = jnp.exp2(s_ref[slot, :, cs] - m_new[:, cs]).astype(BF16)
            r = jnp.dot(vt_ref[j], p, preferred_element_type=F32)
            acc_ref[:, cs] = alpha[:, cs] * acc_ref[:, cs] + r
        return m_new

    cm0 = scores(0, 0)

    def group(t, carry):
        cm, m = carry
        a = ATT_UNROLL * t
        for i in range(ATT_UNROLL):
            cm_next = scores(a + i + 1, (i + 1) % 2)
            m = consume(a + i, i % 2, cm, m)
            cm = cm_next
        return cm, m

    assert (n_kv - 1) % ATT_UNROLL == 0
    cm_last, m = lax.fori_loop(0, (n_kv - 1) // ATT_UNROLL, group, (cm0, jnp.full((1, 2 * tq), -jnp.inf, F32)))
    consume(n_kv - 1, 0, cm_last, m)
    l = acc_ref[A_V_DIM:A_V_DIM + 1, :]
    acc = acc_ref[0:A_V_DIM, :]
    ot = acc[:, :tq] * (1.0 / l[:, :tq]) - acc[:, tq:] * (lam_ref[0] / l[:, tq:])
    yt = ot * lax.rsqrt(jnp.mean(ot * ot, axis=0, keepdims=True) + EPS)
    o_ref[...] = yt.T * nw_ref[...] * post_scale


def _attention(lam, q, k, vt, norm_w, q_row0, n_q, n_keys, post_scale):
    tq = min(ATT_TQ, n_q)
    tk = vt.shape[-1]
    n_kv = n_keys // tk
    if q_row0 % tq:
        q, q_row0 = q[q_row0:q_row0 + n_q], 0
    q_blk0 = q_row0 // tq
    return pl.pallas_call(
        functools.partial(_attn_kernel, n_kv=n_kv, post_scale=post_scale),
        out_shape=jax.ShapeDtypeStruct((n_q, A_WIDTH), F32),
        grid=(A_HEADS, n_q // tq),
        in_specs=[pl.BlockSpec(memory_space=pltpu.SMEM),
                  pl.BlockSpec((tq, LANES), lambda h, i: (i + q_blk0, h)),
                  pl.BlockSpec((n_keys, LANES), lambda h, i: (0, h)),
                  pl.BlockSpec((n_kv, None, VT_ROWS, tk), lambda h, i: (0, h, 0, 0)),
                  pl.BlockSpec((1, LANES), lambda h, i: (0, h))],
        out_specs=pl.BlockSpec((tq, LANES), lambda h, i: (i, h)),
        scratch_shapes=[pltpu.VMEM((VT_ROWS, 2 * tq), F32), pltpu.VMEM((2, tk, 2 * tq), F32)],
        compiler_params=_cparams(("arbitrary", "arbitrary")),
        name="diff_attention",
    )(lam, q, k, vt, norm_w.reshape(1, A_WIDTH))


def _mlstm_kernel(qf, kf, ktf, vf, gf, qb, kb, ktb, vb, gb, of_ref, ob_ref, s_ref, m_ref, *, chunks):
    L = MLSTM_CHUNK
    tm = chunks * L
    neg = -jnp.inf

    @pl.when(pl.program_id(0) == 0)
    def _():
        s_ref[...] = jnp.zeros_like(s_ref)
        m_ref[...] = jnp.zeros_like(m_ref)

    tr = lax.broadcasted_iota(jnp.int32, (tm, tm), 0)
    tc = lax.broadcasted_iota(jnp.int32, (tm, tm), 1)
    same_chunk = tr // L == tc // L
    masks = (same_chunk & (tr >= tc), same_chunk & (tr <= tc))
    row_chunk = lax.broadcasted_iota(jnp.int32, (tm, 1), 0) // L
    col_chunk = lax.broadcasted_iota(jnp.int32, (1, tm), 1) // L
    lane_half = lax.broadcasted_iota(jnp.int32, (tm, LANES), 1) // B_HEAD_DIM
    row_half = lax.broadcasted_iota(jnp.int32, (LANES, 1), 0) // B_HEAD_DIM
    rr = lax.broadcasted_iota(jnp.int32, (LANES, 2 * LANES), 0) // B_HEAD_DIM
    cc = lax.broadcasted_iota(jnp.int32, (LANES, 2 * LANES), 1)
    blockmask = jnp.where(cc < LANES, cc // B_HEAD_DIM, cc - LANES) == rr
    aug = jnp.where(lax.broadcasted_iota(jnp.int32, (tm, LANES), 1) < 2, 1.0, 0.0).astype(F32)
    io = ((qf, kf, ktf, vf, gf, of_ref), (qb, kb, ktb, vb, gb, ob_ref))

    def by_chunk(vals, chunk_ids):
        out = vals[chunks - 1]
        for c in range(chunks - 2, -1, -1):
            out = jnp.where(chunk_ids == c, vals[c], out)
        return out

    dp = [(d, p) for d in range(2) for p in range(2)]
    dph = [(d, p, hh) for d, p in dp for hh in range(2)]

    g_tile, b_tile, g_t, b_t = {}, {}, {}, {}
    for d in range(2):
        g_tile[d] = io[d][4][...]
        b_tile[d] = jnp.dot(masks[d].astype(F32), jax.nn.log_sigmoid(g_tile[d]), precision=HI,
                            preferred_element_type=F32)
        g_t[d] = g_tile[d].T
        b_t[d] = b_tile[d].T
    qp, kp, v_aug, ktp = {}, {}, {}, {}
    for d, p in dp:
        sl = slice(p * LANES, (p + 1) * LANES)
        qp[d, p] = io[d][0][:, sl]
        kp[d, p] = io[d][1][:, sl].astype(BF16)
        ktp[d, p] = io[d][2][sl, :]
        v_aug[d, p] = jnp.concatenate([io[d][3][:, sl], aug], axis=1).astype(BF16)

    b_col, b_row, i_row, gs, a_row, m_loc, w_t = {}, {}, {}, {}, {}, {}, {}
    for d, p, hh in dph:
        h = 2 * p + hh
        il, fl = 4 * d + h, 8 + 4 * d + h
        last = L - 1 if d == 0 else 0
        b_col[d, p, hh] = b_tile[d][:, fl:fl + 1]
        b_row[d, p, hh] = b_t[d][fl:fl + 1, :]
        i_row[d, p, hh] = g_t[d][il:il + 1, :]
        gs[d, p, hh] = [b_row[d, p, hh][:, c * L + last:c * L + last + 1] for c in range(chunks)]
        a_row[d, p, hh] = by_chunk(gs[d, p, hh], col_chunk) - b_row[d, p, hh] + i_row[d, p, hh]
    for k3 in dph:
        m_loc[k3] = [jnp.max(jnp.where(col_chunk == c, a_row[k3], neg), axis=-1, keepdims=True)
                     for c in range(chunks)]
    for k3 in dph:
        w_t[k3] = jnp.exp(a_row[k3] - by_chunk(m_loc[k3], col_chunk))

    upd, sqk = {}, {}
    for d, p in dp:
        kw_t = ktp[d, p] * jnp.where(row_half == 0, w_t[d, p, 0], w_t[d, p, 1])
        for c in range(chunks):
            kw_c = jnp.where(col_chunk == c, kw_t, 0.0).astype(BF16)
            upd[d, p, c] = jnp.where(blockmask, jnp.dot(kw_c, v_aug[d, p], preferred_element_type=F32), 0.0)
    for d, p, hh in dph:
        qm = jnp.where(lane_half == hh, qp[d, p], 0.0).astype(BF16)
        sqk[d, p, hh] = lax.dot_general(qm, kp[d, p], (((1,), (1,)), ((), ())), preferred_element_type=F32)

    m_old = m_ref[...]
    orders = (range(chunks), range(chunks - 1, -1, -1))
    m_at, dec, inj, m_end = {}, {}, {}, {}
    for d, p, hh in dph:
        row = 4 * d + 2 * p + hh
        m_prev = m_old[row:row + 1, 0:1]
        for c in orders[d]:
            m_at[d, p, hh, c] = m_prev
            g_c = gs[d, p, hh][c]
            m_new = jnp.maximum(g_c + m_prev, m_loc[d, p, hh][c])
            dec[d, p, hh, c] = jnp.exp(g_c + m_prev - m_new)
            inj[d, p, hh, c] = jnp.exp(m_loc[d, p, hh][c] - m_new)
            m_prev = m_new
        m_end[row] = m_prev
    state_at, state_end = {}, {}
    for d, p in dp:
        state = s_ref[d, p]
        for c in orders[d]:
            state_at[d, p, c] = state
            state = (jnp.where(rr == 0, dec[d, p, 0, c], dec[d, p, 1, c]) * state
                     + jnp.where(rr == 0, inj[d, p, 0, c], inj[d, p, 1, c]) * upd[d, p, c])
        state_end[d, p] = state
    for d, p in dp:
        s_ref[d, p] = state_end[d, p]
    m_rows = lax.broadcasted_iota(jnp.int32, m_old.shape, 0)
    m_out = m_old
    for row, val in m_end.items():
        m_out = jnp.where(m_rows == row, val, m_out)
    m_ref[...] = m_out

    dmat = {k3: jnp.where(masks[k3[0]], b_col[k3] + (i_row[k3] - b_row[k3]), neg) for k3 in dph}
    m_row = {k3: jnp.max(dmat[k3], axis=-1, keepdims=True) for k3 in dph}
    s_loc = {k3: (sqk[k3] * jnp.exp(dmat[k3] - m_row[k3])).astype(BF16) for k3 in dph}

    r = {}
    for d, p, hh in dph:
        r[d, p, hh] = jnp.dot(s_loc[d, p, hh], v_aug[d, p], preferred_element_type=F32)
    q2 = {}
    for d, p in dp:
        q2[d, p] = jnp.concatenate(
            [jnp.dot(qp[d, p][c * L:(c + 1) * L].astype(BF16), state_at[d, p, c].astype(BF16),
                     preferred_element_type=F32) for c in range(chunks)], axis=0)

    inter = {k3: b_col[k3] + by_chunk([m_at[k3 + (c,)] for c in range(chunks)], row_chunk) for k3 in dph}
    m_col = {k3: jnp.maximum(inter[k3], m_row[k3]) for k3 in dph}
    e_loc = {k3: jnp.exp(m_row[k3] - m_col[k3]) for k3 in dph}
    e_int = {k3: jnp.exp(inter[k3] - m_col[k3]) for k3 in dph}
    floor = {k3: jnp.exp(-m_col[k3]) for k3 in dph}
    den = {(d, p, hh): e_loc[d, p, hh] * r[d, p, hh][:, LANES:LANES + 1]
           + e_int[d, p, hh] * q2[d, p][:, LANES + hh:LANES + hh + 1] for d, p, hh in dph}
    inv = {k3: 1.0 / jnp.maximum(jnp.abs(den[k3]), floor[k3]) for k3 in dph}
    a_loc = {k3: e_loc[k3] * inv[k3] for k3 in dph}
    a_int = {k3: e_int[k3] * inv[k3] for k3 in dph}
    head_out = {(d, p, hh): a_loc[d, p, hh] * r[d, p, hh][:, :LANES] + a_int[d, p, hh] * q2[d, p][:, :LANES]
                for d, p, hh in dph}
    for d in range(2):
        io[d][5][...] = jnp.concatenate(
            [jnp.where(lane_half == 0, head_out[d, p, 0], head_out[d, p, 1]) for p in range(2)], axis=1)


def _mlstm(mq, mk, mkt, mv, g, n_ctx):
    t = mq.shape[0]
    tm = TOK_TILE
    n_tiles = t // tm
    n_ctx_tiles = n_ctx // tm
    mirror = lambda i: jnp.where(i < n_ctx_tiles, n_ctx_tiles - 1 - i, n_tiles - 1 - (i - n_ctx_tiles))
    fwd = lambda i: (i, 0)
    bwd = lambda i: (mirror(i), 0)
    spec = lambda w, im: pl.BlockSpec((tm, w), im)
    tspec = lambda f: pl.BlockSpec((None, B_WIDTH, tm), lambda i: (f(i), 0, 0))
    ins = ([spec(B_WIDTH, fwd)] * 2 + [tspec(lambda i: i), spec(B_WIDTH, fwd), spec(LANES, fwd)]
           + [spec(B_WIDTH, bwd)] * 2 + [tspec(mirror), spec(B_WIDTH, bwd), spec(LANES, bwd)])
    return pl.pallas_call(
        functools.partial(_mlstm_kernel, chunks=tm // MLSTM_CHUNK),
        out_shape=[jax.ShapeDtypeStruct((t, B_WIDTH), F32)] * 2,
        grid=(n_tiles,),
        in_specs=ins,
        out_specs=[spec(B_WIDTH, fwd), spec(B_WIDTH, bwd)],
        scratch_shapes=[pltpu.VMEM((2, 2, LANES, 2 * LANES), F32), pltpu.VMEM((8, LANES), F32)],
        compiler_params=_cparams(("arbitrary",)),
        name="mlstm_scan",
    )(mq, mk, mkt, mv, g, mq, mk, mkt, mv, g)


def _dft_cs(n):
    j = np.arange(n, dtype=np.int64)
    ang = 2.0 * np.pi * ((j[:, None] * j[None, :]) % n).astype(np.float64) / n
    s = 1.0 / math.sqrt(n)
    return np.cos(ang) * s, np.sin(ang) * s


def _chan_mats():
    c, s = _dft_cs(C_GROUP_DIM)
    eye = np.eye(C_GROUPS)
    return np.concatenate([np.kron(eye, c), np.kron(eye, s)], axis=0).astype(np.float32)


def _fft1_kernel(fa_ref, u_ref, z_ref):
    z_ref[...] = jnp.dot(fa_ref[...], u_ref[...], precision=HI, preferred_element_type=F32)


def _fft2_kernel(z_ref, twc_ref, tws_ref, fb_ref, ch_ref, o_ref, *, kb):
    n2 = fb_ref.shape[0] // 2
    for j in range(kb):
        zr = z_ref[0, j]
        zi = z_ref[1, j]
        tc = twc_ref[j]
        ts = tws_ref[j]
        st = jnp.concatenate([zr * tc + zi * ts, zi * tc - zr * ts], axis=0)
        a = jnp.dot(fb_ref[...], st, precision=HI, preferred_element_type=F32)
        ari = jnp.concatenate([a[:n2], a[n2:]], axis=1)
        o_ref[:, j * C_WIDTH:(j + 1) * C_WIDTH] = jnp.dot(ari, ch_ref[...], precision=HI,
                                                         preferred_element_type=F32)


def _fourier_lat(u):
    n = u.shape[0]
    n1 = int(round(math.sqrt(n)))
    assert n1 * n1 == n
    n2 = n1
    c, s = _dft_cs(n1)
    fa = jnp.asarray(np.concatenate([c, -s], axis=0).astype(np.float32))
    fb = jnp.asarray(np.block([[c, s], [-s, c]]).astype(np.float32))
    k1 = np.arange(n1, dtype=np.int64)
    ang = 2.0 * np.pi * ((k1[:, None] * k1[None, :]) % n).astype(np.float64) / n
    twc = jnp.asarray(np.cos(ang).astype(np.float32)).reshape(n1, n2, 1)
    tws = jnp.asarray(np.sin(ang).astype(np.float32)).reshape(n1, n2, 1)
    ch = jnp.asarray(_chan_mats())
    cols = n2 * C_WIDTH
    tb = min(2048, cols)
    z = pl.pallas_call(
        _fft1_kernel,
        out_shape=jax.ShapeDtypeStruct((2 * n1, cols), F32),
        grid=(cols // tb,),
        in_specs=[pl.BlockSpec((2 * n1, n1), lambda i: (0, 0)), pl.BlockSpec((n1, tb), lambda i: (0, i))],
        out_specs=pl.BlockSpec((2 * n1, tb), lambda i: (0, i)),
        compiler_params=_cparams(("arbitrary",)),
        name="fourier_stage1",
    )(fa, u.reshape(n1, cols))
    kb = 8
    out = pl.pallas_call(
        functools.partial(_fft2_kernel, kb=kb),
        out_shape=jax.ShapeDtypeStruct((n2, n1 * C_WIDTH), F32),
        grid=(n1 // kb,),
        in_specs=[pl.BlockSpec((2, kb, n2, C_WIDTH), lambda i: (0, i, 0, 0)),
                  pl.BlockSpec((kb, n2, 1), lambda i: (i, 0, 0)),
                  pl.BlockSpec((kb, n2, 1), lambda i: (i, 0, 0)),
                  pl.BlockSpec((2 * n2, 2 * n2), lambda i: (0, 0)),
                  pl.BlockSpec((2 * C_WIDTH, C_WIDTH), lambda i: (0, 0))],
        out_specs=pl.BlockSpec((n2, kb * C_WIDTH), lambda i: (0, i)),
        compiler_params=_cparams(("arbitrary",)),
        name="fourier_stage2",
    )(z.reshape(2, n1, n2, C_WIDTH), twc, tws, fb, ch)
    return out.reshape(n, C_WIDTH)


def _fft_small_kernel(fa_ref, u_ref, ch_ref, o_ref):
    n = u_ref.shape[0]
    a = jnp.dot(fa_ref[...], u_ref[...], precision=HI, preferred_element_type=F32)
    ari = jnp.concatenate([a[:n], a[n:]], axis=1)
    o_ref[...] = jnp.dot(ari, ch_ref[...], precision=HI, preferred_element_type=F32)


def _fourier_small(u):
    n = u.shape[0]
    c, s = _dft_cs(n)
    fa = jnp.asarray(np.concatenate([c, -s], axis=0).astype(np.float32))
    return pl.pallas_call(
        _fft_small_kernel,
        out_shape=jax.ShapeDtypeStruct((n, C_WIDTH), F32),
        compiler_params=_cparams(None),
        name="fourier_small",
    )(fa, u, jnp.asarray(_chan_mats()))


def _layer_norm(z, w, b):
    mu = jnp.mean(z, axis=-1, keepdims=True)
    zc = z - mu
    var = jnp.mean(zc * zc, axis=-1, keepdims=True)
    return zc * lax.rsqrt(var + EPS) * w + b


def _outproj_kernel(al_ref, ac_ref, fl_ref, fc_ref, hf_ref, hb_ref, bo_ref, x_ref, mod_ref, mnw_ref, seg_ref,
                    w_ref, lw_ref, lb_ref, o_ref, *, n_ctx_tiles, alpha):
    is_ctx = pl.program_id(0) < n_ctx_tiles
    attn = jnp.where(is_ctx, ac_ref[...], al_ref[...])
    four = jnp.where(is_ctx, fc_ref[...], fl_ref[...])
    h = hf_ref[...] + hb_ref[...]
    ms = jnp.dot(h * h, seg_ref[...], precision=HI, preferred_element_type=F32)
    y = h * lax.rsqrt(ms + EPS) * mnw_ref[...] * jax.nn.sigmoid(bo_ref[...])
    mix = (jnp.dot(attn.astype(BF16), w_ref[0:A_WIDTH, :], preferred_element_type=F32)
           + jnp.dot(y.astype(BF16), w_ref[A_WIDTH:A_WIDTH + B_WIDTH, :], preferred_element_type=F32)
           + jnp.dot(four.astype(BF16), w_ref[A_WIDTH + B_WIDTH:, :], preferred_element_type=F32))
    g1 = _mod_rows(mod_ref, 2, is_ctx)
    o_ref[...] = _layer_norm(alpha * x_ref[...] + g1 * mix, lw_ref[...], lb_ref[...])


def _outproj(attn_l, attn_c, four_l, four_c, hf, hb, bo, xt, mod, mnw, w_out_bf, ln_w, ln_b, n_ctx, with_ctx, alpha):
    tm = TOK_TILE
    nct = n_ctx // tm
    n_lat_tiles = attn_l.shape[0] // tm
    if with_ctx:
        n_tiles, uoff, n_ctx_tiles = n_lat_tiles + nct, 0, nct
        lat = lambda i: (jnp.maximum(i - nct, 0), 0)
    else:
        n_tiles, uoff, n_ctx_tiles = n_lat_tiles, nct, 0
        lat = lambda i: (i, 0)
    uni = lambda i: (i + uoff, 0)
    ctxm = lambda i: (jnp.minimum(i, nct - 1), 0)
    seg = jnp.asarray(np.kron(np.eye(B_HEADS), np.full((B_HEAD_DIM, B_HEAD_DIM), 1.0 / B_HEAD_DIM)).astype(np.float32))
    full = lambda a: pl.BlockSpec(a.shape, lambda i: (0,) * a.ndim)
    mnw2, lw2, lb2 = mnw.reshape(1, -1), ln_w.reshape(1, -1), ln_b.reshape(1, -1)
    return pl.pallas_call(
        functools.partial(_outproj_kernel, n_ctx_tiles=n_ctx_tiles, alpha=alpha),
        out_shape=jax.ShapeDtypeStruct((n_tiles * tm, D_MODEL), F32),
        grid=(n_tiles,),
        in_specs=[pl.BlockSpec((tm, A_WIDTH), lat), pl.BlockSpec((tm, A_WIDTH), ctxm),
                  pl.BlockSpec((tm, C_WIDTH), lat), pl.BlockSpec((tm, C_WIDTH), ctxm),
                  pl.BlockSpec((tm, B_WIDTH), uni), pl.BlockSpec((tm, B_WIDTH), uni),
                  pl.BlockSpec((tm, B_WIDTH), uni), pl.BlockSpec((tm, D_MODEL), uni),
                  full(mod), full(mnw2), full(seg), full(w_out_bf), full(lw2), full(lb2)],
        out_specs=pl.BlockSpec((tm, D_MODEL), lambda i: (i, 0)),
        compiler_params=_cparams(("arbitrary",)),
        name="out_projection",
    )(attn_l, attn_c, four_l, four_c, hf, hb, bo, xt, mod, mnw2, seg, w_out_bf, lw2, lb2)


def _router_kernel(x_ref, mod_ref, wrt_ref, br_ref, idx_ref, gate_ref, rank_ref, cnt_ref, run_ref, *, n_ctx_tiles):
    i = pl.program_id(0)

    @pl.when(i == 0)
    def _():
        run_ref[...] = jnp.zeros_like(run_ref)

    is_ctx = i < n_ctx_tiles
    f_in = x_ref[...] * (1.0 + _mod_rows(mod_ref, 4, is_ctx)) + _mod_rows(mod_ref, 3, is_ctx)
    logits = lax.dot_general(wrt_ref[...], f_in, (((1,), (1,)), ((), ())), precision=HI,
                             preferred_element_type=F32) + br_ref[...]
    tm = logits.shape[1]
    eidx = lax.broadcasted_iota(jnp.int32, logits.shape, 0).astype(F32)
    work = logits
    vals, sels = [], []
    for _ in range(TOP_K):
        mx = jnp.max(work, axis=0, keepdims=True)
        sel = jnp.min(jnp.where(work == mx, eidx, float(N_EXPERTS)), axis=0, keepdims=True)
        vals.append(mx)
        sels.append(sel)
        work = jnp.where(eidx == sel, -jnp.inf, work)
    es = [jnp.exp(v - vals[0]) for v in vals]
    tot = es[0] + es[1] + es[2] + es[3]
    onehots = [(eidx == s).astype(F32) for s in sels]
    oh_all = onehots[0] + onehots[1] + onehots[2] + onehots[3]
    r = lax.broadcasted_iota(jnp.int32, (tm, tm), 0)
    c = lax.broadcasted_iota(jnp.int32, (tm, tm), 1)
    before = jnp.dot(oh_all.astype(BF16), (r < c).astype(BF16), preferred_element_type=F32) + run_ref[:, 0:1]
    krow = lax.broadcasted_iota(jnp.int32, (8, tm), 0)
    idx_o = jnp.zeros((8, tm), jnp.int32)
    gate_o = jnp.zeros((8, tm), F32)
    rank_o = jnp.zeros((8, tm), jnp.int32)
    for k in range(TOP_K):
        rank = jnp.sum(onehots[k] * before, axis=0, keepdims=True).astype(jnp.int32)
        idx_o = jnp.where(krow == k, sels[k].astype(jnp.int32), idx_o)
        gate_o = jnp.where(krow == k, es[k] / tot, gate_o)
        rank_o = jnp.where(krow == k, rank, rank_o)
    idx_ref[...] = idx_o
    gate_ref[...] = gate_o
    rank_ref[...] = rank_o
    run_new = run_ref[...] + jnp.sum(oh_all, axis=1, keepdims=True)
    run_ref[...] = run_new
    cnt_ref[...] = run_new.astype(jnp.int32)


def _router(xs, mod, w_router, b_router, n_ctx_tiles):
    t = xs.shape[0]
    tm = TOK_TILE
    wrt = w_router.T
    br = b_router.reshape(-1, 1)
    full = lambda a: pl.BlockSpec(a.shape, lambda i: (0,) * a.ndim)
    col = pl.BlockSpec((8, tm), lambda i: (0, i))
    return pl.pallas_call(
        functools.partial(_router_kernel, n_ctx_tiles=n_ctx_tiles),
        out_shape=[jax.ShapeDtypeStruct((8, t), jnp.int32), jax.ShapeDtypeStruct((8, t), F32),
                   jax.ShapeDtypeStruct((8, t), jnp.int32), jax.ShapeDtypeStruct((N_EXPERTS, LANES), jnp.int32)],
        grid=(t // tm,),
        in_specs=[pl.BlockSpec((tm, D_MODEL), lambda i: (i, 0)), full(mod), full(wrt), full(br)],
        out_specs=[col, col, col, pl.BlockSpec((N_EXPERTS, LANES), lambda i: (0, 0))],
        scratch_shapes=[pltpu.VMEM((N_EXPERTS, LANES), F32)],
        compiler_params=_cparams(("arbitrary",)),
        name="moe_router",
    )(xs, mod, wrt, br)


ROW_CHUNKS = D_MODEL // LANES


def _to_chunk_rows(ref, val):
    m = val.shape[0]
    for c in range(ROW_CHUNKS):
        ref[pl.ds(c, m, stride=ROW_CHUNKS), :] = val[:, c * LANES:(c + 1) * LANES]


def _from_chunk_rows(ref, m):
    return jnp.concatenate([ref[pl.ds(c, m, stride=ROW_CHUNKS), :] for c in range(ROW_CHUNKS)], axis=1)


def _dispatch_kernel(pstart_ref, pend_ref, dest_ref, x0_ref, xn_ref, mod_ref, xb_ref, fbuf0, fbuf1, zbuf, sem, zsem, *,
                     n_ctx_tiles, bm):
    i = pl.program_id(0)
    fbuf = (fbuf0, fbuf1)
    n_steps = pl.num_programs(0)
    tm = xn_ref.shape[0]

    def staged(x, tile):
        is_ctx = tile < n_ctx_tiles
        return x * (1.0 + _mod_rows(mod_ref, 4, is_ctx)) + _mod_rows(mod_ref, 3, is_ctx)

    @pl.when(i == 0)
    def _():
        zbuf[...] = jnp.zeros_like(zbuf)

        def zcopy(e):
            row0 = pl.multiple_of((pend_ref[e] - bm) * ROW_CHUNKS, bm * ROW_CHUNKS)
            return pltpu.make_async_copy(zbuf, xb_ref.at[pl.ds(row0, bm * ROW_CHUNKS)], zsem)

        def zstart(e, c):
            @pl.when(pend_ref[e] > pstart_ref[e])
            def _():
                zcopy(e).start()
            return c

        def zwait(e, c):
            @pl.when(pend_ref[e] > pstart_ref[e])
            def _():
                zcopy(e).wait()
            return c

        lax.fori_loop(0, N_EXPERTS, zstart, 0)
        lax.fori_loop(0, N_EXPERTS, zwait, 0)

        def tcopy(b):
            row0 = pl.multiple_of(b * (bm * ROW_CHUNKS), bm * ROW_CHUNKS)
            return pltpu.make_async_copy(zbuf, xb_ref.at[pl.ds(row0, bm * ROW_CHUNKS)], zsem)

        def tstart(b, c):
            tcopy(b).start()
            return c

        def twait(b, c):
            tcopy(b).wait()
            return c

        tail0 = pend_ref[N_EXPERTS - 1] // bm
        n_blocks = xb_ref.shape[0] // (bm * ROW_CHUNKS)
        lax.fori_loop(tail0, n_blocks, tstart, 0)
        lax.fori_loop(tail0, n_blocks, twait, 0)
        _to_chunk_rows(fbuf[0], staged(x0_ref[...], 0))

    def wait_tile(s):
        for _ in range(TOP_K):
            pltpu.make_async_copy(fbuf[s], xb_ref.at[pl.ds(0, tm * ROW_CHUNKS)], sem.at[s]).wait()

    def step(slot):
        @pl.when(i > 0)
        def _():
            wait_tile(1 - slot)

        for t in range(tm):
            for k in range(TOP_K):
                dst = pl.multiple_of(dest_ref[t * TOP_K + k] * ROW_CHUNKS, ROW_CHUNKS)
                pltpu.make_async_copy(fbuf[slot].at[pl.ds(t * ROW_CHUNKS, ROW_CHUNKS)],
                                      xb_ref.at[pl.ds(dst, ROW_CHUNKS)], sem.at[slot]).start(priority=k % 2)
        _to_chunk_rows(fbuf[1 - slot], staged(xn_ref[...], jnp.minimum(i + 1, n_steps - 1)))

        @pl.when(i == n_steps - 1)
        def _():
            wait_tile(slot)

    for slot in range(2):
        pl.when(i % 2 == slot)(functools.partial(step, slot))


def _dispatch(pad_start, pad_end, dest_flat, xs, mod, n_rows, n_ctx_tiles):
    t = xs.shape[0]
    tm = TOK_TILE
    n_tiles = t // tm
    gs = pltpu.PrefetchScalarGridSpec(
        num_scalar_prefetch=2,
        grid=(t // tm,),
        in_specs=[pl.BlockSpec((tm * TOP_K,), lambda i, *_: (i,), memory_space=pltpu.SMEM),
                  pl.BlockSpec((tm, D_MODEL), lambda i, *_: (0, 0)),
                  pl.BlockSpec((tm, D_MODEL), lambda i, *_: (jnp.minimum(i + 1, n_tiles - 1), 0)),
                  pl.BlockSpec(mod.shape, lambda i, *_: (0, 0))],
        out_specs=pl.BlockSpec(memory_space=pl.ANY),
        scratch_shapes=[pltpu.VMEM((tm * ROW_CHUNKS, LANES), F32)] * 2
        + [pltpu.VMEM((MOE_BM * ROW_CHUNKS, LANES), F32), pltpu.SemaphoreType.DMA((2,)), pltpu.SemaphoreType.DMA],
    )
    return pl.pallas_call(
        functools.partial(_dispatch_kernel, n_ctx_tiles=n_ctx_tiles, bm=MOE_BM),
        out_shape=jax.ShapeDtypeStruct((n_rows * ROW_CHUNKS, LANES), F32),
        grid_spec=gs,
        compiler_params=_cparams(("arbitrary",)),
        name="moe_dispatch",
    )(pad_start, pad_end, dest_flat, xs, xs, mod)


def _expert_kernel(bexp_ref, bvalid_ref, x_ref, wgu_ref, bgu_ref, wd_ref, bd_ref, y_ref, wgu_bf, wd_bf):
    i = pl.program_id(0)
    e = bexp_ref[i]
    prev = bexp_ref[jnp.maximum(i - 1, 0)]

    @pl.when(jnp.logical_or(i == 0, e != prev))
    def _():
        wgu_bf[...] = wgu_ref[...].astype(BF16)
        wd_bf[...] = wd_ref[...].astype(BF16)

    @pl.when(bvalid_ref[i] > 0)
    def _():
        bm = y_ref.shape[0] // ROW_CHUNKS
        x = _from_chunk_rows(x_ref, bm).astype(BF16)
        gu = jnp.dot(x, wgu_bf[...], preferred_element_type=F32) + bgu_ref[...]
        gate = jnp.minimum(gu[:, :D_EXPERT], SWIGLU_LIMIT)
        up = jnp.clip(gu[:, D_EXPERT:], -SWIGLU_LIMIT, SWIGLU_LIMIT)
        act = gate * jax.nn.sigmoid(SWIGLU_ALPHA * gate) * (up + 1.0)
        _to_chunk_rows(y_ref, jnp.dot(act.astype(BF16), wd_bf[...], preferred_element_type=F32) + bd_ref[...])

    @pl.when(bvalid_ref[i] == 0)
    def _():
        y_ref[...] = jnp.zeros_like(y_ref)


def _experts(block_exp, block_valid, xb, layer, w_gate_up, b_gate_up, w_down, b_down):
    n_rows = xb.shape[0] // ROW_CHUNKS
    bm = MOE_BM
    depth = w_gate_up.shape[0]
    bgu = b_gate_up.reshape(depth, N_EXPERTS, 1, 2 * D_EXPERT)
    bd = b_down.reshape(depth, N_EXPERTS, 1, D_MODEL)
    gs = pltpu.PrefetchScalarGridSpec(
        num_scalar_prefetch=2,
        grid=(n_rows // bm,),
        in_specs=[pl.BlockSpec((bm * ROW_CHUNKS, LANES), lambda i, be, bv: (i, 0)),
                  pl.BlockSpec((None, None, D_MODEL, 2 * D_EXPERT), lambda i, be, bv: (layer, be[i], 0, 0)),
                  pl.BlockSpec((None, None, 1, 2 * D_EXPERT), lambda i, be, bv: (layer, be[i], 0, 0)),
                  pl.BlockSpec((None, None, D_EXPERT, D_MODEL), lambda i, be, bv: (layer, be[i], 0, 0)),
                  pl.BlockSpec((None, None, 1, D_MODEL), lambda i, be, bv: (layer, be[i], 0, 0))],
        out_specs=pl.BlockSpec((bm * ROW_CHUNKS, LANES), lambda i, be, bv: (i, 0)),
        scratch_shapes=[pltpu.VMEM((D_MODEL, 2 * D_EXPERT), BF16), pltpu.VMEM((D_EXPERT, D_MODEL), BF16)],
    )
    return pl.pallas_call(
        _expert_kernel,
        out_shape=jax.ShapeDtypeStruct((n_rows * ROW_CHUNKS, LANES), F32),
        grid_spec=gs,
        compiler_params=_cparams(("arbitrary",)),
        name="moe_experts",
    )(block_exp, block_valid, xb, w_gate_up, bgu, w_down, bd)


def _combine_kernel(dest_ref, dnext_ref, gate_ref, x_ref, mod_ref, lw_ref, lb_ref, yb_ref, o_ref, ybuf0, ybuf1, sem, *,
                    n_ctx_tiles, alpha):
    i = pl.program_id(0)
    tm = x_ref.shape[0]
    ybuf = (ybuf0, ybuf1)

    def gather(d_ref, s):
        for t in range(tm):
            for k in range(TOP_K):
                src = pl.multiple_of(d_ref[t * TOP_K + k] * ROW_CHUNKS, ROW_CHUNKS)
                pltpu.make_async_copy(yb_ref.at[pl.ds(src, ROW_CHUNKS)],
                                      ybuf[s].at[k, pl.ds(t * ROW_CHUNKS, ROW_CHUNKS)],
                                      sem.at[s]).start(priority=k % 2)

    def wait_tile(s):
        for k in range(TOP_K):
            pltpu.make_async_copy(yb_ref.at[pl.ds(0, tm * ROW_CHUNKS)], ybuf[s].at[k], sem.at[s]).wait()

    @pl.when(i == 0)
    def _():
        gather(dest_ref, 0)

    def step(slot):
        wait_tile(slot)
        gather(dnext_ref, 1 - slot)
        gates = gate_ref[...]
        y = gates[:, 0:1] * _from_chunk_rows(ybuf[slot].at[0], tm)
        for k in range(1, TOP_K):
            y = y + gates[:, k:k + 1] * _from_chunk_rows(ybuf[slot].at[k], tm)
        is_ctx = i < n_ctx_tiles
        g2 = _mod_rows(mod_ref, 5, is_ctx)
        o_ref[...] = _layer_norm(alpha * x_ref[...] + g2 * y, lw_ref[...], lb_ref[...])

        @pl.when(i == pl.num_programs(0) - 1)
        def _():
            wait_tile(1 - slot)

    for slot in range(2):
        pl.when(i % 2 == slot)(functools.partial(step, slot))


def _combine(dest_flat, gates, xs, mod, ln_w, ln_b, yb, n_ctx_tiles, alpha):
    t = xs.shape[0]
    tm = TOK_TILE
    n_tiles = t // tm
    lw2, lb2 = ln_w.reshape(1, -1), ln_b.reshape(1, -1)
    full = lambda a: pl.BlockSpec(a.shape, lambda i: (0,) * a.ndim)
    return pl.pallas_call(
        functools.partial(_combine_kernel, n_ctx_tiles=n_ctx_tiles, alpha=alpha),
        out_shape=jax.ShapeDtypeStruct((t, D_MODEL), F32),
        grid=(n_tiles,),
        in_specs=[pl.BlockSpec((tm * TOP_K,), lambda i: (i,), memory_space=pltpu.SMEM),
                  pl.BlockSpec((tm * TOP_K,), lambda i: (jnp.minimum(i + 1, n_tiles - 1),), memory_space=pltpu.SMEM),
                  pl.BlockSpec((tm, TOP_K), lambda i: (i, 0)),
                  pl.BlockSpec((tm, D_MODEL), lambda i: (i, 0)),
                  full(mod), full(lw2), full(lb2),
                  pl.BlockSpec(memory_space=pl.ANY)],
        out_specs=pl.BlockSpec((tm, D_MODEL), lambda i: (i, 0)),
        scratch_shapes=[pltpu.VMEM((TOP_K, tm * ROW_CHUNKS, LANES), F32)] * 2 + [pltpu.SemaphoreType.DMA((2,))],
        compiler_params=_cparams(("arbitrary",)),
        name="moe_combine",
    )(dest_flat, dest_flat, gates, xs, mod, lw2, lb2, yb)


def _moe(xs, mod, layer, w_router, b_router, w_gate_up, b_gate_up, w_down, b_down, ln_w, ln_b, n_ctx_tiles, alpha):
    t = xs.shape[0]
    bm = MOE_BM
    idx, gates, rank, cnt = _router(xs, mod, w_router, b_router, n_ctx_tiles)
    counts = cnt[:, 0]
    idx, gates, rank = idx[:TOP_K].T, gates[:TOP_K].T, rank[:TOP_K].T
    padded = (counts + bm - 1) // bm * bm
    pad_end = jnp.cumsum(padded).astype(jnp.int32)
    pad_start = pad_end - padded
    dest = pad_start[idx] + rank
    dest_flat = dest.reshape(-1).astype(jnp.int32)
    n_blocks = -(-(t * TOP_K) // bm) + N_EXPERTS
    blk0 = jnp.arange(n_blocks, dtype=jnp.int32) * bm
    block_exp = jnp.minimum(jnp.sum(pad_end[None, :] <= blk0[:, None], axis=1), N_EXPERTS - 1).astype(jnp.int32)
    block_valid = (blk0 < pad_end[-1]).astype(jnp.int32)
    xb = _dispatch(pad_start, pad_end, dest_flat, xs, mod, n_blocks * bm, n_ctx_tiles)
    yb = _experts(block_exp, block_valid, xb, layer, w_gate_up, b_gate_up, w_down, b_down)
    return _combine(dest_flat, gates, xs, mod, ln_w, ln_b, yb, n_ctx_tiles, alpha)


def kernel(x, c, ctx, c_ctx, w_ada, b_ada, w_in, mlstm_gate_bias, diff_lambda, diff_norm_w, mlstm_norm_w, w_out,
           ln1_w, ln1_b, w_router, b_router, w_gate_up, b_gate_up, w_down, b_down, ln2_w, ln2_b):
    assert x.shape[0] == 1 and ctx.shape[0] == 1
    depth = w_ada.shape[0]
    n_lat, n_ctx = x.shape[1], ctx.shape[1]
    assert n_ctx % TOK_TILE == 0 and n_lat % TOK_TILE == 0
    nct = n_ctx // TOK_TILE
    alpha = (2 * depth) ** 0.25

    vecs = jnp.zeros((8, D_MODEL), F32).at[0].set(c[0]).at[1].set(c_ctx)
    mods = _ada(vecs, w_ada, b_ada)
    cos, sin = _rope_tables(n_lat, n_ctx)
    xt = jnp.concatenate([ctx[0], x[0]], axis=0)

    for l in range(depth):
        need_ctx = l < depth - 1
        lam_init = 0.8 - 0.6 * math.exp(-0.3 * l)
        mod = mods[l]
        w_ext, w_vt = _prep_w_in(w_in[l])
        q, k, vt, mkt, mq, mk, mv, mo, g, cu = _inproj(xt, mod, w_ext, w_vt, cos, sin, mlstm_gate_bias[l], n_ctx)

        dl = diff_lambda[l].astype(F32)
        lam = (jnp.exp(jnp.sum(dl[0] * dl[1])) - jnp.exp(jnp.sum(dl[2] * dl[3])) + lam_init).reshape(1)
        post = 1.0 - lam_init
        attn_l = _attention(lam, q, k, vt, diff_norm_w[l], n_ctx, n_lat, n_ctx + n_lat, post)
        four_l = _fourier_lat(cu[n_ctx:])
        if need_ctx:
            attn_c = _attention(lam, q, k, vt, diff_norm_w[l], 0, n_ctx, n_ctx, post)
            four_c = _fourier_small(cu[:n_ctx])
        else:
            attn_c, four_c = attn_l, four_l
        hf, hb = _mlstm(mq, mk, mkt, mv, g, n_ctx)
        xs = _outproj(attn_l, attn_c, four_l, four_c, hf, hb, mo, xt, mod, mlstm_norm_w[l], w_out[l].astype(BF16),
                      ln1_w[l], ln1_b[l], n_ctx, need_ctx, alpha)
        xt = _moe(xs, mod, l, w_router[l], b_router[l], w_gate_up, b_gate_up, w_down, b_down, ln2_w[l], ln2_b[l],
                  nct if need_ctx else 0, alpha)
    return xt[None]
```

```python
import functools
import math

import numpy as np
import jax
import jax.numpy as jnp
from jax import lax
from jax.experimental import pallas as pl
from jax.experimental.pallas import tpu as pltpu

F32 = jnp.float32
BF16 = jnp.bfloat16
HI = lax.Precision.HIGHEST

D_MODEL = 1024
GRID_W = 64
HEAD_DIM = 64
A_HEADS = 4
A_QK_DIM = HEAD_DIM
A_V_DIM = 2 * HEAD_DIM
A_WIDTH = A_HEADS * A_V_DIM
A_SCALE = A_QK_DIM ** -0.5
B_HEADS = 4
B_HEAD_DIM = HEAD_DIM
B_WIDTH = B_HEADS * B_HEAD_DIM
C_GROUPS = 4
C_GROUP_DIM = HEAD_DIM
C_WIDTH = C_GROUPS * C_GROUP_DIM
IN_SIZES = (A_WIDTH, A_WIDTH, A_WIDTH, B_WIDTH, B_WIDTH, B_WIDTH, B_WIDTH, 4 * B_HEADS, C_WIDTH)
ROPE_BASE = 10000.0
MLSTM_CHUNK = 64
N_EXPERTS = 32
TOP_K = 4
D_EXPERT = D_MODEL
SWIGLU_LIMIT = 7.0
SWIGLU_ALPHA = 1.702
EPS = 1e-5

LANES = 128
VMEM_LIMIT = 56 * 1024 * 1024
TOK_TILE = 256
ATT_TQ = 512
ATT_COLS = 256
ATT_UNROLL = 8
MOE_BM = 256

_OQ, _OK = 0, 512
_OMQ, _OMK, _OMV, _OMO, _OG, _OCU = 1024, 1280, 1536, 1792, 2048, 2176
Q_SCALE = A_SCALE * math.log2(math.e)
VT_ROWS = A_V_DIM + 16


def _cparams(sem, vmem=VMEM_LIMIT):
    return pltpu.CompilerParams(dimension_semantics=sem, vmem_limit_bytes=vmem)


def _ada_kernel(v_ref, w_ref, b_ref, o_ref):
    v = v_ref[...]
    s = v * jax.nn.sigmoid(v)
    o_ref[...] = jnp.dot(s, w_ref[...], precision=HI, preferred_element_type=F32) + b_ref[...]


def _ada(vecs, w_ada, b_ada):
    depth, d, d6 = w_ada.shape
    bn = 1024
    return pl.pallas_call(
        _ada_kernel,
        out_shape=jax.ShapeDtypeStruct((depth, 8, d6), F32),
        grid=(depth, d6 // bn),
        in_specs=[pl.BlockSpec((8, d), lambda l, j: (0, 0)),
                  pl.BlockSpec((None, d, bn), lambda l, j: (l, 0, j)),
                  pl.BlockSpec((None, 1, bn), lambda l, j: (l, 0, j))],
        out_specs=pl.BlockSpec((None, 8, bn), lambda l, j: (l, 0, j)),
        compiler_params=_cparams(("arbitrary", "arbitrary")),
        name="ada_modulation",
    )(vecs, w_ada, b_ada.reshape(depth, 1, d6))


def _mod_rows(mod_ref, k, is_ctx):
    lat = mod_ref[0:1, k * D_MODEL:(k + 1) * D_MODEL]
    ctx = mod_ref[1:2, k * D_MODEL:(k + 1) * D_MODEL]
    return jnp.where(is_ctx, ctx, lat)


def _inproj_kernel(x_ref, mod_ref, w_ref, wvt_ref, cos_ref, sin_ref, gb_ref,
                   q_ref, k_ref, vt_ref, mkt_ref, mq_ref, mk_ref, mv_ref, mo_ref, g_ref, cu_ref, *, n_ctx_tiles):
    is_ctx = pl.program_id(0) < n_ctx_tiles
    sh = _mod_rows(mod_ref, 0, is_ctx)
    sc = _mod_rows(mod_ref, 1, is_ctx)
    hm = (x_ref[...] * (1.0 + sc) + sh).astype(BF16)

    def proj(off, width):
        return jnp.dot(hm, w_ref[:, off:off + width], preferred_element_type=F32)

    cos = cos_ref[...]
    sin = sin_ref[...]
    quarter = A_QK_DIM // 4
    first_half = (lax.broadcasted_iota(jnp.int32, cos.shape, 1) % (2 * quarter)) < quarter

    def rope(t):
        rot = jnp.where(first_half, -pltpu.roll(t, LANES - quarter, 1), pltpu.roll(t, quarter, 1))
        return t * cos + rot * sin

    for j in range(A_HEADS):
        o = j * LANES
        q_ref[:, o:o + LANES] = (rope(proj(_OQ + o, LANES)) * Q_SCALE).astype(BF16)
        k_ref[:, o:o + LANES] = rope(proj(_OK + o, LANES)).astype(BF16)
    tr = lax.dot_general(wvt_ref[...], hm, (((1,), (1,)), ((), ())), preferred_element_type=F32)
    mkt_ref[...] = tr[A_WIDTH:, :] * (B_HEAD_DIM ** -0.5)
    vt = tr[:A_WIDTH, :].astype(BF16)
    for j in range(A_HEADS):
        vt_ref[j, 0:A_V_DIM, :] = vt[j * A_V_DIM:(j + 1) * A_V_DIM, :]
        vt_ref[j, A_V_DIM:, :] = jnp.ones((VT_ROWS - A_V_DIM, vt.shape[1]), BF16)
    mq_ref[...] = proj(_OMQ, B_WIDTH)
    mk_ref[...] = proj(_OMK, B_WIDTH) * (B_HEAD_DIM ** -0.5)
    mv_ref[...] = proj(_OMV, B_WIDTH)
    mo_ref[...] = proj(_OMO, B_WIDTH)
    g_ref[...] = proj(_OG, LANES) + gb_ref[...]
    cu_ref[...] = proj(_OCU, C_WIDTH)


def _prep_w_in(w_in):
    offs = np.cumsum((0,) + IN_SIZES)
    aq, ak, av, bq, bk, bv, bo, bg, cu = [w_in[:, offs[i]:offs[i + 1]] for i in range(9)]
    bg = jnp.pad(bg, ((0, 0), (0, LANES - bg.shape[1])))
    w = jnp.concatenate([aq, ak, bq, bk, bv, bo, bg, cu], axis=1)
    return w.astype(BF16), jnp.concatenate([av, bk], axis=1).T.astype(BF16)


def _rope_tables(n_lat, n_ctx):
    rows = n_lat // GRID_W
    axis_dim = A_QK_DIM // 2
    inv_freq = ROPE_BASE ** (-np.arange(0, axis_dim, 2, dtype=np.float64) / axis_dim)
    ang_r = np.repeat(np.arange(rows, dtype=np.float64), GRID_W)[:, None] * inv_freq
    ang_c = np.tile(np.arange(GRID_W, dtype=np.float64), rows)[:, None] * inv_freq
    ang = np.concatenate([ang_r, ang_r, ang_c, ang_c] * 2, axis=-1)
    cos = np.concatenate([np.ones((n_ctx, 2 * A_QK_DIM)), np.cos(ang)], axis=0)
    sin = np.concatenate([np.zeros((n_ctx, 2 * A_QK_DIM)), np.sin(ang)], axis=0)
    return jnp.asarray(cos.astype(np.float32)), jnp.asarray(sin.astype(np.float32))


def _inproj(xt, mod, w_ext, w_vt, cos, sin, gate_bias, n_ctx):
    t = xt.shape[0]
    tm = TOK_TILE
    gb = jnp.pad(gate_bias.reshape(1, -1), ((0, 0), (0, LANES - 4 * B_HEADS)))
    row = lambda w: pl.BlockSpec((tm, w), lambda i: (i, 0))
    full = lambda a: pl.BlockSpec(a.shape, lambda i: (0,) * a.ndim)
    sds = jax.ShapeDtypeStruct
    rows_out = [(A_WIDTH, BF16)] * 2 + [(B_WIDTH, F32)] * 4 + [(LANES, F32), (C_WIDTH, F32)]
    out_shape = [sds((t, w), dt) for w, dt in rows_out]
    out_specs = [row(w) for w, _ in rows_out]
    out_shape.insert(2, sds((t // tm, A_HEADS, VT_ROWS, tm), BF16))
    out_specs.insert(2, pl.BlockSpec((None, A_HEADS, VT_ROWS, tm), lambda i: (i, 0, 0, 0)))
    out_shape.insert(3, sds((t // tm, B_WIDTH, tm), F32))
    out_specs.insert(3, pl.BlockSpec((None, B_WIDTH, tm), lambda i: (i, 0, 0)))
    return pl.pallas_call(
        functools.partial(_inproj_kernel, n_ctx_tiles=n_ctx // tm),
        out_shape=out_shape,
        grid=(t // tm,),
        in_specs=[row(D_MODEL), full(mod), full(w_ext), full(w_vt), row(LANES), row(LANES), full(gb)],
        out_specs=out_specs,
        compiler_params=_cparams(("arbitrary",)),
        name="in_projection",
    )(xt, mod, w_ext, w_vt, cos, sin, gb)


def _attn_kernel(lam_ref, q_ref, k_ref, vt_ref, nw_ref, o_ref, acc_ref, s_ref, *, n_kv, post_scale):
    q = q_ref[...]
    tq = q.shape[0]
    tk = vt_ref.shape[-1]
    lane = lax.broadcasted_iota(jnp.int32, q.shape, 1)
    zero = jnp.zeros_like(q)
    q2 = jnp.concatenate([jnp.where(lane < A_QK_DIM, q, zero), jnp.where(lane >= A_QK_DIM, q, zero)], axis=0)
    acc_ref[...] = jnp.zeros_like(acc_ref)

    def scores(j, slot):
        kb = k_ref[pl.ds(pl.multiple_of(j * tk, tk), tk), :]
        st = lax.dot_general(kb, q2, (((1,), (1,)), ((), ())), preferred_element_type=F32)
        s_ref[slot] = st
        return jnp.max(st, axis=0, keepdims=True)

    def consume(j, slot, cmax, m):
        m_new = jnp.maximum(m, cmax)
        alpha = jnp.exp2(m - m_new)
        for h in range(2 * tq // ATT_COLS):
            cs = slice(h * ATT_COLS, (h + 1) * ATT_COLS)
            p = jnp.exp2(s_ref[slot, :, cs] - m_new[:, cs]).astype(BF16)
            r = jnp.dot(vt_ref[j], p, preferred_element_type=F32)
            acc_ref[:, cs] = alpha[:, cs] * acc_ref[:, cs] + r
        return m_new

    cm0 = scores(0, 0)

    def group(t, carry):
        cm, m = carry
        a = ATT_UNROLL * t
        for i in range(ATT_UNROLL):
            cm_next = scores(a + i + 1, (i + 1) % 2)
            m = consume(a + i, i % 2, cm, m)
            cm = cm_next
        return cm, m

    assert (n_kv - 1) % ATT_UNROLL == 0
    cm_last, m = lax.fori_loop(0, (n_kv - 1) // ATT_UNROLL, group, (cm0, jnp.full((1, 2 * tq), -jnp.inf, F32)))
    consume(n_kv - 1, 0, cm_last, m)
    l = acc_ref[A_V_DIM:A_V_DIM + 1, :]
    acc = acc_ref[0:A_V_DIM, :]
    ot = acc[:, :tq] * (1.0 / l[:, :tq]) - acc[:, tq:] * (lam_ref[0] / l[:, tq:])
    yt = ot * lax.rsqrt(jnp.mean(ot * ot, axis=0, keepdims=True) + EPS)
    o_ref[...] = yt.T * nw_ref[...] * post_scale


def _attention(lam, q, k, vt, norm_w, q_row0, n_q, n_keys, post_scale):
    tq = min(ATT_TQ, n_q)
    tk = vt.shape[-1]
    n_kv = n_keys // tk
    if q_row0 % tq:
        q, q_row0 = q[q_row0:q_row0 + n_q], 0
    q_blk0 = q_row0 // tq
    return pl.pallas_call(
        functools.partial(_attn_kernel, n_kv=n_kv, post_scale=post_scale),
        out_shape=jax.ShapeDtypeStruct((n_q, A_WIDTH), F32),
        grid=(A_HEADS, n_q // tq),
        in_specs=[pl.BlockSpec(memory_space=pltpu.SMEM),
                  pl.BlockSpec((tq, LANES), lambda h, i: (i + q_blk0, h)),
                  pl.BlockSpec((n_keys, LANES), lambda h, i: (0, h)),
                  pl.BlockSpec((n_kv, None, VT_ROWS, tk), lambda h, i: (0, h, 0, 0)),
                  pl.BlockSpec((1, LANES), lambda h, i: (0, h))],
        out_specs=pl.BlockSpec((tq, LANES), lambda h, i: (i, h)),
        scratch_shapes=[pltpu.VMEM((VT_ROWS, 2 * tq), F32), pltpu.VMEM((2, tk, 2 * tq), F32)],
        compiler_params=_cparams(("arbitrary", "arbitrary")),
        name="diff_attention",
    )(lam, q, k, vt, norm_w.reshape(1, A_WIDTH))


def _mlstm_kernel(qf, kf, ktf, vf, gf, qb, kb, ktb, vb, gb, of_ref, ob_ref, s_ref, m_ref, *, chunks):
    L = MLSTM_CHUNK
    tm = chunks * L
    neg = -jnp.inf

    @pl.when(pl.program_id(0) == 0)
    def _():
        s_ref[...] = jnp.zeros_like(s_ref)
        m_ref[...] = jnp.zeros_like(m_ref)

    tr = lax.broadcasted_iota(jnp.int32, (tm, tm), 0)
    tc = lax.broadcasted_iota(jnp.int32, (tm, tm), 1)
    same_chunk = tr // L == tc // L
    masks = (same_chunk & (tr >= tc), same_chunk & (tr <= tc))
    row_chunk = lax.broadcasted_iota(jnp.int32, (tm, 1), 0) // L
    col_chunk = lax.broadcasted_iota(jnp.int32, (1, tm), 1) // L
    lane_half = lax.broadcasted_iota(jnp.int32, (tm, LANES), 1) // B_HEAD_DIM
    row_half = lax.broadcasted_iota(jnp.int32, (LANES, 1), 0) // B_HEAD_DIM
    rr = lax.broadcasted_iota(jnp.int32, (LANES, 2 * LANES), 0) // B_HEAD_DIM
    cc = lax.broadcasted_iota(jnp.int32, (LANES, 2 * LANES), 1)
    blockmask = jnp.where(cc < LANES, cc // B_HEAD_DIM, cc - LANES) == rr
    aug = jnp.where(lax.broadcasted_iota(jnp.int32, (tm, LANES), 1) < 2, 1.0, 0.0).astype(F32)
    io = ((qf, kf, ktf, vf, gf, of_ref), (qb, kb, ktb, vb, gb, ob_ref))

    def by_chunk(vals, chunk_ids):
        out = vals[chunks - 1]
        for c in range(chunks - 2, -1, -1):
            out = jnp.where(chunk_ids == c, vals[c], out)
        return out

    dp = [(d, p) for d in range(2) for p in range(2)]
    dph = [(d, p, hh) for d, p in dp for hh in range(2)]

    g_tile, b_tile, g_t, b_t = {}, {}, {}, {}
    for d in range(2):
        g_tile[d] = io[d][4][...]
        b_tile[d] = jnp.dot(masks[d].astype(F32), jax.nn.log_sigmoid(g_tile[d]), precision=HI,
                            preferred_element_type=F32)
        g_t[d] = g_tile[d].T
        b_t[d] = b_tile[d].T
    qp, kp, v_aug, ktp = {}, {}, {}, {}
    for d, p in dp:
        sl = slice(p * LANES, (p + 1) * LANES)
        qp[d, p] = io[d][0][:, sl]
        kp[d, p] = io[d][1][:, sl].astype(BF16)
        ktp[d, p] = io[d][2][sl, :]
        v_aug[d, p] = jnp.concatenate([io[d][3][:, sl], aug], axis=1).astype(BF16)

    b_col, b_row, i_row, gs, a_row, m_loc, w_t = {}, {}, {}, {}, {}, {}, {}
    for d, p, hh in dph:
        h = 2 * p + hh
        il, fl = 4 * d + h, 8 + 4 * d + h
        last = L - 1 if d == 0 else 0
        b_col[d, p, hh] = b_tile[d][:, fl:fl + 1]
        b_row[d, p, hh] = b_t[d][fl:fl + 1, :]
        i_row[d, p, hh] = g_t[d][il:il + 1, :]
        gs[d, p, hh] = [b_row[d, p, hh][:, c * L + last:c * L + last + 1] for c in range(chunks)]
        a_row[d, p, hh] = by_chunk(gs[d, p, hh], col_chunk) - b_row[d, p, hh] + i_row[d, p, hh]
    for k3 in dph:
        m_loc[k3] = [jnp.max(jnp.where(col_chunk == c, a_row[k3], neg), axis=-1, keepdims=True)
                     for c in range(chunks)]
    for k3 in dph:
        w_t[k3] = jnp.exp(a_row[k3] - by_chunk(m_loc[k3], col_chunk))

    upd, sqk = {}, {}
    for d, p in dp:
        kw_t = ktp[d, p] * jnp.where(row_half == 0, w_t[d, p, 0], w_t[d, p, 1])
        for c in range(chunks):
            kw_c = jnp.where(col_chunk == c, kw_t, 0.0).astype(BF16)
            upd[d, p, c] = jnp.where(blockmask, jnp.dot(kw_c, v_aug[d, p], preferred_element_type=F32), 0.0)
    for d, p, hh in dph:
        qm = jnp.where(lane_half == hh, qp[d, p], 0.0).astype(BF16)
        sqk[d, p, hh] = lax.dot_general(qm, kp[d, p], (((1,), (1,)), ((), ())), preferred_element_type=F32)

    m_old = m_ref[...]
    orders = (range(chunks), range(chunks - 1, -1, -1))
    m_at, dec, inj, m_end = {}, {}, {}, {}
    for d, p, hh in dph:
        row = 4 * d + 2 * p + hh
        m_prev = m_old[row:row + 1, 0:1]
        for c in orders[d]:
            m_at[d, p, hh, c] = m_prev
            g_c = gs[d, p, hh][c]
            m_new = jnp.maximum(g_c + m_prev, m_loc[d, p, hh][c])
            dec[d, p, hh, c] = jnp.exp(g_c + m_prev - m_new)
            inj[d, p, hh, c] = jnp.exp(m_loc[d, p, hh][c] - m_new)
            m_prev = m_new
        m_end[row] = m_prev
    state_at, state_end = {}, {}
    for d, p in dp:
        state = s_ref[d, p]
        for c in orders[d]:
            state_at[d, p, c] = state
            state = (jnp.where(rr == 0, dec[d, p, 0, c], dec[d, p, 1, c]) * state
                     + jnp.where(rr == 0, inj[d, p, 0, c], inj[d, p, 1, c]) * upd[d, p, c])
        state_end[d, p] = state
    for d, p in dp:
        s_ref[d, p] = state_end[d, p]
    m_rows = lax.broadcasted_iota(jnp.int32, m_old.shape, 0)
    m_out = m_old
    for row, val in m_end.items():
        m_out = jnp.where(m_rows == row, val, m_out)
    m_ref[...] = m_out

    dmat = {k3: jnp.where(masks[k3[0]], b_col[k3] + (i_row[k3] - b_row[k3]), neg) for k3 in dph}
    m_row = {k3: jnp.max(dmat[k3], axis=-1, keepdims=True) for k3 in dph}
    s_loc = {k3: (sqk[k3] * jnp.exp(dmat[k3] - m_row[k3])).astype(BF16) for k3 in dph}

    r = {}
    for d, p, hh in dph:
        r[d, p, hh] = jnp.dot(s_loc[d, p, hh], v_aug[d, p], preferred_element_type=F32)
    q2 = {}
    for d, p in dp:
        q2[d, p] = jnp.concatenate(
            [jnp.dot(qp[d, p][c * L:(c + 1) * L].astype(BF16), state_at[d, p, c].astype(BF16),
                     preferred_element_type=F32) for c in range(chunks)], axis=0)

    inter = {k3: b_col[k3] + by_chunk([m_at[k3 + (c,)] for c in range(chunks)], row_chunk) for k3 in dph}
    m_col = {k3: jnp.maximum(inter[k3], m_row[k3]) for k3 in dph}
    e_loc = {k3: jnp.exp(m_row[k3] - m_col[k3]) for k3 in dph}
    e_int = {k3: jnp.exp(inter[k3] - m_col[k3]) for k3 in dph}
    floor = {k3: jnp.exp(-m_col[k3]) for k3 in dph}
    den = {(d, p, hh): e_loc[d, p, hh] * r[d, p, hh][:, LANES:LANES + 1]
           + e_int[d, p, hh] * q2[d, p][:, LANES + hh:LANES + hh + 1] for d, p, hh in dph}
    inv = {k3: 1.0 / jnp.maximum(jnp.abs(den[k3]), floor[k3]) for k3 in dph}
    a_loc = {k3: e_loc[k3] * inv[k3] for k3 in dph}
    a_int = {k3: e_int[k3] * inv[k3] for k3 in dph}
    head_out = {(d, p, hh): a_loc[d, p, hh] * r[d, p, hh][:, :LANES] + a_int[d, p, hh] * q2[d, p][:, :LANES]
                for d, p, hh in dph}
    for d in range(2):
        io[d][5][...] = jnp.concatenate(
            [jnp.where(lane_half == 0, head_out[d, p, 0], head_out[d, p, 1]) for p in range(2)], axis=1)


def _mlstm(mq, mk, mkt, mv, g, n_ctx):
    t = mq.shape[0]
    tm = TOK_TILE
    n_tiles = t // tm
    n_ctx_tiles = n_ctx // tm
    mirror = lambda i: jnp.where(i < n_ctx_tiles, n_ctx_tiles - 1 - i, n_tiles - 1 - (i - n_ctx_tiles))
    fwd = lambda i: (i, 0)
    bwd = lambda i: (mirror(i), 0)
    spec = lambda w, im: pl.BlockSpec((tm, w), im)
    tspec = lambda f: pl.BlockSpec((None, B_WIDTH, tm), lambda i: (f(i), 0, 0))
    ins = ([spec(B_WIDTH, fwd)] * 2 + [tspec(lambda i: i), spec(B_WIDTH, fwd), spec(LANES, fwd)]
           + [spec(B_WIDTH, bwd)] * 2 + [tspec(mirror), spec(B_WIDTH, bwd), spec(LANES, bwd)])
    return pl.pallas_call(
        functools.partial(_mlstm_kernel, chunks=tm // MLSTM_CHUNK),
        out_shape=[jax.ShapeDtypeStruct((t, B_WIDTH), F32)] * 2,
        grid=(n_tiles,),
        in_specs=ins,
        out_specs=[spec(B_WIDTH, fwd), spec(B_WIDTH, bwd)],
        scratch_shapes=[pltpu.VMEM((2, 2, LANES, 2 * LANES), F32), pltpu.VMEM((8, LANES), F32)],
        compiler_params=_cparams(("arbitrary",)),
        name="mlstm_scan",
    )(mq, mk, mkt, mv, g, mq, mk, mkt, mv, g)


def _dft_cs(n):
    j = np.arange(n, dtype=np.int64)
    ang = 2.0 * np.pi * ((j[:, None] * j[None, :]) % n).astype(np.float64) / n
    s = 1.0 / math.sqrt(n)
    return np.cos(ang) * s, np.sin(ang) * s


def _chan_mats():
    c, s = _dft_cs(C_GROUP_DIM)
    eye = np.eye(C_GROUPS)
    return np.concatenate([np.kron(eye, c), np.kron(eye, s)], axis=0).astype(np.float32)


def _fft1_kernel(fa_ref, u_ref, z_ref):
    z_ref[...] = jnp.dot(fa_ref[...], u_ref[...], precision=HI, preferred_element_type=F32)


def _fft2_kernel(z_ref, twc_ref, tws_ref, fb_ref, ch_ref, o_ref, *, kb):
    n2 = fb_ref.shape[0] // 2
    for j in range(kb):
        zr = z_ref[0, j]
        zi = z_ref[1, j]
        tc = twc_ref[j]
        ts = tws_ref[j]
        st = jnp.concatenate([zr * tc + zi * ts, zi * tc - zr * ts], axis=0)
        a = jnp.dot(fb_ref[...], st, precision=HI, preferred_element_type=F32)
        ari = jnp.concatenate([a[:n2], a[n2:]], axis=1)
        o_ref[:, j * C_WIDTH:(j + 1) * C_WIDTH] = jnp.dot(ari, ch_ref[...], precision=HI,
                                                         preferred_element_type=F32)


def _fourier_lat(u):
    n = u.shape[0]
    n1 = int(round(math.sqrt(n)))
    assert n1 * n1 == n
    n2 = n1
    c, s = _dft_cs(n1)
    fa = jnp.asarray(np.concatenate([c, -s], axis=0).astype(np.float32))
    fb = jnp.asarray(np.block([[c, s], [-s, c]]).astype(np.float32))
    k1 = np.arange(n1, dtype=np.int64)
    ang = 2.0 * np.pi * ((k1[:, None] * k1[None, :]) % n).astype(np.float64) / n
    twc = jnp.asarray(np.cos(ang).astype(np.float32)).reshape(n1, n2, 1)
    tws = jnp.asarray(np.sin(ang).astype(np.float32)).reshape(n1, n2, 1)
    ch = jnp.asarray(_chan_mats())
    cols = n2 * C_WIDTH
    tb = min(2048, cols)
    z = pl.pallas_call(
        _fft1_kernel,
        out_shape=jax.ShapeDtypeStruct((2 * n1, cols), F32),
        grid=(cols // tb,),
        in_specs=[pl.BlockSpec((2 * n1, n1), lambda i: (0, 0)), pl.BlockSpec((n1, tb), lambda i: (0, i))],
        out_specs=pl.BlockSpec((2 * n1, tb), lambda i: (0, i)),
        compiler_params=_cparams(("arbitrary",)),
        name="fourier_stage1",
    )(fa, u.reshape(n1, cols))
    kb = 8
    out = pl.pallas_call(
        functools.partial(_fft2_kernel, kb=kb),
        out_shape=jax.ShapeDtypeStruct((n2, n1 * C_WIDTH), F32),
        grid=(n1 // kb,),
        in_specs=[pl.BlockSpec((2, kb, n2, C_WIDTH), lambda i: (0, i, 0, 0)),
                  pl.BlockSpec((kb, n2, 1), lambda i: (i, 0, 0)),
                  pl.BlockSpec((kb, n2, 1), lambda i: (i, 0, 0)),
                  pl.BlockSpec((2 * n2, 2 * n2), lambda i: (0, 0)),
                  pl.BlockSpec((2 * C_WIDTH, C_WIDTH), lambda i: (0, 0))],
        out_specs=pl.BlockSpec((n2, kb * C_WIDTH), lambda i: (0, i)),
        compiler_params=_cparams(("arbitrary",)),
        name="fourier_stage2",
    )(z.reshape(2, n1, n2, C_WIDTH), twc, tws, fb, ch)
    return out.reshape(n, C_WIDTH)


def _fft_small_kernel(fa_ref, u_ref, ch_ref, o_ref):
    n = u_ref.shape[0]
    a = jnp.dot(fa_ref[...], u_ref[...], precision=HI, preferred_element_type=F32)
    ari = jnp.concatenate([a[:n], a[n:]], axis=1)
    o_ref[...] = jnp.dot(ari, ch_ref[...], precision=HI, preferred_element_type=F32)


def _fourier_small(u):
    n = u.shape[0]
    c, s = _dft_cs(n)
    fa = jnp.asarray(np.concatenate([c, -s], axis=0).astype(np.float32))
    return pl.pallas_call(
        _fft_small_kernel,
        out_shape=jax.ShapeDtypeStruct((n, C_WIDTH), F32),
        compiler_params=_cparams(None),
        name="fourier_small",
    )(fa, u, jnp.asarray(_chan_mats()))


def _layer_norm(z, w, b):
    mu = jnp.mean(z, axis=-1, keepdims=True)
    zc = z - mu
    var = jnp.mean(zc * zc, axis=-1, keepdims=True)
    return zc * lax.rsqrt(var + EPS) * w + b


def _outproj_kernel(al_ref, ac_ref, fl_ref, fc_ref, hf_ref, hb_ref, bo_ref, x_ref, mod_ref, mnw_ref, seg_ref,
                    w_ref, lw_ref, lb_ref, o_ref, *, n_ctx_tiles, alpha):
    is_ctx = pl.program_id(0) < n_ctx_tiles
    attn = jnp.where(is_ctx, ac_ref[...], al_ref[...])
    four = jnp.where(is_ctx, fc_ref[...], fl_ref[...])
    h = hf_ref[...] + hb_ref[...]
    ms = jnp.dot(h * h, seg_ref[...], precision=HI, preferred_element_type=F32)
    y = h * lax.rsqrt(ms + EPS) * mnw_ref[...] * jax.nn.sigmoid(bo_ref[...])
    mix = (jnp.dot(attn.astype(BF16), w_ref[0:A_WIDTH, :], preferred_element_type=F32)
           + jnp.dot(y.astype(BF16), w_ref[A_WIDTH:A_WIDTH + B_WIDTH, :], preferred_element_type=F32)
           + jnp.dot(four.astype(BF16), w_ref[A_WIDTH + B_WIDTH:, :], preferred_element_type=F32))
    g1 = _mod_rows(mod_ref, 2, is_ctx)
    o_ref[...] = _layer_norm(alpha * x_ref[...] + g1 * mix, lw_ref[...], lb_ref[...])


def _outproj(attn_l, attn_c, four_l, four_c, hf, hb, bo, xt, mod, mnw, w_out_bf, ln_w, ln_b, n_ctx, with_ctx, alpha):
    tm = TOK_TILE
    nct = n_ctx // tm
    n_lat_tiles = attn_l.shape[0] // tm
    if with_ctx:
        n_tiles, uoff, n_ctx_tiles = n_lat_tiles + nct, 0, nct
        lat = lambda i: (jnp.maximum(i - nct, 0), 0)
    else:
        n_tiles, uoff, n_ctx_tiles = n_lat_tiles, nct, 0
        lat = lambda i: (i, 0)
    uni = lambda i: (i + uoff, 0)
    ctxm = lambda i: (jnp.minimum(i, nct - 1), 0)
    seg = jnp.asarray(np.kron(np.eye(B_HEADS), np.full((B_HEAD_DIM, B_HEAD_DIM), 1.0 / B_HEAD_DIM)).astype(np.float32))
    full = lambda a: pl.BlockSpec(a.shape, lambda i: (0,) * a.ndim)
    mnw2, lw2, lb2 = mnw.reshape(1, -1), ln_w.reshape(1, -1), ln_b.reshape(1, -1)
    return pl.pallas_call(
        functools.partial(_outproj_kernel, n_ctx_tiles=n_ctx_tiles, alpha=alpha),
        out_shape=jax.ShapeDtypeStruct((n_tiles * tm, D_MODEL), F32),
        grid=(n_tiles,),
        in_specs=[pl.BlockSpec((tm, A_WIDTH), lat), pl.BlockSpec((tm, A_WIDTH), ctxm),
                  pl.BlockSpec((tm, C_WIDTH), lat), pl.BlockSpec((tm, C_WIDTH), ctxm),
                  pl.BlockSpec((tm, B_WIDTH), uni), pl.BlockSpec((tm, B_WIDTH), uni),
                  pl.BlockSpec((tm, B_WIDTH), uni), pl.BlockSpec((tm, D_MODEL), uni),
                  full(mod), full(mnw2), full(seg), full(w_out_bf), full(lw2), full(lb2)],
        out_specs=pl.BlockSpec((tm, D_MODEL), lambda i: (i, 0)),
        compiler_params=_cparams(("arbitrary",)),
        name="out_projection",
    )(attn_l, attn_c, four_l, four_c, hf, hb, bo, xt, mod, mnw2, seg, w_out_bf, lw2, lb2)


def _router_kernel(x_ref, mod_ref, wrt_ref, br_ref, idx_ref, gate_ref, rank_ref, cnt_ref, run_ref, *, n_ctx_tiles):
    i = pl.program_id(0)

    @pl.when(i == 0)
    def _():
        run_ref[...] = jnp.zeros_like(run_ref)

    is_ctx = i < n_ctx_tiles
    f_in = x_ref[...] * (1.0 + _mod_rows(mod_ref, 4, is_ctx)) + _mod_rows(mod_ref, 3, is_ctx)
    logits = lax.dot_general(wrt_ref[...], f_in, (((1,), (1,)), ((), ())), precision=HI,
                             preferred_element_type=F32) + br_ref[...]
    tm = logits.shape[1]
    eidx = lax.broadcasted_iota(jnp.int32, logits.shape, 0).astype(F32)
    work = logits
    vals, sels = [], []
    for _ in range(TOP_K):
        mx = jnp.max(work, axis=0, keepdims=True)
        sel = jnp.min(jnp.where(work == mx, eidx, float(N_EXPERTS)), axis=0, keepdims=True)
        vals.append(mx)
        sels.append(sel)
        work = jnp.where(eidx == sel, -jnp.inf, work)
    es = [jnp.exp(v - vals[0]) for v in vals]
    tot = es[0] + es[1] + es[2] + es[3]
    onehots = [(eidx == s).astype(F32) for s in sels]
    oh_all = onehots[0] + onehots[1] + onehots[2] + onehots[3]
    r = lax.broadcasted_iota(jnp.int32, (tm, tm), 0)
    c = lax.broadcasted_iota(jnp.int32, (tm, tm), 1)
    before = jnp.dot(oh_all.astype(BF16), (r < c).astype(BF16), preferred_element_type=F32) + run_ref[:, 0:1]
    krow = lax.broadcasted_iota(jnp.int32, (8, tm), 0)
    idx_o = jnp.zeros((8, tm), jnp.int32)
    gate_o = jnp.zeros((8, tm), F32)
    rank_o = jnp.zeros((8, tm), jnp.int32)
    for k in range(TOP_K):
        rank = jnp.sum(onehots[k] * before, axis=0, keepdims=True).astype(jnp.int32)
        idx_o = jnp.where(krow == k, sels[k].astype(jnp.int32), idx_o)
        gate_o = jnp.where(krow == k, es[k] / tot, gate_o)
        rank_o = jnp.where(krow == k, rank, rank_o)
    idx_ref[...] = idx_o
    gate_ref[...] = gate_o
    rank_ref[...] = rank_o
    run_new = run_ref[...] + jnp.sum(oh_all, axis=1, keepdims=True)
    run_ref[...] = run_new
    cnt_ref[...] = run_new.astype(jnp.int32)


def _router(xs, mod, w_router, b_router, n_ctx_tiles):
    t = xs.shape[0]
    tm = TOK_TILE
    wrt = w_router.T
    br = b_router.reshape(-1, 1)
    full = lambda a: pl.BlockSpec(a.shape, lambda i: (0,) * a.ndim)
    col = pl.BlockSpec((8, tm), lambda i: (0, i))
    return pl.pallas_call(
        functools.partial(_router_kernel, n_ctx_tiles=n_ctx_tiles),
        out_shape=[jax.ShapeDtypeStruct((8, t), jnp.int32), jax.ShapeDtypeStruct((8, t), F32),
                   jax.ShapeDtypeStruct((8, t), jnp.int32), jax.ShapeDtypeStruct((N_EXPERTS, LANES), jnp.int32)],
        grid=(t // tm,),
        in_specs=[pl.BlockSpec((tm, D_MODEL), lambda i: (i, 0)), full(mod), full(wrt), full(br)],
        out_specs=[col, col, col, pl.BlockSpec((N_EXPERTS, LANES), lambda i: (0, 0))],
        scratch_shapes=[pltpu.VMEM((N_EXPERTS, LANES), F32)],
        compiler_params=_cparams(("arbitrary",)),
        name="moe_router",
    )(xs, mod, wrt, br)


ROW_CHUNKS = D_MODEL // LANES
DMA_UNROLL = 8


def _to_chunk_rows(ref, val):
    m = val.shape[0]
    for c in range(ROW_CHUNKS):
        ref[pl.ds(c, m, stride=ROW_CHUNKS), :] = val[:, c * LANES:(c + 1) * LANES]


def _from_chunk_rows(ref, m):
    return jnp.concatenate([ref[pl.ds(c, m, stride=ROW_CHUNKS), :] for c in range(ROW_CHUNKS)], axis=1)


def _dispatch_kernel(pstart_ref, pend_ref, dest_ref, x_ref, mod_ref, xb_ref, fbuf, zbuf, sem, zsem, *, n_ctx_tiles, bm):
    i = pl.program_id(0)
    tm = x_ref.shape[0]

    @pl.when(i == 0)
    def _():
        zbuf[...] = jnp.zeros_like(zbuf)

        def zcopy(e):
            row0 = pl.multiple_of((pend_ref[e] - bm) * ROW_CHUNKS, bm * ROW_CHUNKS)
            return pltpu.make_async_copy(zbuf, xb_ref.at[pl.ds(row0, bm * ROW_CHUNKS)], zsem)

        def zstart(e, c):
            @pl.when(pend_ref[e] > pstart_ref[e])
            def _():
                zcopy(e).start()
            return c

        def zwait(e, c):
            @pl.when(pend_ref[e] > pstart_ref[e])
            def _():
                zcopy(e).wait()
            return c

        lax.fori_loop(0, N_EXPERTS, zstart, 0)
        lax.fori_loop(0, N_EXPERTS, zwait, 0)

        def tcopy(b):
            row0 = pl.multiple_of(b * (bm * ROW_CHUNKS), bm * ROW_CHUNKS)
            return pltpu.make_async_copy(zbuf, xb_ref.at[pl.ds(row0, bm * ROW_CHUNKS)], zsem)

        def tstart(b, c):
            tcopy(b).start()
            return c

        def twait(b, c):
            tcopy(b).wait()
            return c

        tail0 = pend_ref[N_EXPERTS - 1] // bm
        n_blocks = xb_ref.shape[0] // (bm * ROW_CHUNKS)
        lax.fori_loop(tail0, n_blocks, tstart, 0)
        lax.fori_loop(tail0, n_blocks, twait, 0)

    is_ctx = i < n_ctx_tiles
    f = x_ref[...] * (1.0 + _mod_rows(mod_ref, 4, is_ctx)) + _mod_rows(mod_ref, 3, is_ctx)
    slot = i % 2
    _to_chunk_rows(fbuf.at[slot], f)

    def start(t, c):
        src = pl.multiple_of(t * ROW_CHUNKS, ROW_CHUNKS)
        for k in range(TOP_K):
            dst = pl.multiple_of(dest_ref[t * TOP_K + k] * ROW_CHUNKS, ROW_CHUNKS)
            pltpu.make_async_copy(fbuf.at[slot, pl.ds(src, ROW_CHUNKS)], xb_ref.at[pl.ds(dst, ROW_CHUNKS)],
                                  sem.at[slot]).start(priority=k % 2)
        return c

    lax.fori_loop(0, tm, start, 0, unroll=DMA_UNROLL)

    def wait_tile(s):
        for _ in range(TOP_K):
            pltpu.make_async_copy(fbuf.at[s], xb_ref.at[pl.ds(0, tm * ROW_CHUNKS)], sem.at[s]).wait()

    @pl.when(i > 0)
    def _():
        wait_tile(1 - slot)

    @pl.when(i == pl.num_programs(0) - 1)
    def _():
        wait_tile(slot)


def _dispatch(pad_start, pad_end, dest_flat, xs, mod, n_rows, n_ctx_tiles):
    t = xs.shape[0]
    tm = TOK_TILE
    gs = pltpu.PrefetchScalarGridSpec(
        num_scalar_prefetch=2,
        grid=(t // tm,),
        in_specs=[pl.BlockSpec((tm * TOP_K,), lambda i, *_: (i,), memory_space=pltpu.SMEM),
                  pl.BlockSpec((tm, D_MODEL), lambda i, *_: (i, 0)),
                  pl.BlockSpec(mod.shape, lambda i, *_: (0, 0))],
        out_specs=pl.BlockSpec(memory_space=pl.ANY),
        scratch_shapes=[pltpu.VMEM((2, tm * ROW_CHUNKS, LANES), F32), pltpu.VMEM((MOE_BM * ROW_CHUNKS, LANES), F32),
                        pltpu.SemaphoreType.DMA((2,)), pltpu.SemaphoreType.DMA],
    )
    return pl.pallas_call(
        functools.partial(_dispatch_kernel, n_ctx_tiles=n_ctx_tiles, bm=MOE_BM),
        out_shape=jax.ShapeDtypeStruct((n_rows * ROW_CHUNKS, LANES), F32),
        grid_spec=gs,
        compiler_params=_cparams(("arbitrary",)),
        name="moe_dispatch",
    )(pad_start, pad_end, dest_flat, xs, mod)


def _expert_kernel(bexp_ref, bvalid_ref, x_ref, wgu_ref, bgu_ref, wd_ref, bd_ref, y_ref, wgu_bf, wd_bf):
    i = pl.program_id(0)
    e = bexp_ref[i]
    prev = bexp_ref[jnp.maximum(i - 1, 0)]

    @pl.when(jnp.logical_or(i == 0, e != prev))
    def _():
        wgu_bf[...] = wgu_ref[...].astype(BF16)
        wd_bf[...] = wd_ref[...].astype(BF16)

    @pl.when(bvalid_ref[i] > 0)
    def _():
        bm = y_ref.shape[0] // ROW_CHUNKS
        x = _from_chunk_rows(x_ref, bm).astype(BF16)
        gu = jnp.dot(x, wgu_bf[...], preferred_element_type=F32) + bgu_ref[...]
        gate = jnp.minimum(gu[:, :D_EXPERT], SWIGLU_LIMIT)
        up = jnp.clip(gu[:, D_EXPERT:], -SWIGLU_LIMIT, SWIGLU_LIMIT)
        act = gate * jax.nn.sigmoid(SWIGLU_ALPHA * gate) * (up + 1.0)
        _to_chunk_rows(y_ref, jnp.dot(act.astype(BF16), wd_bf[...], preferred_element_type=F32) + bd_ref[...])

    @pl.when(bvalid_ref[i] == 0)
    def _():
        y_ref[...] = jnp.zeros_like(y_ref)


def _experts(block_exp, block_valid, xb, layer, w_gate_up, b_gate_up, w_down, b_down):
    n_rows = xb.shape[0] // ROW_CHUNKS
    bm = MOE_BM
    depth = w_gate_up.shape[0]
    bgu = b_gate_up.reshape(depth, N_EXPERTS, 1, 2 * D_EXPERT)
    bd = b_down.reshape(depth, N_EXPERTS, 1, D_MODEL)
    gs = pltpu.PrefetchScalarGridSpec(
        num_scalar_prefetch=2,
        grid=(n_rows // bm,),
        in_specs=[pl.BlockSpec((bm * ROW_CHUNKS, LANES), lambda i, be, bv: (i, 0)),
                  pl.BlockSpec((None, None, D_MODEL, 2 * D_EXPERT), lambda i, be, bv: (layer, be[i], 0, 0)),
                  pl.BlockSpec((None, None, 1, 2 * D_EXPERT), lambda i, be, bv: (layer, be[i], 0, 0)),
                  pl.BlockSpec((None, None, D_EXPERT, D_MODEL), lambda i, be, bv: (layer, be[i], 0, 0)),
                  pl.BlockSpec((None, None, 1, D_MODEL), lambda i, be, bv: (layer, be[i], 0, 0))],
        out_specs=pl.BlockSpec((bm * ROW_CHUNKS, LANES), lambda i, be, bv: (i, 0)),
        scratch_shapes=[pltpu.VMEM((D_MODEL, 2 * D_EXPERT), BF16), pltpu.VMEM((D_EXPERT, D_MODEL), BF16)],
    )
    return pl.pallas_call(
        _expert_kernel,
        out_shape=jax.ShapeDtypeStruct((n_rows * ROW_CHUNKS, LANES), F32),
        grid_spec=gs,
        compiler_params=_cparams(("arbitrary",)),
        name="moe_experts",
    )(block_exp, block_valid, xb, w_gate_up, bgu, w_down, bd)


def _combine_kernel(dest_ref, dnext_ref, gate_ref, x_ref, mod_ref, lw_ref, lb_ref, yb_ref, o_ref, ybuf, sem, *,
                    n_ctx_tiles, alpha):
    i = pl.program_id(0)
    tm = x_ref.shape[0]
    slot = i % 2

    def gather(d_ref, s):
        def start(t, c):
            dst = pl.multiple_of(t * ROW_CHUNKS, ROW_CHUNKS)
            for k in range(TOP_K):
                src = pl.multiple_of(d_ref[t * TOP_K + k] * ROW_CHUNKS, ROW_CHUNKS)
                pltpu.make_async_copy(yb_ref.at[pl.ds(src, ROW_CHUNKS)], ybuf.at[s, k, pl.ds(dst, ROW_CHUNKS)],
                                      sem.at[s]).start(priority=k % 2)
            return c

        lax.fori_loop(0, tm, start, 0, unroll=DMA_UNROLL)

    @pl.when(i == 0)
    def _():
        gather(dest_ref, 0)

    @pl.when(i + 1 < pl.num_programs(0))
    def _():
        gather(dnext_ref, 1 - slot)

    for k in range(TOP_K):
        pltpu.make_async_copy(yb_ref.at[pl.ds(0, tm * ROW_CHUNKS)], ybuf.at[slot, k], sem.at[slot]).wait()
    gates = gate_ref[...]
    y = gates[:, 0:1] * _from_chunk_rows(ybuf.at[slot, 0], tm)
    for k in range(1, TOP_K):
        y = y + gates[:, k:k + 1] * _from_chunk_rows(ybuf.at[slot, k], tm)
    is_ctx = i < n_ctx_tiles
    g2 = _mod_rows(mod_ref, 5, is_ctx)
    o_ref[...] = _layer_norm(alpha * x_ref[...] + g2 * y, lw_ref[...], lb_ref[...])


def _combine(dest_flat, gates, xs, mod, ln_w, ln_b, yb, n_ctx_tiles, alpha):
    t = xs.shape[0]
    tm = TOK_TILE
    n_tiles = t // tm
    lw2, lb2 = ln_w.reshape(1, -1), ln_b.reshape(1, -1)
    full = lambda a: pl.BlockSpec(a.shape, lambda i: (0,) * a.ndim)
    return pl.pallas_call(
        functools.partial(_combine_kernel, n_ctx_tiles=n_ctx_tiles, alpha=alpha),
        out_shape=jax.ShapeDtypeStruct((t, D_MODEL), F32),
        grid=(n_tiles,),
        in_specs=[pl.BlockSpec((tm * TOP_K,), lambda i: (i,), memory_space=pltpu.SMEM),
                  pl.BlockSpec((tm * TOP_K,), lambda i: (jnp.minimum(i + 1, n_tiles - 1),), memory_space=pltpu.SMEM),
                  pl.BlockSpec((tm, TOP_K), lambda i: (i, 0)),
                  pl.BlockSpec((tm, D_MODEL), lambda i: (i, 0)),
                  full(mod), full(lw2), full(lb2),
                  pl.BlockSpec(memory_space=pl.ANY)],
        out_specs=pl.BlockSpec((tm, D_MODEL), lambda i: (i, 0)),
        scratch_shapes=[pltpu.VMEM((2, TOP_K, tm * ROW_CHUNKS, LANES), F32), pltpu.SemaphoreType.DMA((2,))],
        compiler_params=_cparams(("arbitrary",)),
        name="moe_combine",
    )(dest_flat, dest_flat, gates, xs, mod, lw2, lb2, yb)


def _moe(xs, mod, layer, w_router, b_router, w_gate_up, b_gate_up, w_down, b_down, ln_w, ln_b, n_ctx_tiles, alpha):
    t = xs.shape[0]
    bm = MOE_BM
    idx, gates, rank, cnt = _router(xs, mod, w_router, b_router, n_ctx_tiles)
    counts = cnt[:, 0]
    idx, gates, rank = idx[:TOP_K].T, gates[:TOP_K].T, rank[:TOP_K].T
    padded = (counts + bm - 1) // bm * bm
    pad_end = jnp.cumsum(padded).astype(jnp.int32)
    pad_start = pad_end - padded
    dest = pad_start[idx] + rank
    dest_flat = dest.reshape(-1).astype(jnp.int32)
    n_blocks = -(-(t * TOP_K) // bm) + N_EXPERTS
    blk0 = jnp.arange(n_blocks, dtype=jnp.int32) * bm
    block_exp = jnp.minimum(jnp.sum(pad_end[None, :] <= blk0[:, None], axis=1), N_EXPERTS - 1).astype(jnp.int32)
    block_valid = (blk0 < pad_end[-1]).astype(jnp.int32)
    xb = _dispatch(pad_start, pad_end, dest_flat, xs, mod, n_blocks * bm, n_ctx_tiles)
    yb = _experts(block_exp, block_valid, xb, layer, w_gate_up, b_gate_up, w_down, b_down)
    return _combine(dest_flat, gates, xs, mod, ln_w, ln_b, yb, n_ctx_tiles, alpha)


def kernel(x, c, ctx, c_ctx, w_ada, b_ada, w_in, mlstm_gate_bias, diff_lambda, diff_norm_w, mlstm_norm_w, w_out,
           ln1_w, ln1_b, w_router, b_router, w_gate_up, b_gate_up, w_down, b_down, ln2_w, ln2_b):
    assert x.shape[0] == 1 and ctx.shape[0] == 1
    depth = w_ada.shape[0]
    n_lat, n_ctx = x.shape[1], ctx.shape[1]
    assert n_ctx % TOK_TILE == 0 and n_lat % TOK_TILE == 0
    nct = n_ctx // TOK_TILE
    alpha = (2 * depth) ** 0.25

    vecs = jnp.zeros((8, D_MODEL), F32).at[0].set(c[0]).at[1].set(c_ctx)
    mods = _ada(vecs, w_ada, b_ada)
    cos, sin = _rope_tables(n_lat, n_ctx)
    xt = jnp.concatenate([ctx[0], x[0]], axis=0)

    for l in range(depth):
        need_ctx = l < depth - 1
        lam_init = 0.8 - 0.6 * math.exp(-0.3 * l)
        mod = mods[l]
        w_ext, w_vt = _prep_w_in(w_in[l])
        q, k, vt, mkt, mq, mk, mv, mo, g, cu = _inproj(xt, mod, w_ext, w_vt, cos, sin, mlstm_gate_bias[l], n_ctx)

        dl = diff_lambda[l].astype(F32)
        lam = (jnp.exp(jnp.sum(dl[0] * dl[1])) - jnp.exp(jnp.sum(dl[2] * dl[3])) + lam_init).reshape(1)
        post = 1.0 - lam_init
        attn_l = _attention(lam, q, k, vt, diff_norm_w[l], n_ctx, n_lat, n_ctx + n_lat, post)
        four_l = _fourier_lat(cu[n_ctx:])
        if need_ctx:
            attn_c = _attention(lam, q, k, vt, diff_norm_w[l], 0, n_ctx, n_ctx, post)
            four_c = _fourier_small(cu[:n_ctx])
        else:
            attn_c, four_c = attn_l, four_l
        hf, hb = _mlstm(mq, mk, mkt, mv, g, n_ctx)
        xs = _outproj(attn_l, attn_c, four_l, four_c, hf, hb, mo, xt, mod, mlstm_norm_w[l], w_out[l].astype(BF16),
                      ln1_w[l], ln1_b[l], n_ctx, need_ctx, alpha)
        xt = _moe(xs, mod, l, w_router[l], b_router[l], w_gate_up, b_gate_up, w_down, b_down, ln2_w[l], ln2_b[l],
                  nct if need_ctx else 0, alpha)
    return xt[None]
```

```python
import functools
import math

import numpy as np
import jax
import jax.numpy as jnp
from jax import lax
from jax.experimental import pallas as pl
from jax.experimental.pallas import tpu as pltpu

F32 = jnp.float32
BF16 = jnp.bfloat16
HI = lax.Precision.HIGHEST

D_MODEL = 1024
GRID_W = 64
HEAD_DIM = 64
A_HEADS = 4
A_QK_DIM = HEAD_DIM
A_V_DIM = 2 * HEAD_DIM
A_WIDTH = A_HEADS * A_V_DIM
A_SCALE = A_QK_DIM ** -0.5
B_HEADS = 4
B_HEAD_DIM = HEAD_DIM
B_WIDTH = B_HEADS * B_HEAD_DIM
C_GROUPS = 4
C_GROUP_DIM = HEAD_DIM
C_WIDTH = C_GROUPS * C_GROUP_DIM
IN_SIZES = (A_WIDTH, A_WIDTH, A_WIDTH, B_WIDTH, B_WIDTH, B_WIDTH, B_WIDTH, 4 * B_HEADS, C_WIDTH)
ROPE_BASE = 10000.0
MLSTM_CHUNK = 64
N_EXPERTS = 32
TOP_K = 4
D_EXPERT = D_MODEL
SWIGLU_LIMIT = 7.0
SWIGLU_ALPHA = 1.702
EPS = 1e-5

LANES = 128
VMEM_LIMIT = 56 * 1024 * 1024
TOK_TILE = 256
ATT_TQ = 512
ATT_COLS = 256
ATT_UNROLL = 16
MOE_BM = 256

_OQ, _OK = 0, 512
_OMQ, _OMK, _OMV, _OMO, _OG, _OCU = 1024, 1280, 1536, 1792, 2048, 2176
Q_SCALE = A_SCALE * math.log2(math.e)
VT_ROWS = A_V_DIM + 16


def _cparams(sem, vmem=VMEM_LIMIT):
    return pltpu.CompilerParams(dimension_semantics=sem, vmem_limit_bytes=vmem)


def _ada_kernel(v_ref, w_ref, b_ref, o_ref):
    v = v_ref[...]
    s = v * jax.nn.sigmoid(v)
    o_ref[...] = jnp.dot(s, w_ref[...], precision=HI, preferred_element_type=F32) + b_ref[...]


def _ada(vecs, w_ada, b_ada):
    depth, d, d6 = w_ada.shape
    bn = 1024
    return pl.pallas_call(
        _ada_kernel,
        out_shape=jax.ShapeDtypeStruct((depth, 8, d6), F32),
        grid=(depth, d6 // bn),
        in_specs=[pl.BlockSpec((8, d), lambda l, j: (0, 0)),
                  pl.BlockSpec((None, d, bn), lambda l, j: (l, 0, j)),
                  pl.BlockSpec((None, 1, bn), lambda l, j: (l, 0, j))],
        out_specs=pl.BlockSpec((None, 8, bn), lambda l, j: (l, 0, j)),
        compiler_params=_cparams(("arbitrary", "arbitrary")),
        name="ada_modulation",
    )(vecs, w_ada, b_ada.reshape(depth, 1, d6))


def _mod_rows(mod_ref, k, is_ctx):
    lat = mod_ref[0:1, k * D_MODEL:(k + 1) * D_MODEL]
    ctx = mod_ref[1:2, k * D_MODEL:(k + 1) * D_MODEL]
    return jnp.where(is_ctx, ctx, lat)


def _inproj_kernel(x_ref, mod_ref, w_ref, wvt_ref, cos_ref, sin_ref, gb_ref,
                   q_ref, k_ref, vt_ref, mkt_ref, mq_ref, mk_ref, mv_ref, mo_ref, g_ref, cu_ref, *, n_ctx_tiles):
    is_ctx = pl.program_id(0) < n_ctx_tiles
    sh = _mod_rows(mod_ref, 0, is_ctx)
    sc = _mod_rows(mod_ref, 1, is_ctx)
    hm = (x_ref[...] * (1.0 + sc) + sh).astype(BF16)

    def proj(off, width):
        return jnp.dot(hm, w_ref[:, off:off + width], preferred_element_type=F32)

    cos = cos_ref[...]
    sin = sin_ref[...]
    quarter = A_QK_DIM // 4
    first_half = (lax.broadcasted_iota(jnp.int32, cos.shape, 1) % (2 * quarter)) < quarter

    def rope(t):
        rot = jnp.where(first_half, -pltpu.roll(t, LANES - quarter, 1), pltpu.roll(t, quarter, 1))
        return t * cos + rot * sin

    for j in range(A_HEADS):
        o = j * LANES
        q_ref[:, o:o + LANES] = (rope(proj(_OQ + o, LANES)) * Q_SCALE).astype(BF16)
        k_ref[:, o:o + LANES] = rope(proj(_OK + o, LANES)).astype(BF16)
    tr = lax.dot_general(wvt_ref[...], hm, (((1,), (1,)), ((), ())), preferred_element_type=F32)
    mkt_ref[...] = tr[A_WIDTH:, :] * (B_HEAD_DIM ** -0.5)
    vt = tr[:A_WIDTH, :].astype(BF16)
    for j in range(A_HEADS):
        vt_ref[j, 0:A_V_DIM, :] = vt[j * A_V_DIM:(j + 1) * A_V_DIM, :]
        vt_ref[j, A_V_DIM:, :] = jnp.ones((VT_ROWS - A_V_DIM, vt.shape[1]), BF16)
    mq_ref[...] = proj(_OMQ, B_WIDTH)
    mk_ref[...] = proj(_OMK, B_WIDTH) * (B_HEAD_DIM ** -0.5)
    mv_ref[...] = proj(_OMV, B_WIDTH)
    mo_ref[...] = proj(_OMO, B_WIDTH)
    g_ref[...] = proj(_OG, LANES) + gb_ref[...]
    cu_ref[...] = proj(_OCU, C_WIDTH)


def _prep_w_in(w_in):
    offs = np.cumsum((0,) + IN_SIZES)
    aq, ak, av, bq, bk, bv, bo, bg, cu = [w_in[:, offs[i]:offs[i + 1]] for i in range(9)]
    bg = jnp.pad(bg, ((0, 0), (0, LANES - bg.shape[1])))
    w = jnp.concatenate([aq, ak, bq, bk, bv, bo, bg, cu], axis=1)
    return w.astype(BF16), jnp.concatenate([av, bk], axis=1).T.astype(BF16)


def _rope_tables(n_lat, n_ctx):
    rows = n_lat // GRID_W
    axis_dim = A_QK_DIM // 2
    inv_freq = ROPE_BASE ** (-np.arange(0, axis_dim, 2, dtype=np.float64) / axis_dim)
    ang_r = np.repeat(np.arange(rows, dtype=np.float64), GRID_W)[:, None] * inv_freq
    ang_c = np.tile(np.arange(GRID_W, dtype=np.float64), rows)[:, None] * inv_freq
    ang = np.concatenate([ang_r, ang_r, ang_c, ang_c] * 2, axis=-1)
    cos = np.concatenate([np.ones((n_ctx, 2 * A_QK_DIM)), np.cos(ang)], axis=0)
    sin = np.concatenate([np.zeros((n_ctx, 2 * A_QK_DIM)), np.sin(ang)], axis=0)
    return jnp.asarray(cos.astype(np.float32)), jnp.asarray(sin.astype(np.float32))


def _inproj(xt, mod, w_ext, w_vt, cos, sin, gate_bias, n_ctx):
    t = xt.shape[0]
    tm = TOK_TILE
    gb = jnp.pad(gate_bias.reshape(1, -1), ((0, 0), (0, LANES - 4 * B_HEADS)))
    row = lambda w: pl.BlockSpec((tm, w), lambda i: (i, 0))
    full = lambda a: pl.BlockSpec(a.shape, lambda i: (0,) * a.ndim)
    sds = jax.ShapeDtypeStruct
    rows_out = [(A_WIDTH, BF16)] * 2 + [(B_WIDTH, F32)] * 4 + [(LANES, F32), (C_WIDTH, F32)]
    out_shape = [sds((t, w), dt) for w, dt in rows_out]
    out_specs = [row(w) for w, _ in rows_out]
    out_shape.insert(2, sds((t // tm, A_HEADS, VT_ROWS, tm), BF16))
    out_specs.insert(2, pl.BlockSpec((None, A_HEADS, VT_ROWS, tm), lambda i: (i, 0, 0, 0)))
    out_shape.insert(3, sds((t // tm, B_WIDTH, tm), F32))
    out_specs.insert(3, pl.BlockSpec((None, B_WIDTH, tm), lambda i: (i, 0, 0)))
    return pl.pallas_call(
        functools.partial(_inproj_kernel, n_ctx_tiles=n_ctx // tm),
        out_shape=out_shape,
        grid=(t // tm,),
        in_specs=[row(D_MODEL), full(mod), full(w_ext), full(w_vt), row(LANES), row(LANES), full(gb)],
        out_specs=out_specs,
        compiler_params=_cparams(("arbitrary",)),
        name="in_projection",
    )(xt, mod, w_ext, w_vt, cos, sin, gb)


def _attn_kernel(lam_ref, q_ref, k_ref, vt_ref, nw_ref, o_ref, acc_ref, s_ref, *, n_kv, post_scale):
    q = q_ref[...]
    tq = q.shape[0]
    tk = vt_ref.shape[-1]
    lane = lax.broadcasted_iota(jnp.int32, q.shape, 1)
    zero = jnp.zeros_like(q)
    q2 = jnp.concatenate([jnp.where(lane < A_QK_DIM, q, zero), jnp.where(lane >= A_QK_DIM, q, zero)], axis=0)
    acc_ref[...] = jnp.zeros_like(acc_ref)

    def scores(j, slot):
        kb = k_ref[pl.ds(pl.multiple_of(j * tk, tk), tk), :]
        st = lax.dot_general(kb, q2, (((1,), (1,)), ((), ())), preferred_element_type=F32)
        s_ref[slot] = st
        return jnp.max(st, axis=0, keepdims=True)

    def consume(j, slot, cmax, m):
        m_new = jnp.maximum(m, cmax)
        alpha = jnp.exp2(m - m_new)
        for h in range(2 * tq // ATT_COLS):
            cs = slice(h * ATT_COLS, (h + 1) * ATT_COLS)
            p = jnp.exp2(s_ref[slot, :, cs] - m_new[:, cs]).astype(BF16)
            r = jnp.dot(vt_ref[j], p, preferred_element_type=F32)
            acc_ref[:, cs] = alpha[:, cs] * acc_ref[:, cs] + r
        return m_new

    cm0 = scores(0, 0)

    def group(t, carry):
        cm, m = carry
        a = ATT_UNROLL * t
        for i in range(ATT_UNROLL):
            cm_next = scores(a + i + 1, (i + 1) % 2)
            m = consume(a + i, i % 2, cm, m)
            cm = cm_next
        return cm, m

    assert (n_kv - 1) % ATT_UNROLL == 0
    cm_last, m = lax.fori_loop(0, (n_kv - 1) // ATT_UNROLL, group, (cm0, jnp.full((1, 2 * tq), -jnp.inf, F32)))
    consume(n_kv - 1, 0, cm_last, m)
    l = acc_ref[A_V_DIM:A_V_DIM + 1, :]
    acc = acc_ref[0:A_V_DIM, :]
    ot = acc[:, :tq] * (1.0 / l[:, :tq]) - acc[:, tq:] * (lam_ref[0] / l[:, tq:])
    yt = ot * lax.rsqrt(jnp.mean(ot * ot, axis=0, keepdims=True) + EPS)
    o_ref[...] = yt.T * nw_ref[...] * post_scale


def _attention(lam, q, k, vt, norm_w, q_row0, n_q, n_keys, post_scale):
    tq = min(ATT_TQ, n_q)
    tk = vt.shape[-1]
    n_kv = n_keys // tk
    if q_row0 % tq:
        q, q_row0 = q[q_row0:q_row0 + n_q], 0
    q_blk0 = q_row0 // tq
    return pl.pallas_call(
        functools.partial(_attn_kernel, n_kv=n_kv, post_scale=post_scale),
        out_shape=jax.ShapeDtypeStruct((n_q, A_WIDTH), F32),
        grid=(A_HEADS, n_q // tq),
        in_specs=[pl.BlockSpec(memory_space=pltpu.SMEM),
                  pl.BlockSpec((tq, LANES), lambda h, i: (i + q_blk0, h)),
                  pl.BlockSpec((n_keys, LANES), lambda h, i: (0, h)),
                  pl.BlockSpec((n_kv, None, VT_ROWS, tk), lambda h, i: (0, h, 0, 0)),
                  pl.BlockSpec((1, LANES), lambda h, i: (0, h))],
        out_specs=pl.BlockSpec((tq, LANES), lambda h, i: (i, h)),
        scratch_shapes=[pltpu.VMEM((VT_ROWS, 2 * tq), F32), pltpu.VMEM((2, tk, 2 * tq), F32)],
        compiler_params=_cparams(("arbitrary", "arbitrary")),
        name="diff_attention",
    )(lam, q, k, vt, norm_w.reshape(1, A_WIDTH))


def _mlstm_kernel(qf, kf, ktf, vf, gf, qb, kb, ktb, vb, gb, of_ref, ob_ref, s_ref, m_ref, *, chunks):
    L = MLSTM_CHUNK
    tm = chunks * L
    neg = -jnp.inf

    @pl.when(pl.program_id(0) == 0)
    def _():
        s_ref[...] = jnp.zeros_like(s_ref)
        m_ref[...] = jnp.zeros_like(m_ref)

    tr = lax.broadcasted_iota(jnp.int32, (tm, tm), 0)
    tc = lax.broadcasted_iota(jnp.int32, (tm, tm), 1)
    same_chunk = tr // L == tc // L
    masks = (same_chunk & (tr >= tc), same_chunk & (tr <= tc))
    row_chunk = lax.broadcasted_iota(jnp.int32, (tm, 1), 0) // L
    col_chunk = lax.broadcasted_iota(jnp.int32, (1, tm), 1) // L
    lane_half = lax.broadcasted_iota(jnp.int32, (tm, LANES), 1) // B_HEAD_DIM
    row_half = lax.broadcasted_iota(jnp.int32, (LANES, 1), 0) // B_HEAD_DIM
    rr = lax.broadcasted_iota(jnp.int32, (LANES, 2 * LANES), 0) // B_HEAD_DIM
    cc = lax.broadcasted_iota(jnp.int32, (LANES, 2 * LANES), 1)
    blockmask = jnp.where(cc < LANES, cc // B_HEAD_DIM, cc - LANES) == rr
    aug = jnp.where(lax.broadcasted_iota(jnp.int32, (tm, LANES), 1) < 2, 1.0, 0.0).astype(F32)
    io = ((qf, kf, ktf, vf, gf, of_ref), (qb, kb, ktb, vb, gb, ob_ref))

    def by_chunk(vals, chunk_ids):
        out = vals[chunks - 1]
        for c in range(chunks - 2, -1, -1):
            out = jnp.where(chunk_ids == c, vals[c], out)
        return out

    dp = [(d, p) for d in range(2) for p in range(2)]
    dph = [(d, p, hh) for d, p in dp for hh in range(2)]

    g_tile, b_tile, g_t, b_t = {}, {}, {}, {}
    for d in range(2):
        g_tile[d] = io[d][4][...]
        b_tile[d] = jnp.dot(masks[d].astype(F32), jax.nn.log_sigmoid(g_tile[d]), precision=HI,
                            preferred_element_type=F32)
        g_t[d] = g_tile[d].T
        b_t[d] = b_tile[d].T
    qp, kp, v_aug, ktp = {}, {}, {}, {}
    for d, p in dp:
        sl = slice(p * LANES, (p + 1) * LANES)
        qp[d, p] = io[d][0][:, sl]
        kp[d, p] = io[d][1][:, sl].astype(BF16)
        ktp[d, p] = io[d][2][sl, :]
        v_aug[d, p] = jnp.concatenate([io[d][3][:, sl], aug], axis=1).astype(BF16)

    b_col, b_row, i_row, gs, a_row, m_loc, w_t = {}, {}, {}, {}, {}, {}, {}
    for d, p, hh in dph:
        h = 2 * p + hh
        il, fl = 4 * d + h, 8 + 4 * d + h
        last = L - 1 if d == 0 else 0
        b_col[d, p, hh] = b_tile[d][:, fl:fl + 1]
        b_row[d, p, hh] = b_t[d][fl:fl + 1, :]
        i_row[d, p, hh] = g_t[d][il:il + 1, :]
        gs[d, p, hh] = [b_row[d, p, hh][:, c * L + last:c * L + last + 1] for c in range(chunks)]
        a_row[d, p, hh] = by_chunk(gs[d, p, hh], col_chunk) - b_row[d, p, hh] + i_row[d, p, hh]
    for k3 in dph:
        m_loc[k3] = [jnp.max(jnp.where(col_chunk == c, a_row[k3], neg), axis=-1, keepdims=True)
                     for c in range(chunks)]
    for k3 in dph:
        w_t[k3] = jnp.exp(a_row[k3] - by_chunk(m_loc[k3], col_chunk))

    upd, sqk = {}, {}
    for d, p in dp:
        kw_t = ktp[d, p] * jnp.where(row_half == 0, w_t[d, p, 0], w_t[d, p, 1])
        for c in range(chunks):
            kw_c = jnp.where(col_chunk == c, kw_t, 0.0).astype(BF16)
            upd[d, p, c] = jnp.where(blockmask, jnp.dot(kw_c, v_aug[d, p], preferred_element_type=F32), 0.0)
    for d, p, hh in dph:
        qm = jnp.where(lane_half == hh, qp[d, p], 0.0).astype(BF16)
        sqk[d, p, hh] = lax.dot_general(qm, kp[d, p], (((1,), (1,)), ((), ())), preferred_element_type=F32)

    m_old = m_ref[...]
    orders = (range(chunks), range(chunks - 1, -1, -1))
    m_at, dec, inj, m_end = {}, {}, {}, {}
    for d, p, hh in dph:
        row = 4 * d + 2 * p + hh
        m_prev = m_old[row:row + 1, 0:1]
        for c in orders[d]:
            m_at[d, p, hh, c] = m_prev
            g_c = gs[d, p, hh][c]
            m_new = jnp.maximum(g_c + m_prev, m_loc[d, p, hh][c])
            dec[d, p, hh, c] = jnp.exp(g_c + m_prev - m_new)
            inj[d, p, hh, c] = jnp.exp(m_loc[d, p, hh][c] - m_new)
            m_prev = m_new
        m_end[row] = m_prev
    state_at, state_end = {}, {}
    for d, p in dp:
        state = s_ref[d, p]
        for c in orders[d]:
            state_at[d, p, c] = state
            state = (jnp.where(rr == 0, dec[d, p, 0, c], dec[d, p, 1, c]) * state
                     + jnp.where(rr == 0, inj[d, p, 0, c], inj[d, p, 1, c]) * upd[d, p, c])
        state_end[d, p] = state
    for d, p in dp:
        s_ref[d, p] = state_end[d, p]
    m_rows = lax.broadcasted_iota(jnp.int32, m_old.shape, 0)
    m_out = m_old
    for row, val in m_end.items():
        m_out = jnp.where(m_rows == row, val, m_out)
    m_ref[...] = m_out

    dmat = {k3: jnp.where(masks[k3[0]], b_col[k3] + (i_row[k3] - b_row[k3]), neg) for k3 in dph}
    m_row = {k3: jnp.max(dmat[k3], axis=-1, keepdims=True) for k3 in dph}
    s_loc = {k3: (sqk[k3] * jnp.exp(dmat[k3] - m_row[k3])).astype(BF16) for k3 in dph}

    r = {}
    for d, p, hh in dph:
        r[d, p, hh] = jnp.dot(s_loc[d, p, hh], v_aug[d, p], preferred_element_type=F32)
    q2 = {}
    for d, p in dp:
        q2[d, p] = jnp.concatenate(
            [jnp.dot(qp[d, p][c * L:(c + 1) * L].astype(BF16), state_at[d, p, c].astype(BF16),
                     preferred_element_type=F32) for c in range(chunks)], axis=0)

    inter = {k3: b_col[k3] + by_chunk([m_at[k3 + (c,)] for c in range(chunks)], row_chunk) for k3 in dph}
    m_col = {k3: jnp.maximum(inter[k3], m_row[k3]) for k3 in dph}
    e_loc = {k3: jnp.exp(m_row[k3] - m_col[k3]) for k3 in dph}
    e_int = {k3: jnp.exp(inter[k3] - m_col[k3]) for k3 in dph}
    floor = {k3: jnp.exp(-m_col[k3]) for k3 in dph}
    den = {(d, p, hh): e_loc[d, p, hh] * r[d, p, hh][:, LANES:LANES + 1]
           + e_int[d, p, hh] * q2[d, p][:, LANES + hh:LANES + hh + 1] for d, p, hh in dph}
    inv = {k3: 1.0 / jnp.maximum(jnp.abs(den[k3]), floor[k3]) for k3 in dph}
    a_loc = {k3: e_loc[k3] * inv[k3] for k3 in dph}
    a_int = {k3: e_int[k3] * inv[k3] for k3 in dph}
    head_out = {(d, p, hh): a_loc[d, p, hh] * r[d, p, hh][:, :LANES] + a_int[d, p, hh] * q2[d, p][:, :LANES]
                for d, p, hh in dph}
    for d in range(2):
        io[d][5][...] = jnp.concatenate(
            [jnp.where(lane_half == 0, head_out[d, p, 0], head_out[d, p, 1]) for p in range(2)], axis=1)


def _mlstm(mq, mk, mkt, mv, g, n_ctx):
    t = mq.shape[0]
    tm = TOK_TILE
    n_tiles = t // tm
    n_ctx_tiles = n_ctx // tm
    mirror = lambda i: jnp.where(i < n_ctx_tiles, n_ctx_tiles - 1 - i, n_tiles - 1 - (i - n_ctx_tiles))
    fwd = lambda i: (i, 0)
    bwd = lambda i: (mirror(i), 0)
    spec = lambda w, im: pl.BlockSpec((tm, w), im)
    tspec = lambda f: pl.BlockSpec((None, B_WIDTH, tm), lambda i: (f(i), 0, 0))
    ins = ([spec(B_WIDTH, fwd)] * 2 + [tspec(lambda i: i), spec(B_WIDTH, fwd), spec(LANES, fwd)]
           + [spec(B_WIDTH, bwd)] * 2 + [tspec(mirror), spec(B_WIDTH, bwd), spec(LANES, bwd)])
    return pl.pallas_call(
        functools.partial(_mlstm_kernel, chunks=tm // MLSTM_CHUNK),
        out_shape=[jax.ShapeDtypeStruct((t, B_WIDTH), F32)] * 2,
        grid=(n_tiles,),
        in_specs=ins,
        out_specs=[spec(B_WIDTH, fwd), spec(B_WIDTH, bwd)],
        scratch_shapes=[pltpu.VMEM((2, 2, LANES, 2 * LANES), F32), pltpu.VMEM((8, LANES), F32)],
        compiler_params=_cparams(("arbitrary",)),
        name="mlstm_scan",
    )(mq, mk, mkt, mv, g, mq, mk, mkt, mv, g)


def _dft_cs(n):
    j = np.arange(n, dtype=np.int64)
    ang = 2.0 * np.pi * ((j[:, None] * j[None, :]) % n).astype(np.float64) / n
    s = 1.0 / math.sqrt(n)
    return np.cos(ang) * s, np.sin(ang) * s


def _chan_mats():
    c, s = _dft_cs(C_GROUP_DIM)
    eye = np.eye(C_GROUPS)
    return np.concatenate([np.kron(eye, c), np.kron(eye, s)], axis=0).astype(np.float32)


def _split_bf16(x):
    hi = x.astype(BF16)
    return hi, (x - hi.astype(F32)).astype(BF16)


def _split_const(a):
    hi, lo = _split_bf16(jnp.asarray(a, F32))
    return jnp.stack([hi, lo])


def _dot3(a_hi, a_lo, b_hi, b_lo):
    dot = functools.partial(jnp.dot, preferred_element_type=F32)
    return dot(a_hi, b_hi) + dot(a_hi, b_lo) + dot(a_lo, b_hi)


def _fft1_kernel(fa_ref, u_ref, z_ref):
    z_ref[...] = _dot3(fa_ref[0], fa_ref[1], *_split_bf16(u_ref[...]))


def _fft2_kernel(z_ref, twc_ref, tws_ref, fb_ref, ch_ref, o_ref, *, kb):
    n2 = fb_ref.shape[1] // 2
    for j in range(kb):
        zr = z_ref[0, j]
        zi = z_ref[1, j]
        tc = twc_ref[j]
        ts = tws_ref[j]
        st = jnp.concatenate([zr * tc + zi * ts, zi * tc - zr * ts], axis=0)
        a = _dot3(fb_ref[0], fb_ref[1], *_split_bf16(st))
        ari_hi, ari_lo = _split_bf16(jnp.concatenate([a[:n2], a[n2:]], axis=1))
        o_ref[:, j * C_WIDTH:(j + 1) * C_WIDTH] = _dot3(ari_hi, ari_lo, ch_ref[0], ch_ref[1])


def _fourier_lat(u):
    n = u.shape[0]
    n1 = int(round(math.sqrt(n)))
    assert n1 * n1 == n
    n2 = n1
    c, s = _dft_cs(n1)
    fa = _split_const(np.concatenate([c, -s], axis=0))
    fb = _split_const(np.block([[c, s], [-s, c]]))
    k1 = np.arange(n1, dtype=np.int64)
    ang = 2.0 * np.pi * ((k1[:, None] * k1[None, :]) % n).astype(np.float64) / n
    twc = jnp.asarray(np.cos(ang).astype(np.float32)).reshape(n1, n2, 1)
    tws = jnp.asarray(np.sin(ang).astype(np.float32)).reshape(n1, n2, 1)
    ch = _split_const(_chan_mats())
    cols = n2 * C_WIDTH
    tb = min(2048, cols)
    z = pl.pallas_call(
        _fft1_kernel,
        out_shape=jax.ShapeDtypeStruct((2 * n1, cols), F32),
        grid=(cols // tb,),
        in_specs=[pl.BlockSpec((2, 2 * n1, n1), lambda i: (0, 0, 0)), pl.BlockSpec((n1, tb), lambda i: (0, i))],
        out_specs=pl.BlockSpec((2 * n1, tb), lambda i: (0, i)),
        compiler_params=_cparams(("arbitrary",)),
        name="fourier_stage1",
    )(fa, u.reshape(n1, cols))
    kb = 8
    out = pl.pallas_call(
        functools.partial(_fft2_kernel, kb=kb),
        out_shape=jax.ShapeDtypeStruct((n2, n1 * C_WIDTH), F32),
        grid=(n1 // kb,),
        in_specs=[pl.BlockSpec((2, kb, n2, C_WIDTH), lambda i: (0, i, 0, 0)),
                  pl.BlockSpec((kb, n2, 1), lambda i: (i, 0, 0)),
                  pl.BlockSpec((kb, n2, 1), lambda i: (i, 0, 0)),
                  pl.BlockSpec((2, 2 * n2, 2 * n2), lambda i: (0, 0, 0)),
                  pl.BlockSpec((2, 2 * C_WIDTH, C_WIDTH), lambda i: (0, 0, 0))],
        out_specs=pl.BlockSpec((n2, kb * C_WIDTH), lambda i: (0, i)),
        compiler_params=_cparams(("arbitrary",)),
        name="fourier_stage2",
    )(z.reshape(2, n1, n2, C_WIDTH), twc, tws, fb, ch)
    return out.reshape(n, C_WIDTH)


def _fft_small_kernel(fa_ref, u_ref, ch_ref, o_ref):
    n = u_ref.shape[0]
    a = jnp.dot(fa_ref[...], u_ref[...], precision=HI, preferred_element_type=F32)
    ari = jnp.concatenate([a[:n], a[n:]], axis=1)
    o_ref[...] = jnp.dot(ari, ch_ref[...], precision=HI, preferred_element_type=F32)


def _fourier_small(u):
    n = u.shape[0]
    c, s = _dft_cs(n)
    fa = jnp.asarray(np.concatenate([c, -s], axis=0).astype(np.float32))
    return pl.pallas_call(
        _fft_small_kernel,
        out_shape=jax.ShapeDtypeStruct((n, C_WIDTH), F32),
        compiler_params=_cparams(None),
        name="fourier_small",
    )(fa, u, jnp.asarray(_chan_mats()))


def _layer_norm(z, w, b):
    mu = jnp.mean(z, axis=-1, keepdims=True)
    zc = z - mu
    var = jnp.mean(zc * zc, axis=-1, keepdims=True)
    return zc * lax.rsqrt(var + EPS) * w + b


def _outproj_kernel(al_ref, ac_ref, fl_ref, fc_ref, hf_ref, hb_ref, bo_ref, x_ref, mod_ref, mnw_ref, seg_ref,
                    w_ref, lw_ref, lb_ref, o_ref, *, n_ctx_tiles, alpha):
    is_ctx = pl.program_id(0) < n_ctx_tiles
    attn = jnp.where(is_ctx, ac_ref[...], al_ref[...])
    four = jnp.where(is_ctx, fc_ref[...], fl_ref[...])
    h = hf_ref[...] + hb_ref[...]
    ms = jnp.dot(h * h, seg_ref[...], precision=HI, preferred_element_type=F32)
    y = h * lax.rsqrt(ms + EPS) * mnw_ref[...] * jax.nn.sigmoid(bo_ref[...])
    mix = (jnp.dot(attn.astype(BF16), w_ref[0:A_WIDTH, :], preferred_element_type=F32)
           + jnp.dot(y.astype(BF16), w_ref[A_WIDTH:A_WIDTH + B_WIDTH, :], preferred_element_type=F32)
           + jnp.dot(four.astype(BF16), w_ref[A_WIDTH + B_WIDTH:, :], preferred_element_type=F32))
    g1 = _mod_rows(mod_ref, 2, is_ctx)
    o_ref[...] = _layer_norm(alpha * x_ref[...] + g1 * mix, lw_ref[...], lb_ref[...])


def _outproj(attn_l, attn_c, four_l, four_c, hf, hb, bo, xt, mod, mnw, w_out_bf, ln_w, ln_b, n_ctx, with_ctx, alpha):
    tm = TOK_TILE
    nct = n_ctx // tm
    n_lat_tiles = attn_l.shape[0] // tm
    if with_ctx:
        n_tiles, uoff, n_ctx_tiles = n_lat_tiles + nct, 0, nct
        lat = lambda i: (jnp.maximum(i - nct, 0), 0)
    else:
        n_tiles, uoff, n_ctx_tiles = n_lat_tiles, nct, 0
        lat = lambda i: (i, 0)
    uni = lambda i: (i + uoff, 0)
    ctxm = lambda i: (jnp.minimum(i, nct - 1), 0)
    seg = jnp.asarray(np.kron(np.eye(B_HEADS), np.full((B_HEAD_DIM, B_HEAD_DIM), 1.0 / B_HEAD_DIM)).astype(np.float32))
    full = lambda a: pl.BlockSpec(a.shape, lambda i: (0,) * a.ndim)
    mnw2, lw2, lb2 = mnw.reshape(1, -1), ln_w.reshape(1, -1), ln_b.reshape(1, -1)
    return pl.pallas_call(
        functools.partial(_outproj_kernel, n_ctx_tiles=n_ctx_tiles, alpha=alpha),
        out_shape=jax.ShapeDtypeStruct((n_tiles * tm, D_MODEL), F32),
        grid=(n_tiles,),
        in_specs=[pl.BlockSpec((tm, A_WIDTH), lat), pl.BlockSpec((tm, A_WIDTH), ctxm),
                  pl.BlockSpec((tm, C_WIDTH), lat), pl.BlockSpec((tm, C_WIDTH), ctxm),
                  pl.BlockSpec((tm, B_WIDTH), uni), pl.BlockSpec((tm, B_WIDTH), uni),
                  pl.BlockSpec((tm, B_WIDTH), uni), pl.BlockSpec((tm, D_MODEL), uni),
                  full(mod), full(mnw2), full(seg), full(w_out_bf), full(lw2), full(lb2)],
        out_specs=pl.BlockSpec((tm, D_MODEL), lambda i: (i, 0)),
        compiler_params=_cparams(("arbitrary",)),
        name="out_projection",
    )(attn_l, attn_c, four_l, four_c, hf, hb, bo, xt, mod, mnw2, seg, w_out_bf, lw2, lb2)


def _router_kernel(x_ref, mod_ref, wrt_ref, br_ref, idx_ref, gate_ref, rank_ref, cnt_ref, run_ref, *, n_ctx_tiles):
    i = pl.program_id(0)

    @pl.when(i == 0)
    def _():
        run_ref[...] = jnp.zeros_like(run_ref)

    is_ctx = i < n_ctx_tiles
    f_in = x_ref[...] * (1.0 + _mod_rows(mod_ref, 4, is_ctx)) + _mod_rows(mod_ref, 3, is_ctx)
    logits = lax.dot_general(wrt_ref[...], f_in, (((1,), (1,)), ((), ())), precision=HI,
                             preferred_element_type=F32) + br_ref[...]
    tm = logits.shape[1]
    eidx = lax.broadcasted_iota(jnp.int32, logits.shape, 0).astype(F32)
    work = logits
    vals, sels = [], []
    for _ in range(TOP_K):
        mx = jnp.max(work, axis=0, keepdims=True)
        sel = jnp.min(jnp.where(work == mx, eidx, float(N_EXPERTS)), axis=0, keepdims=True)
        vals.append(mx)
        sels.append(sel)
        work = jnp.where(eidx == sel, -jnp.inf, work)
    es = [jnp.exp(v - vals[0]) for v in vals]
    tot = es[0] + es[1] + es[2] + es[3]
    onehots = [(eidx == s).astype(F32) for s in sels]
    oh_all = onehots[0] + onehots[1] + onehots[2] + onehots[3]
    r = lax.broadcasted_iota(jnp.int32, (tm, tm), 0)
    c = lax.broadcasted_iota(jnp.int32, (tm, tm), 1)
    before = jnp.dot(oh_all.astype(BF16), (r < c).astype(BF16), preferred_element_type=F32) + run_ref[:, 0:1]
    krow = lax.broadcasted_iota(jnp.int32, (8, tm), 0)
    idx_o = jnp.zeros((8, tm), jnp.int32)
    gate_o = jnp.zeros((8, tm), F32)
    rank_o = jnp.zeros((8, tm), jnp.int32)
    for k in range(TOP_K):
        rank = jnp.sum(onehots[k] * before, axis=0, keepdims=True).astype(jnp.int32)
        idx_o = jnp.where(krow == k, sels[k].astype(jnp.int32), idx_o)
        gate_o = jnp.where(krow == k, es[k] / tot, gate_o)
        rank_o = jnp.where(krow == k, rank, rank_o)
    idx_ref[...] = idx_o
    gate_ref[...] = gate_o
    rank_ref[...] = rank_o
    run_new = run_ref[...] + jnp.sum(oh_all, axis=1, keepdims=True)
    run_ref[...] = run_new
    cnt_ref[...] = run_new.astype(jnp.int32)


def _router(xs, mod, w_router, b_router, n_ctx_tiles):
    t = xs.shape[0]
    tm = TOK_TILE
    wrt = w_router.T
    br = b_router.reshape(-1, 1)
    full = lambda a: pl.BlockSpec(a.shape, lambda i: (0,) * a.ndim)
    col = pl.BlockSpec((8, tm), lambda i: (0, i))
    return pl.pallas_call(
        functools.partial(_router_kernel, n_ctx_tiles=n_ctx_tiles),
        out_shape=[jax.ShapeDtypeStruct((8, t), jnp.int32), jax.ShapeDtypeStruct((8, t), F32),
                   jax.ShapeDtypeStruct((8, t), jnp.int32), jax.ShapeDtypeStruct((N_EXPERTS, LANES), jnp.int32)],
        grid=(t // tm,),
        in_specs=[pl.BlockSpec((tm, D_MODEL), lambda i: (i, 0)), full(mod), full(wrt), full(br)],
        out_specs=[col, col, col, pl.BlockSpec((N_EXPERTS, LANES), lambda i: (0, 0))],
        scratch_shapes=[pltpu.VMEM((N_EXPERTS, LANES), F32)],
        compiler_params=_cparams(("arbitrary",)),
        name="moe_router",
    )(xs, mod, wrt, br)


ROW_CHUNKS = D_MODEL // LANES
DMA_UNROLL = 8


def _to_chunk_rows(ref, val):
    m = val.shape[0]
    for c in range(ROW_CHUNKS):
        ref[pl.ds(c, m, stride=ROW_CHUNKS), :] = val[:, c * LANES:(c + 1) * LANES]


def _from_chunk_rows(ref, m):
    return jnp.concatenate([ref[pl.ds(c, m, stride=ROW_CHUNKS), :] for c in range(ROW_CHUNKS)], axis=1)


def _dispatch_kernel(pstart_ref, pend_ref, dest_ref, x_ref, mod_ref, xb_ref, fbuf, zbuf, sem, zsem, *, n_ctx_tiles, bm):
    i = pl.program_id(0)
    tm = x_ref.shape[0]

    @pl.when(i == 0)
    def _():
        zbuf[...] = jnp.zeros_like(zbuf)

        def zcopy(e):
            row0 = pl.multiple_of((pend_ref[e] - bm) * ROW_CHUNKS, bm * ROW_CHUNKS)
            return pltpu.make_async_copy(zbuf, xb_ref.at[pl.ds(row0, bm * ROW_CHUNKS)], zsem)

        def zstart(e, c):
            @pl.when(pend_ref[e] > pstart_ref[e])
            def _():
                zcopy(e).start()
            return c

        def zwait(e, c):
            @pl.when(pend_ref[e] > pstart_ref[e])
            def _():
                zcopy(e).wait()
            return c

        lax.fori_loop(0, N_EXPERTS, zstart, 0)
        lax.fori_loop(0, N_EXPERTS, zwait, 0)

        def tcopy(b):
            row0 = pl.multiple_of(b * (bm * ROW_CHUNKS), bm * ROW_CHUNKS)
            return pltpu.make_async_copy(zbuf, xb_ref.at[pl.ds(row0, bm * ROW_CHUNKS)], zsem)

        def tstart(b, c):
            tcopy(b).start()
            return c

        def twait(b, c):
            tcopy(b).wait()
            return c

        tail0 = pend_ref[N_EXPERTS - 1] // bm
        n_blocks = xb_ref.shape[0] // (bm * ROW_CHUNKS)
        lax.fori_loop(tail0, n_blocks, tstart, 0)
        lax.fori_loop(tail0, n_blocks, twait, 0)

    is_ctx = i < n_ctx_tiles
    f = x_ref[...] * (1.0 + _mod_rows(mod_ref, 4, is_ctx)) + _mod_rows(mod_ref, 3, is_ctx)
    slot = i % 2
    _to_chunk_rows(fbuf.at[slot], f)

    def start(t, c):
        src = pl.multiple_of(t * ROW_CHUNKS, ROW_CHUNKS)
        for k in range(TOP_K):
            dst = pl.multiple_of(dest_ref[t * TOP_K + k] * ROW_CHUNKS, ROW_CHUNKS)
            pltpu.make_async_copy(fbuf.at[slot, pl.ds(src, ROW_CHUNKS)], xb_ref.at[pl.ds(dst, ROW_CHUNKS)],
                                  sem.at[slot]).start(priority=k % 2)
        return c

    lax.fori_loop(0, tm, start, 0, unroll=DMA_UNROLL)

    def wait_tile(s):
        for _ in range(TOP_K):
            pltpu.make_async_copy(fbuf.at[s], xb_ref.at[pl.ds(0, tm * ROW_CHUNKS)], sem.at[s]).wait()

    @pl.when(i > 0)
    def _():
        wait_tile(1 - slot)

    @pl.when(i == pl.num_programs(0) - 1)
    def _():
        wait_tile(slot)


def _dispatch(pad_start, pad_end, dest_flat, xs, mod, n_rows, n_ctx_tiles):
    t = xs.shape[0]
    tm = TOK_TILE
    gs = pltpu.PrefetchScalarGridSpec(
        num_scalar_prefetch=2,
        grid=(t // tm,),
        in_specs=[pl.BlockSpec((tm * TOP_K,), lambda i, *_: (i,), memory_space=pltpu.SMEM),
                  pl.BlockSpec((tm, D_MODEL), lambda i, *_: (i, 0)),
                  pl.BlockSpec(mod.shape, lambda i, *_: (0, 0))],
        out_specs=pl.BlockSpec(memory_space=pl.ANY),
        scratch_shapes=[pltpu.VMEM((2, tm * ROW_CHUNKS, LANES), F32), pltpu.VMEM((MOE_BM * ROW_CHUNKS, LANES), F32),
                        pltpu.SemaphoreType.DMA((2,)), pltpu.SemaphoreType.DMA],
    )
    return pl.pallas_call(
        functools.partial(_dispatch_kernel, n_ctx_tiles=n_ctx_tiles, bm=MOE_BM),
        out_shape=jax.ShapeDtypeStruct((n_rows * ROW_CHUNKS, LANES), F32),
        grid_spec=gs,
        compiler_params=_cparams(("arbitrary",)),
        name="moe_dispatch",
    )(pad_start, pad_end, dest_flat, xs, mod)


def _expert_kernel(bexp_ref, bvalid_ref, x_ref, wgu_ref, bgu_ref, wd_ref, bd_ref, y_ref, wgu_bf, wd_bf):
    i = pl.program_id(0)
    e = bexp_ref[i]
    prev = bexp_ref[jnp.maximum(i - 1, 0)]

    @pl.when(jnp.logical_or(i == 0, e != prev))
    def _():
        wgu_bf[...] = wgu_ref[...].astype(BF16)
        wd_bf[...] = wd_ref[...].astype(BF16)

    @pl.when(bvalid_ref[i] > 0)
    def _():
        bm = y_ref.shape[0] // ROW_CHUNKS
        x = _from_chunk_rows(x_ref, bm).astype(BF16)
        gu = jnp.dot(x, wgu_bf[...], preferred_element_type=F32) + bgu_ref[...]
        gate = jnp.minimum(gu[:, :D_EXPERT], SWIGLU_LIMIT)
        up = jnp.clip(gu[:, D_EXPERT:], -SWIGLU_LIMIT, SWIGLU_LIMIT)
        act = gate * jax.nn.sigmoid(SWIGLU_ALPHA * gate) * (up + 1.0)
        _to_chunk_rows(y_ref, jnp.dot(act.astype(BF16), wd_bf[...], preferred_element_type=F32) + bd_ref[...])

    @pl.when(bvalid_ref[i] == 0)
    def _():
        y_ref[...] = jnp.zeros_like(y_ref)


def _experts(block_exp, block_valid, xb, layer, w_gate_up, b_gate_up, w_down, b_down):
    n_rows = xb.shape[0] // ROW_CHUNKS
    bm = MOE_BM
    depth = w_gate_up.shape[0]
    bgu = b_gate_up.reshape(depth, N_EXPERTS, 1, 2 * D_EXPERT)
    bd = b_down.reshape(depth, N_EXPERTS, 1, D_MODEL)
    gs = pltpu.PrefetchScalarGridSpec(
        num_scalar_prefetch=2,
        grid=(n_rows // bm,),
        in_specs=[pl.BlockSpec((bm * ROW_CHUNKS, LANES), lambda i, be, bv: (i, 0)),
                  pl.BlockSpec((None, None, D_MODEL, 2 * D_EXPERT), lambda i, be, bv: (layer, be[i], 0, 0)),
                  pl.BlockSpec((None, None, 1, 2 * D_EXPERT), lambda i, be, bv: (layer, be[i], 0, 0)),
                  pl.BlockSpec((None, None, D_EXPERT, D_MODEL), lambda i, be, bv: (layer, be[i], 0, 0)),
                  pl.BlockSpec((None, None, 1, D_MODEL), lambda i, be, bv: (layer, be[i], 0, 0))],
        out_specs=pl.BlockSpec((bm * ROW_CHUNKS, LANES), lambda i, be, bv: (i, 0)),
        scratch_shapes=[pltpu.VMEM((D_MODEL, 2 * D_EXPERT), BF16), pltpu.VMEM((D_EXPERT, D_MODEL), BF16)],
    )
    return pl.pallas_call(
        _expert_kernel,
        out_shape=jax.ShapeDtypeStruct((n_rows * ROW_CHUNKS, LANES), F32),
        grid_spec=gs,
        compiler_params=_cparams(("arbitrary",)),
        name="moe_experts",
    )(block_exp, block_valid, xb, w_gate_up, bgu, w_down, bd)


def _combine_kernel(dest_ref, dnext_ref, gate_ref, x_ref, mod_ref, lw_ref, lb_ref, yb_ref, o_ref, ybuf, sem, *,
                    n_ctx_tiles, alpha):
    i = pl.program_id(0)
    tm = x_ref.shape[0]
    slot = i % 2

    def gather(d_ref, s):
        def start(t, c):
            dst = pl.multiple_of(t * ROW_CHUNKS, ROW_CHUNKS)
            for k in range(TOP_K):
                src = pl.multiple_of(d_ref[t * TOP_K + k] * ROW_CHUNKS, ROW_CHUNKS)
                pltpu.make_async_copy(yb_ref.at[pl.ds(src, ROW_CHUNKS)], ybuf.at[s, k, pl.ds(dst, ROW_CHUNKS)],
                                      sem.at[s]).start(priority=k % 2)
            return c

        lax.fori_loop(0, tm, start, 0, unroll=DMA_UNROLL)

    @pl.when(i == 0)
    def _():
        gather(dest_ref, 0)

    @pl.when(i + 1 < pl.num_programs(0))
    def _():
        gather(dnext_ref, 1 - slot)

    for k in range(TOP_K):
        pltpu.make_async_copy(yb_ref.at[pl.ds(0, tm * ROW_CHUNKS)], ybuf.at[slot, k], sem.at[slot]).wait()
    gates = gate_ref[...]
    y = gates[:, 0:1] * _from_chunk_rows(ybuf.at[slot, 0], tm)
    for k in range(1, TOP_K):
        y = y + gates[:, k:k + 1] * _from_chunk_rows(ybuf.at[slot, k], tm)
    is_ctx = i < n_ctx_tiles
    g2 = _mod_rows(mod_ref, 5, is_ctx)
    o_ref[...] = _layer_norm(alpha * x_ref[...] + g2 * y, lw_ref[...], lb_ref[...])


def _combine(dest_flat, gates, xs, mod, ln_w, ln_b, yb, n_ctx_tiles, alpha):
    t = xs.shape[0]
    tm = TOK_TILE
    n_tiles = t // tm
    lw2, lb2 = ln_w.reshape(1, -1), ln_b.reshape(1, -1)
    full = lambda a: pl.BlockSpec(a.shape, lambda i: (0,) * a.ndim)
    return pl.pallas_call(
        functools.partial(_combine_kernel, n_ctx_tiles=n_ctx_tiles, alpha=alpha),
        out_shape=jax.ShapeDtypeStruct((t, D_MODEL), F32),
        grid=(n_tiles,),
        in_specs=[pl.BlockSpec((tm * TOP_K,), lambda i: (i,), memory_space=pltpu.SMEM),
                  pl.BlockSpec((tm * TOP_K,), lambda i: (jnp.minimum(i + 1, n_tiles - 1),), memory_space=pltpu.SMEM),
                  pl.BlockSpec((tm, TOP_K), lambda i: (i, 0)),
                  pl.BlockSpec((tm, D_MODEL), lambda i: (i, 0)),
                  full(mod), full(lw2), full(lb2),
                  pl.BlockSpec(memory_space=pl.ANY)],
        out_specs=pl.BlockSpec((tm, D_MODEL), lambda i: (i, 0)),
        scratch_shapes=[pltpu.VMEM((2, TOP_K, tm * ROW_CHUNKS, LANES), F32), pltpu.SemaphoreType.DMA((2,))],
        compiler_params=_cparams(("arbitrary",)),
        name="moe_combine",
    )(dest_flat, dest_flat, gates, xs, mod, lw2, lb2, yb)


def _moe(xs, mod, layer, w_router, b_router, w_gate_up, b_gate_up, w_down, b_down, ln_w, ln_b, n_ctx_tiles, alpha):
    t = xs.shape[0]
    bm = MOE_BM
    idx, gates, rank, cnt = _router(xs, mod, w_router, b_router, n_ctx_tiles)
    counts = cnt[:, 0]
    idx, gates, rank = idx[:TOP_K].T, gates[:TOP_K].T, rank[:TOP_K].T
    padded = (counts + bm - 1) // bm * bm
    pad_end = jnp.cumsum(padded).astype(jnp.int32)
    pad_start = pad_end - padded
    dest = pad_start[idx] + rank
    dest_flat = dest.reshape(-1).astype(jnp.int32)
    n_blocks = -(-(t * TOP_K) // bm) + N_EXPERTS
    blk0 = jnp.arange(n_blocks, dtype=jnp.int32) * bm
    block_exp = jnp.minimum(jnp.sum(pad_end[None, :] <= blk0[:, None], axis=1), N_EXPERTS - 1).astype(jnp.int32)
    block_valid = (blk0 < pad_end[-1]).astype(jnp.int32)
    xb = _dispatch(pad_start, pad_end, dest_flat, xs, mod, n_blocks * bm, n_ctx_tiles)
    yb = _experts(block_exp, block_valid, xb, layer, w_gate_up, b_gate_up, w_down, b_down)
    return _combine(dest_flat, gates, xs, mod, ln_w, ln_b, yb, n_ctx_tiles, alpha)


def kernel(x, c, ctx, c_ctx, w_ada, b_ada, w_in, mlstm_gate_bias, diff_lambda, diff_norm_w, mlstm_norm_w, w_out,
           ln1_w, ln1_b, w_router, b_router, w_gate_up, b_gate_up, w_down, b_down, ln2_w, ln2_b):
    assert x.shape[0] == 1 and ctx.shape[0] == 1
    depth = w_ada.shape[0]
    n_lat, n_ctx = x.shape[1], ctx.shape[1]
    assert n_ctx % TOK_TILE == 0 and n_lat % TOK_TILE == 0
    nct = n_ctx // TOK_TILE
    alpha = (2 * depth) ** 0.25

    vecs = jnp.zeros((8, D_MODEL), F32).at[0].set(c[0]).at[1].set(c_ctx)
    mods = _ada(vecs, w_ada, b_ada)
    cos, sin = _rope_tables(n_lat, n_ctx)
    xt = jnp.concatenate([ctx[0], x[0]], axis=0)

    for l in range(depth):
        need_ctx = l < depth - 1
        lam_init = 0.8 - 0.6 * math.exp(-0.3 * l)
        mod = mods[l]
        w_ext, w_vt = _prep_w_in(w_in[l])
        q, k, vt, mkt, mq, mk, mv, mo, g, cu = _inproj(xt, mod, w_ext, w_vt, cos, sin, mlstm_gate_bias[l], n_ctx)

        dl = diff_lambda[l].astype(F32)
        lam = (jnp.exp(jnp.sum(dl[0] * dl[1])) - jnp.exp(jnp.sum(dl[2] * dl[3])) + lam_init).reshape(1)
        post = 1.0 - lam_init
        attn_l = _attention(lam, q, k, vt, diff_norm_w[l], n_ctx, n_lat, n_ctx + n_lat, post)
        four_l = _fourier_lat(cu[n_ctx:])
        if need_ctx:
            attn_c = _attention(lam, q, k, vt, diff_norm_w[l], 0, n_ctx, n_ctx, post)
            four_c = _fourier_small(cu[:n_ctx])
        else:
            attn_c, four_c = attn_l, four_l
        hf, hb = _mlstm(mq, mk, mkt, mv, g, n_ctx)
        xs = _outproj(attn_l, attn_c, four_l, four_c, hf, hb, mo, xt, mod, mlstm_norm_w[l], w_out[l].astype(BF16),
                      ln1_w[l], ln1_b[l], n_ctx, need_ctx, alpha)
        xt = _moe(xs, mod, l, w_router[l], b_router[l], w_gate_up, b_gate_up, w_down, b_down, ln2_w[l], ln2_b[l],
                  nct if need_ctx else 0, alpha)
    return xt[None]
```

```python
import functools
import math

import numpy as np
import jax
import jax.numpy as jnp
from jax import lax
from jax.experimental import pallas as pl
from jax.experimental.pallas import tpu as pltpu

F32 = jnp.float32
BF16 = jnp.bfloat16
HI = lax.Precision.HIGHEST

D_MODEL = 1024
GRID_W = 64
HEAD_DIM = 64
A_HEADS = 4
A_QK_DIM = HEAD_DIM
A_V_DIM = 2 * HEAD_DIM
A_WIDTH = A_HEADS * A_V_DIM
A_SCALE = A_QK_DIM ** -0.5
B_HEADS = 4
B_HEAD_DIM = HEAD_DIM
B_WIDTH = B_HEADS * B_HEAD_DIM
C_GROUPS = 4
C_GROUP_DIM = HEAD_DIM
C_WIDTH = C_GROUPS * C_GROUP_DIM
IN_SIZES = (A_WIDTH, A_WIDTH, A_WIDTH, B_WIDTH, B_WIDTH, B_WIDTH, B_WIDTH, 4 * B_HEADS, C_WIDTH)
ROPE_BASE = 10000.0
MLSTM_CHUNK = 64
N_EXPERTS = 32
TOP_K = 4
D_EXPERT = D_MODEL
SWIGLU_LIMIT = 7.0
SWIGLU_ALPHA = 1.702
EPS = 1e-5

LANES = 128
VMEM_LIMIT = 56 * 1024 * 1024
TOK_TILE = 256
ATT_TQ = 512
ATT_COLS = 256
ATT_UNROLL = 32
MOE_BM = 512

_OQ, _OK = 0, 512
_OMQ, _OMK, _OMV, _OMO, _OG, _OCU = 1024, 1280, 1536, 1792, 2048, 2176
Q_SCALE = A_SCALE * math.log2(math.e)
VT_ROWS = A_V_DIM + 16


def _cparams(sem, vmem=VMEM_LIMIT):
    return pltpu.CompilerParams(dimension_semantics=sem, vmem_limit_bytes=vmem)


def _ada_kernel(v_ref, w_ref, b_ref, o_ref):
    v = v_ref[...]
    s = v * jax.nn.sigmoid(v)
    o_ref[...] = jnp.dot(s, w_ref[...], precision=HI, preferred_element_type=F32) + b_ref[...]


def _ada(vecs, w_ada, b_ada):
    depth, d, d6 = w_ada.shape
    bn = 1024
    return pl.pallas_call(
        _ada_kernel,
        out_shape=jax.ShapeDtypeStruct((depth, 8, d6), F32),
        grid=(depth, d6 // bn),
        in_specs=[pl.BlockSpec((8, d), lambda l, j: (0, 0)),
                  pl.BlockSpec((None, d, bn), lambda l, j: (l, 0, j)),
                  pl.BlockSpec((None, 1, bn), lambda l, j: (l, 0, j))],
        out_specs=pl.BlockSpec((None, 8, bn), lambda l, j: (l, 0, j)),
        compiler_params=_cparams(("arbitrary", "arbitrary")),
        name="ada_modulation",
    )(vecs, w_ada, b_ada.reshape(depth, 1, d6))


def _mod_rows(mod_ref, k, is_ctx):
    lat = mod_ref[0:1, k * D_MODEL:(k + 1) * D_MODEL]
    ctx = mod_ref[1:2, k * D_MODEL:(k + 1) * D_MODEL]
    return jnp.where(is_ctx, ctx, lat)


def _inproj_kernel(x_ref, mod_ref, w_ref, wvt_ref, cos_ref, sin_ref, gb_ref,
                   q_ref, k_ref, vt_ref, mkt_ref, mq_ref, mk_ref, mv_ref, mo_ref, g_ref, cu_ref, *, n_ctx_tiles):
    is_ctx = pl.program_id(0) < n_ctx_tiles
    sh = _mod_rows(mod_ref, 0, is_ctx)
    sc = _mod_rows(mod_ref, 1, is_ctx)
    hm = (x_ref[...] * (1.0 + sc) + sh).astype(BF16)

    def proj(off, width):
        return jnp.dot(hm, w_ref[:, off:off + width], preferred_element_type=F32)

    cos = cos_ref[...]
    sin = sin_ref[...]
    quarter = A_QK_DIM // 4
    first_half = (lax.broadcasted_iota(jnp.int32, cos.shape, 1) % (2 * quarter)) < quarter

    def rope(t):
        rot = jnp.where(first_half, -pltpu.roll(t, LANES - quarter, 1), pltpu.roll(t, quarter, 1))
        return t * cos + rot * sin

    for j in range(A_HEADS):
        o = j * LANES
        q_ref[:, o:o + LANES] = (rope(proj(_OQ + o, LANES)) * Q_SCALE).astype(BF16)
        k_ref[:, o:o + LANES] = rope(proj(_OK + o, LANES)).astype(BF16)
    tr = lax.dot_general(wvt_ref[...], hm, (((1,), (1,)), ((), ())), preferred_element_type=F32)
    mkt_ref[...] = tr[A_WIDTH:, :] * (B_HEAD_DIM ** -0.5)
    vt = tr[:A_WIDTH, :].astype(BF16)
    for j in range(A_HEADS):
        vt_ref[j, 0:A_V_DIM, :] = vt[j * A_V_DIM:(j + 1) * A_V_DIM, :]
        vt_ref[j, A_V_DIM:, :] = jnp.ones((VT_ROWS - A_V_DIM, vt.shape[1]), BF16)
    mq_ref[...] = proj(_OMQ, B_WIDTH)
    mk_ref[...] = proj(_OMK, B_WIDTH) * (B_HEAD_DIM ** -0.5)
    mv_ref[...] = proj(_OMV, B_WIDTH)
    mo_ref[...] = proj(_OMO, B_WIDTH)
    g_ref[...] = proj(_OG, LANES) + gb_ref[...]
    cu_ref[...] = proj(_OCU, C_WIDTH)


def _prep_w_in(w_in):
    offs = np.cumsum((0,) + IN_SIZES)
    aq, ak, av, bq, bk, bv, bo, bg, cu = [w_in[:, offs[i]:offs[i + 1]] for i in range(9)]
    bg = jnp.pad(bg, ((0, 0), (0, LANES - bg.shape[1])))
    w = jnp.concatenate([aq, ak, bq, bk, bv, bo, bg, cu], axis=1)
    return w.astype(BF16), jnp.concatenate([av, bk], axis=1).T.astype(BF16)


def _rope_tables(n_lat, n_ctx):
    rows = n_lat // GRID_W
    axis_dim = A_QK_DIM // 2
    inv_freq = ROPE_BASE ** (-np.arange(0, axis_dim, 2, dtype=np.float64) / axis_dim)
    ang_r = np.repeat(np.arange(rows, dtype=np.float64), GRID_W)[:, None] * inv_freq
    ang_c = np.tile(np.arange(GRID_W, dtype=np.float64), rows)[:, None] * inv_freq
    ang = np.concatenate([ang_r, ang_r, ang_c, ang_c] * 2, axis=-1)
    cos = np.concatenate([np.ones((n_ctx, 2 * A_QK_DIM)), np.cos(ang)], axis=0)
    sin = np.concatenate([np.zeros((n_ctx, 2 * A_QK_DIM)), np.sin(ang)], axis=0)
    return jnp.asarray(cos.astype(np.float32)), jnp.asarray(sin.astype(np.float32))


def _inproj(xt, mod, w_ext, w_vt, cos, sin, gate_bias, n_ctx):
    t = xt.shape[0]
    tm = TOK_TILE
    gb = jnp.pad(gate_bias.reshape(1, -1), ((0, 0), (0, LANES - 4 * B_HEADS)))
    row = lambda w: pl.BlockSpec((tm, w), lambda i: (i, 0))
    full = lambda a: pl.BlockSpec(a.shape, lambda i: (0,) * a.ndim)
    sds = jax.ShapeDtypeStruct
    rows_out = [(A_WIDTH, BF16)] * 2 + [(B_WIDTH, F32)] * 4 + [(LANES, F32), (C_WIDTH, F32)]
    out_shape = [sds((t, w), dt) for w, dt in rows_out]
    out_specs = [row(w) for w, _ in rows_out]
    out_shape.insert(2, sds((t // tm, A_HEADS, VT_ROWS, tm), BF16))
    out_specs.insert(2, pl.BlockSpec((None, A_HEADS, VT_ROWS, tm), lambda i: (i, 0, 0, 0)))
    out_shape.insert(3, sds((t // tm, B_WIDTH, tm), F32))
    out_specs.insert(3, pl.BlockSpec((None, B_WIDTH, tm), lambda i: (i, 0, 0)))
    return pl.pallas_call(
        functools.partial(_inproj_kernel, n_ctx_tiles=n_ctx // tm),
        out_shape=out_shape,
        grid=(t // tm,),
        in_specs=[row(D_MODEL), full(mod), full(w_ext), full(w_vt), row(LANES), row(LANES), full(gb)],
        out_specs=out_specs,
        compiler_params=_cparams(("arbitrary",)),
        name="in_projection",
    )(xt, mod, w_ext, w_vt, cos, sin, gb)


def _attn_kernel(lam_ref, q_ref, k_ref, vt_ref, nw_ref, o_ref, acc_ref, s_ref, *, n_kv, post_scale):
    q = q_ref[...]
    tq = q.shape[0]
    tk = vt_ref.shape[-1]
    lane = lax.broadcasted_iota(jnp.int32, q.shape, 1)
    zero = jnp.zeros_like(q)
    q2 = jnp.concatenate([jnp.where(lane < A_QK_DIM, q, zero), jnp.where(lane >= A_QK_DIM, q, zero)], axis=0)
    acc_ref[...] = jnp.zeros_like(acc_ref)

    def scores(j, slot):
        kb = k_ref[pl.ds(pl.multiple_of(j * tk, tk), tk), :]
        st = lax.dot_general(kb, q2, (((1,), (1,)), ((), ())), preferred_element_type=F32)
        s_ref[slot] = st
        return jnp.max(st, axis=0, keepdims=True)

    def consume(j, slot, cmax, m):
        m_new = jnp.maximum(m, cmax)
        alpha = jnp.exp2(m - m_new)
        for h in range(2 * tq // ATT_COLS):
            cs = slice(h * ATT_COLS, (h + 1) * ATT_COLS)
            p = jnp.exp2(s_ref[slot, :, cs] - m_new[:, cs]).astype(BF16)
            r = jnp.dot(vt_ref[j], p, preferred_element_type=F32)
            acc_ref[:, cs] = alpha[:, cs] * acc_ref[:, cs] + r
        return m_new

    cm0 = scores(0, 0)

    unroll = math.gcd(n_kv - 1, ATT_UNROLL)

    def group(t, carry):
        cm, m = carry
        a = unroll * t
        for i in range(unroll):
            cm_next = scores(a + i + 1, (i + 1) % 2)
            m = consume(a + i, i % 2, cm, m)
            cm = cm_next
        return cm, m

    assert unroll % 2 == 0
    cm_last, m = lax.fori_loop(0, (n_kv - 1) // unroll, group, (cm0, jnp.full((1, 2 * tq), -jnp.inf, F32)))
    consume(n_kv - 1, 0, cm_last, m)
    l = acc_ref[A_V_DIM:A_V_DIM + 1, :]
    acc = acc_ref[0:A_V_DIM, :]
    ot = acc[:, :tq] * (1.0 / l[:, :tq]) - acc[:, tq:] * (lam_ref[0] / l[:, tq:])
    yt = ot * lax.rsqrt(jnp.mean(ot * ot, axis=0, keepdims=True) + EPS)
    o_ref[...] = yt.T * nw_ref[...] * post_scale


def _attention(lam, q, k, vt, norm_w, q_row0, n_q, n_keys, post_scale):
    tq = min(ATT_TQ, n_q)
    tk = vt.shape[-1]
    n_kv = n_keys // tk
    if q_row0 % tq:
        q, q_row0 = q[q_row0:q_row0 + n_q], 0
    q_blk0 = q_row0 // tq
    return pl.pallas_call(
        functools.partial(_attn_kernel, n_kv=n_kv, post_scale=post_scale),
        out_shape=jax.ShapeDtypeStruct((n_q, A_WIDTH), F32),
        grid=(A_HEADS, n_q // tq),
        in_specs=[pl.BlockSpec(memory_space=pltpu.SMEM),
                  pl.BlockSpec((tq, LANES), lambda h, i: (i + q_blk0, h)),
                  pl.BlockSpec((n_keys, LANES), lambda h, i: (0, h)),
                  pl.BlockSpec((n_kv, None, VT_ROWS, tk), lambda h, i: (0, h, 0, 0)),
                  pl.BlockSpec((1, LANES), lambda h, i: (0, h))],
        out_specs=pl.BlockSpec((tq, LANES), lambda h, i: (i, h)),
        scratch_shapes=[pltpu.VMEM((VT_ROWS, 2 * tq), F32), pltpu.VMEM((2, tk, 2 * tq), F32)],
        compiler_params=_cparams(("arbitrary", "arbitrary")),
        name="diff_attention",
    )(lam, q, k, vt, norm_w.reshape(1, A_WIDTH))


def _mlstm_kernel(qf, kf, ktf, vf, gf, qb, kb, ktb, vb, gb, of_ref, ob_ref, s_ref, m_ref, *, chunks):
    L = MLSTM_CHUNK
    tm = chunks * L
    neg = -jnp.inf

    @pl.when(pl.program_id(0) == 0)
    def _():
        s_ref[...] = jnp.zeros_like(s_ref)
        m_ref[...] = jnp.zeros_like(m_ref)

    tr = lax.broadcasted_iota(jnp.int32, (tm, tm), 0)
    tc = lax.broadcasted_iota(jnp.int32, (tm, tm), 1)
    same_chunk = tr // L == tc // L
    masks = (same_chunk & (tr >= tc), same_chunk & (tr <= tc))
    row_chunk = lax.broadcasted_iota(jnp.int32, (tm, 1), 0) // L
    col_chunk = lax.broadcasted_iota(jnp.int32, (1, tm), 1) // L
    lane_half = lax.broadcasted_iota(jnp.int32, (tm, LANES), 1) // B_HEAD_DIM
    row_half = lax.broadcasted_iota(jnp.int32, (LANES, 1), 0) // B_HEAD_DIM
    rr = lax.broadcasted_iota(jnp.int32, (LANES, 2 * LANES), 0) // B_HEAD_DIM
    cc = lax.broadcasted_iota(jnp.int32, (LANES, 2 * LANES), 1)
    blockmask = jnp.where(cc < LANES, cc // B_HEAD_DIM, cc - LANES) == rr
    aug = jnp.where(lax.broadcasted_iota(jnp.int32, (tm, LANES), 1) < 2, 1.0, 0.0).astype(F32)
    io = ((qf, kf, ktf, vf, gf, of_ref), (qb, kb, ktb, vb, gb, ob_ref))

    def by_chunk(vals, chunk_ids):
        out = vals[chunks - 1]
        for c in range(chunks - 2, -1, -1):
            out = jnp.where(chunk_ids == c, vals[c], out)
        return out

    dp = [(d, p) for d in range(2) for p in range(2)]
    dph = [(d, p, hh) for d, p in dp for hh in range(2)]

    g_tile, b_tile, g_t, b_t = {}, {}, {}, {}
    for d in range(2):
        g_tile[d] = io[d][4][...]
        b_tile[d] = jnp.dot(masks[d].astype(F32), jax.nn.log_sigmoid(g_tile[d]), precision=HI,
                            preferred_element_type=F32)
        g_t[d] = g_tile[d].T
        b_t[d] = b_tile[d].T
    qp, kp, v_aug, ktp = {}, {}, {}, {}
    for d, p in dp:
        sl = slice(p * LANES, (p + 1) * LANES)
        qp[d, p] = io[d][0][:, sl]
        kp[d, p] = io[d][1][:, sl].astype(BF16)
        ktp[d, p] = io[d][2][sl, :]
        v_aug[d, p] = jnp.concatenate([io[d][3][:, sl], aug], axis=1).astype(BF16)

    b_col, b_row, i_row, gs, a_row, m_loc, w_t = {}, {}, {}, {}, {}, {}, {}
    for d, p, hh in dph:
        h = 2 * p + hh
        il, fl = 4 * d + h, 8 + 4 * d + h
        last = L - 1 if d == 0 else 0
        b_col[d, p, hh] = b_tile[d][:, fl:fl + 1]
        b_row[d, p, hh] = b_t[d][fl:fl + 1, :]
        i_row[d, p, hh] = g_t[d][il:il + 1, :]
        gs[d, p, hh] = [b_row[d, p, hh][:, c * L + last:c * L + last + 1] for c in range(chunks)]
        a_row[d, p, hh] = by_chunk(gs[d, p, hh], col_chunk) - b_row[d, p, hh] + i_row[d, p, hh]
    for k3 in dph:
        m_loc[k3] = [jnp.max(jnp.where(col_chunk == c, a_row[k3], neg), axis=-1, keepdims=True)
                     for c in range(chunks)]
    for k3 in dph:
        w_t[k3] = jnp.exp(a_row[k3] - by_chunk(m_loc[k3], col_chunk))

    upd, sqk = {}, {}
    for d, p in dp:
        kw_t = ktp[d, p] * jnp.where(row_half == 0, w_t[d, p, 0], w_t[d, p, 1])
        for c in range(chunks):
            kw_c = jnp.where(col_chunk == c, kw_t, 0.0).astype(BF16)
            upd[d, p, c] = jnp.where(blockmask, jnp.dot(kw_c, v_aug[d, p], preferred_element_type=F32), 0.0)
    for d, p, hh in dph:
        qm = jnp.where(lane_half == hh, qp[d, p], 0.0).astype(BF16)
        sqk[d, p, hh] = lax.dot_general(qm, kp[d, p], (((1,), (1,)), ((), ())), preferred_element_type=F32)

    m_old = m_ref[...]
    orders = (range(chunks), range(chunks - 1, -1, -1))
    m_at, dec, inj, m_end = {}, {}, {}, {}
    for d, p, hh in dph:
        row = 4 * d + 2 * p + hh
        m_prev = m_old[row:row + 1, 0:1]
        for c in orders[d]:
            m_at[d, p, hh, c] = m_prev
            g_c = gs[d, p, hh][c]
            m_new = jnp.maximum(g_c + m_prev, m_loc[d, p, hh][c])
            dec[d, p, hh, c] = jnp.exp(g_c + m_prev - m_new)
            inj[d, p, hh, c] = jnp.exp(m_loc[d, p, hh][c] - m_new)
            m_prev = m_new
        m_end[row] = m_prev
    state_at, state_end = {}, {}
    for d, p in dp:
        state = s_ref[d, p]
        for c in orders[d]:
            state_at[d, p, c] = state
            state = (jnp.where(rr == 0, dec[d, p, 0, c], dec[d, p, 1, c]) * state
                     + jnp.where(rr == 0, inj[d, p, 0, c], inj[d, p, 1, c]) * upd[d, p, c])
        state_end[d, p] = state
    for d, p in dp:
        s_ref[d, p] = state_end[d, p]
    m_rows = lax.broadcasted_iota(jnp.int32, m_old.shape, 0)
    m_out = m_old
    for row, val in m_end.items():
        m_out = jnp.where(m_rows == row, val, m_out)
    m_ref[...] = m_out

    dmat = {k3: jnp.where(masks[k3[0]], b_col[k3] + (i_row[k3] - b_row[k3]), neg) for k3 in dph}
    m_row = {k3: jnp.max(dmat[k3], axis=-1, keepdims=True) for k3 in dph}
    s_loc = {k3: (sqk[k3] * jnp.exp(dmat[k3] - m_row[k3])).astype(BF16) for k3 in dph}

    r = {}
    for d, p, hh in dph:
        r[d, p, hh] = jnp.dot(s_loc[d, p, hh], v_aug[d, p], preferred_element_type=F32)
    q2 = {}
    for d, p in dp:
        q2[d, p] = jnp.concatenate(
            [jnp.dot(qp[d, p][c * L:(c + 1) * L].astype(BF16), state_at[d, p, c].astype(BF16),
                     preferred_element_type=F32) for c in range(chunks)], axis=0)

    inter = {k3: b_col[k3] + by_chunk([m_at[k3 + (c,)] for c in range(chunks)], row_chunk) for k3 in dph}
    m_col = {k3: jnp.maximum(inter[k3], m_row[k3]) for k3 in dph}
    e_loc = {k3: jnp.exp(m_row[k3] - m_col[k3]) for k3 in dph}
    e_int = {k3: jnp.exp(inter[k3] - m_col[k3]) for k3 in dph}
    floor = {k3: jnp.exp(-m_col[k3]) for k3 in dph}
    den = {(d, p, hh): e_loc[d, p, hh] * r[d, p, hh][:, LANES:LANES + 1]
           + e_int[d, p, hh] * q2[d, p][:, LANES + hh:LANES + hh + 1] for d, p, hh in dph}
    inv = {k3: 1.0 / jnp.maximum(jnp.abs(den[k3]), floor[k3]) for k3 in dph}
    a_loc = {k3: e_loc[k3] * inv[k3] for k3 in dph}
    a_int = {k3: e_int[k3] * inv[k3] for k3 in dph}
    head_out = {(d, p, hh): a_loc[d, p, hh] * r[d, p, hh][:, :LANES] + a_int[d, p, hh] * q2[d, p][:, :LANES]
                for d, p, hh in dph}
    for d in range(2):
        io[d][5][...] = jnp.concatenate(
            [jnp.where(lane_half == 0, head_out[d, p, 0], head_out[d, p, 1]) for p in range(2)], axis=1)


def _mlstm(mq, mk, mkt, mv, g, n_ctx):
    t = mq.shape[0]
    tm = TOK_TILE
    n_tiles = t // tm
    n_ctx_tiles = n_ctx // tm
    mirror = lambda i: jnp.where(i < n_ctx_tiles, n_ctx_tiles - 1 - i, n_tiles - 1 - (i - n_ctx_tiles))
    fwd = lambda i: (i, 0)
    bwd = lambda i: (mirror(i), 0)
    spec = lambda w, im: pl.BlockSpec((tm, w), im)
    tspec = lambda f: pl.BlockSpec((None, B_WIDTH, tm), lambda i: (f(i), 0, 0))
    ins = ([spec(B_WIDTH, fwd)] * 2 + [tspec(lambda i: i), spec(B_WIDTH, fwd), spec(LANES, fwd)]
           + [spec(B_WIDTH, bwd)] * 2 + [tspec(mirror), spec(B_WIDTH, bwd), spec(LANES, bwd)])
    return pl.pallas_call(
        functools.partial(_mlstm_kernel, chunks=tm // MLSTM_CHUNK),
        out_shape=[jax.ShapeDtypeStruct((t, B_WIDTH), F32)] * 2,
        grid=(n_tiles,),
        in_specs=ins,
        out_specs=[spec(B_WIDTH, fwd), spec(B_WIDTH, bwd)],
        scratch_shapes=[pltpu.VMEM((2, 2, LANES, 2 * LANES), F32), pltpu.VMEM((8, LANES), F32)],
        compiler_params=_cparams(("arbitrary",)),
        name="mlstm_scan",
    )(mq, mk, mkt, mv, g, mq, mk, mkt, mv, g)


def _dft_cs(n):
    j = np.arange(n, dtype=np.int64)
    ang = 2.0 * np.pi * ((j[:, None] * j[None, :]) % n).astype(np.float64) / n
    s = 1.0 / math.sqrt(n)
    return np.cos(ang) * s, np.sin(ang) * s


def _chan_mats():
    c, s = _dft_cs(C_GROUP_DIM)
    eye = np.eye(C_GROUPS)
    return np.concatenate([np.kron(eye, c), np.kron(eye, s)], axis=0).astype(np.float32)


def _split_bf16(x):
    hi = x.astype(BF16)
    return hi, (x - hi.astype(F32)).astype(BF16)


def _split_const(a):
    hi, lo = _split_bf16(jnp.asarray(a, F32))
    return jnp.stack([hi, lo])


def _dot3(a_hi, a_lo, b_hi, b_lo):
    dot = functools.partial(jnp.dot, preferred_element_type=F32)
    return dot(a_hi, b_hi) + dot(a_hi, b_lo) + dot(a_lo, b_hi)


def _fft1_kernel(fa_ref, u_ref, z_ref):
    z_ref[...] = _dot3(fa_ref[0], fa_ref[1], *_split_bf16(u_ref[...]))


def _fft2_kernel(z_ref, twc_ref, tws_ref, fb_ref, ch_ref, o_ref, *, kb):
    n2 = fb_ref.shape[1] // 2
    for j in range(kb):
        zr = z_ref[0, j]
        zi = z_ref[1, j]
        tc = twc_ref[j]
        ts = tws_ref[j]
        st = jnp.concatenate([zr * tc + zi * ts, zi * tc - zr * ts], axis=0)
        a = _dot3(fb_ref[0], fb_ref[1], *_split_bf16(st))
        ari_hi, ari_lo = _split_bf16(jnp.concatenate([a[:n2], a[n2:]], axis=1))
        o_ref[:, j * C_WIDTH:(j + 1) * C_WIDTH] = _dot3(ari_hi, ari_lo, ch_ref[0], ch_ref[1])


def _fourier_lat(u):
    n = u.shape[0]
    n1 = int(round(math.sqrt(n)))
    assert n1 * n1 == n
    n2 = n1
    c, s = _dft_cs(n1)
    fa = _split_const(np.concatenate([c, -s], axis=0))
    fb = _split_const(np.block([[c, s], [-s, c]]))
    k1 = np.arange(n1, dtype=np.int64)
    ang = 2.0 * np.pi * ((k1[:, None] * k1[None, :]) % n).astype(np.float64) / n
    twc = jnp.asarray(np.cos(ang).astype(np.float32)).reshape(n1, n2, 1)
    tws = jnp.asarray(np.sin(ang).astype(np.float32)).reshape(n1, n2, 1)
    ch = _split_const(_chan_mats())
    cols = n2 * C_WIDTH
    tb = min(2048, cols)
    z = pl.pallas_call(
        _fft1_kernel,
        out_shape=jax.ShapeDtypeStruct((2 * n1, cols), F32),
        grid=(cols // tb,),
        in_specs=[pl.BlockSpec((2, 2 * n1, n1), lambda i: (0, 0, 0)), pl.BlockSpec((n1, tb), lambda i: (0, i))],
        out_specs=pl.BlockSpec((2 * n1, tb), lambda i: (0, i)),
        compiler_params=_cparams(("arbitrary",)),
        name="fourier_stage1",
    )(fa, u.reshape(n1, cols))
    kb = 8
    out = pl.pallas_call(
        functools.partial(_fft2_kernel, kb=kb),
        out_shape=jax.ShapeDtypeStruct((n2, n1 * C_WIDTH), F32),
        grid=(n1 // kb,),
        in_specs=[pl.BlockSpec((2, kb, n2, C_WIDTH), lambda i: (0, i, 0, 0)),
                  pl.BlockSpec((kb, n2, 1), lambda i: (i, 0, 0)),
                  pl.BlockSpec((kb, n2, 1), lambda i: (i, 0, 0)),
                  pl.BlockSpec((2, 2 * n2, 2 * n2), lambda i: (0, 0, 0)),
                  pl.BlockSpec((2, 2 * C_WIDTH, C_WIDTH), lambda i: (0, 0, 0))],
        out_specs=pl.BlockSpec((n2, kb * C_WIDTH), lambda i: (0, i)),
        compiler_params=_cparams(("arbitrary",)),
        name="fourier_stage2",
    )(z.reshape(2, n1, n2, C_WIDTH), twc, tws, fb, ch)
    return out.reshape(n, C_WIDTH)


def _fft_small_kernel(fa_ref, u_ref, ch_ref, o_ref):
    n = u_ref.shape[0]
    a = jnp.dot(fa_ref[...], u_ref[...], precision=HI, preferred_element_type=F32)
    ari = jnp.concatenate([a[:n], a[n:]], axis=1)
    o_ref[...] = jnp.dot(ari, ch_ref[...], precision=HI, preferred_element_type=F32)


def _fourier_small(u):
    n = u.shape[0]
    c, s = _dft_cs(n)
    fa = jnp.asarray(np.concatenate([c, -s], axis=0).astype(np.float32))
    return pl.pallas_call(
        _fft_small_kernel,
        out_shape=jax.ShapeDtypeStruct((n, C_WIDTH), F32),
        compiler_params=_cparams(None),
        name="fourier_small",
    )(fa, u, jnp.asarray(_chan_mats()))


def _layer_norm(z, w, b):
    mu = jnp.mean(z, axis=-1, keepdims=True)
    zc = z - mu
    var = jnp.mean(zc * zc, axis=-1, keepdims=True)
    return zc * lax.rsqrt(var + EPS) * w + b


def _outproj_kernel(al_ref, ac_ref, fl_ref, fc_ref, hf_ref, hb_ref, bo_ref, x_ref, mod_ref, mnw_ref, seg_ref,
                    w_ref, lw_ref, lb_ref, o_ref, *, n_ctx_tiles, alpha):
    is_ctx = pl.program_id(0) < n_ctx_tiles
    attn = jnp.where(is_ctx, ac_ref[...], al_ref[...])
    four = jnp.where(is_ctx, fc_ref[...], fl_ref[...])
    h = hf_ref[...] + hb_ref[...]
    ms = jnp.dot(h * h, seg_ref[...], precision=HI, preferred_element_type=F32)
    y = h * lax.rsqrt(ms + EPS) * mnw_ref[...] * jax.nn.sigmoid(bo_ref[...])
    mix = (jnp.dot(attn.astype(BF16), w_ref[0:A_WIDTH, :], preferred_element_type=F32)
           + jnp.dot(y.astype(BF16), w_ref[A_WIDTH:A_WIDTH + B_WIDTH, :], preferred_element_type=F32)
           + jnp.dot(four.astype(BF16), w_ref[A_WIDTH + B_WIDTH:, :], preferred_element_type=F32))
    g1 = _mod_rows(mod_ref, 2, is_ctx)
    o_ref[...] = _layer_norm(alpha * x_ref[...] + g1 * mix, lw_ref[...], lb_ref[...])


def _outproj(attn_l, attn_c, four_l, four_c, hf, hb, bo, xt, mod, mnw, w_out_bf, ln_w, ln_b, n_ctx, with_ctx, alpha):
    tm = TOK_TILE
    nct = n_ctx // tm
    n_lat_tiles = attn_l.shape[0] // tm
    if with_ctx:
        n_tiles, uoff, n_ctx_tiles = n_lat_tiles + nct, 0, nct
        lat = lambda i: (jnp.maximum(i - nct, 0), 0)
    else:
        n_tiles, uoff, n_ctx_tiles = n_lat_tiles, nct, 0
        lat = lambda i: (i, 0)
    uni = lambda i: (i + uoff, 0)
    ctxm = lambda i: (jnp.minimum(i, nct - 1), 0)
    seg = jnp.asarray(np.kron(np.eye(B_HEADS), np.full((B_HEAD_DIM, B_HEAD_DIM), 1.0 / B_HEAD_DIM)).astype(np.float32))
    full = lambda a: pl.BlockSpec(a.shape, lambda i: (0,) * a.ndim)
    mnw2, lw2, lb2 = mnw.reshape(1, -1), ln_w.reshape(1, -1), ln_b.reshape(1, -1)
    return pl.pallas_call(
        functools.partial(_outproj_kernel, n_ctx_tiles=n_ctx_tiles, alpha=alpha),
        out_shape=jax.ShapeDtypeStruct((n_tiles * tm, D_MODEL), F32),
        grid=(n_tiles,),
        in_specs=[pl.BlockSpec((tm, A_WIDTH), lat), pl.BlockSpec((tm, A_WIDTH), ctxm),
                  pl.BlockSpec((tm, C_WIDTH), lat), pl.BlockSpec((tm, C_WIDTH), ctxm),
                  pl.BlockSpec((tm, B_WIDTH), uni), pl.BlockSpec((tm, B_WIDTH), uni),
                  pl.BlockSpec((tm, B_WIDTH), uni), pl.BlockSpec((tm, D_MODEL), uni),
                  full(mod), full(mnw2), full(seg), full(w_out_bf), full(lw2), full(lb2)],
        out_specs=pl.BlockSpec((tm, D_MODEL), lambda i: (i, 0)),
        compiler_params=_cparams(("arbitrary",)),
        name="out_projection",
    )(attn_l, attn_c, four_l, four_c, hf, hb, bo, xt, mod, mnw2, seg, w_out_bf, lw2, lb2)


def _router_kernel(x_ref, mod_ref, wrt_ref, br_ref, idx_ref, gate_ref, rank_ref, cnt_ref, run_ref, *, n_ctx_tiles):
    i = pl.program_id(0)

    @pl.when(i == 0)
    def _():
        run_ref[...] = jnp.zeros_like(run_ref)

    is_ctx = i < n_ctx_tiles
    f_in = x_ref[...] * (1.0 + _mod_rows(mod_ref, 4, is_ctx)) + _mod_rows(mod_ref, 3, is_ctx)
    logits = lax.dot_general(wrt_ref[...], f_in, (((1,), (1,)), ((), ())), precision=HI,
                             preferred_element_type=F32) + br_ref[...]
    tm = logits.shape[1]
    eidx = lax.broadcasted_iota(jnp.int32, logits.shape, 0).astype(F32)
    work = logits
    vals, sels = [], []
    for _ in range(TOP_K):
        mx = jnp.max(work, axis=0, keepdims=True)
        sel = jnp.min(jnp.where(work == mx, eidx, float(N_EXPERTS)), axis=0, keepdims=True)
        vals.append(mx)
        sels.append(sel)
        work = jnp.where(eidx == sel, -jnp.inf, work)
    es = [jnp.exp(v - vals[0]) for v in vals]
    tot = es[0] + es[1] + es[2] + es[3]
    onehots = [(eidx == s).astype(F32) for s in sels]
    oh_all = onehots[0] + onehots[1] + onehots[2] + onehots[3]
    r = lax.broadcasted_iota(jnp.int32, (tm, tm), 0)
    c = lax.broadcasted_iota(jnp.int32, (tm, tm), 1)
    before = jnp.dot(oh_all.astype(BF16), (r < c).astype(BF16), preferred_element_type=F32) + run_ref[:, 0:1]
    krow = lax.broadcasted_iota(jnp.int32, (8, tm), 0)
    idx_o = jnp.zeros((8, tm), jnp.int32)
    gate_o = jnp.zeros((8, tm), F32)
    rank_o = jnp.zeros((8, tm), jnp.int32)
    for k in range(TOP_K):
        rank = jnp.sum(onehots[k] * before, axis=0, keepdims=True).astype(jnp.int32)
        idx_o = jnp.where(krow == k, sels[k].astype(jnp.int32), idx_o)
        gate_o = jnp.where(krow == k, es[k] / tot, gate_o)
        rank_o = jnp.where(krow == k, rank, rank_o)
    idx_ref[...] = idx_o
    gate_ref[...] = gate_o
    rank_ref[...] = rank_o
    run_new = run_ref[...] + jnp.sum(oh_all, axis=1, keepdims=True)
    run_ref[...] = run_new
    cnt_ref[...] = run_new.astype(jnp.int32)


def _router(xs, mod, w_router, b_router, n_ctx_tiles):
    t = xs.shape[0]
    tm = TOK_TILE
    wrt = w_router.T
    br = b_router.reshape(-1, 1)
    full = lambda a: pl.BlockSpec(a.shape, lambda i: (0,) * a.ndim)
    col = pl.BlockSpec((8, tm), lambda i: (0, i))
    return pl.pallas_call(
        functools.partial(_router_kernel, n_ctx_tiles=n_ctx_tiles),
        out_shape=[jax.ShapeDtypeStruct((8, t), jnp.int32), jax.ShapeDtypeStruct((8, t), F32),
                   jax.ShapeDtypeStruct((8, t), jnp.int32), jax.ShapeDtypeStruct((N_EXPERTS, LANES), jnp.int32)],
        grid=(t // tm,),
        in_specs=[pl.BlockSpec((tm, D_MODEL), lambda i: (i, 0)), full(mod), full(wrt), full(br)],
        out_specs=[col, col, col, pl.BlockSpec((N_EXPERTS, LANES), lambda i: (0, 0))],
        scratch_shapes=[pltpu.VMEM((N_EXPERTS, LANES), F32)],
        compiler_params=_cparams(("arbitrary",)),
        name="moe_router",
    )(xs, mod, wrt, br)


ROW_CHUNKS = D_MODEL // LANES
DMA_UNROLL = 8


def _to_chunk_rows(ref, val):
    m = val.shape[0]
    for c in range(ROW_CHUNKS):
        ref[pl.ds(c, m, stride=ROW_CHUNKS), :] = val[:, c * LANES:(c + 1) * LANES]


def _from_chunk_rows(ref, m):
    return jnp.concatenate([ref[pl.ds(c, m, stride=ROW_CHUNKS), :] for c in range(ROW_CHUNKS)], axis=1)


def _dispatch_kernel(pstart_ref, pend_ref, dest_ref, x_ref, mod_ref, xb_ref, fbuf, zbuf, sem, zsem, *, n_ctx_tiles, bm):
    i = pl.program_id(0)
    tm = x_ref.shape[0]

    @pl.when(i == 0)
    def _():
        zbuf[...] = jnp.zeros_like(zbuf)

        def zcopy(e):
            row0 = pl.multiple_of((pend_ref[e] - bm) * ROW_CHUNKS, bm * ROW_CHUNKS)
            return pltpu.make_async_copy(zbuf, xb_ref.at[pl.ds(row0, bm * ROW_CHUNKS)], zsem)

        def zstart(e, c):
            @pl.when(pend_ref[e] > pstart_ref[e])
            def _():
                zcopy(e).start()
            return c

        def zwait(e, c):
            @pl.when(pend_ref[e] > pstart_ref[e])
            def _():
                zcopy(e).wait()
            return c

        lax.fori_loop(0, N_EXPERTS, zstart, 0)
        lax.fori_loop(0, N_EXPERTS, zwait, 0)

        def tcopy(b):
            row0 = pl.multiple_of(b * (bm * ROW_CHUNKS), bm * ROW_CHUNKS)
            return pltpu.make_async_copy(zbuf, xb_ref.at[pl.ds(row0, bm * ROW_CHUNKS)], zsem)

        def tstart(b, c):
            tcopy(b).start()
            return c

        def twait(b, c):
            tcopy(b).wait()
            return c

        tail0 = pend_ref[N_EXPERTS - 1] // bm
        n_blocks = xb_ref.shape[0] // (bm * ROW_CHUNKS)
        lax.fori_loop(tail0, n_blocks, tstart, 0)
        lax.fori_loop(tail0, n_blocks, twait, 0)

    is_ctx = i < n_ctx_tiles
    f = x_ref[...] * (1.0 + _mod_rows(mod_ref, 4, is_ctx)) + _mod_rows(mod_ref, 3, is_ctx)
    slot = i % 2
    _to_chunk_rows(fbuf.at[slot], f)

    def start(t, c):
        src = pl.multiple_of(t * ROW_CHUNKS, ROW_CHUNKS)
        for k in range(TOP_K):
            dst = pl.multiple_of(dest_ref[t * TOP_K + k] * ROW_CHUNKS, ROW_CHUNKS)
            pltpu.make_async_copy(fbuf.at[slot, pl.ds(src, ROW_CHUNKS)], xb_ref.at[pl.ds(dst, ROW_CHUNKS)],
                                  sem.at[slot]).start(priority=k % 2)
        return c

    lax.fori_loop(0, tm, start, 0, unroll=DMA_UNROLL)

    def wait_tile(s):
        for _ in range(TOP_K):
            pltpu.make_async_copy(fbuf.at[s], xb_ref.at[pl.ds(0, tm * ROW_CHUNKS)], sem.at[s]).wait()

    @pl.when(i > 0)
    def _():
        wait_tile(1 - slot)

    @pl.when(i == pl.num_programs(0) - 1)
    def _():
        wait_tile(slot)


def _dispatch(pad_start, pad_end, dest_flat, xs, mod, n_rows, n_ctx_tiles):
    t = xs.shape[0]
    tm = TOK_TILE
    gs = pltpu.PrefetchScalarGridSpec(
        num_scalar_prefetch=2,
        grid=(t // tm,),
        in_specs=[pl.BlockSpec((tm * TOP_K,), lambda i, *_: (i,), memory_space=pltpu.SMEM),
                  pl.BlockSpec((tm, D_MODEL), lambda i, *_: (i, 0)),
                  pl.BlockSpec(mod.shape, lambda i, *_: (0, 0))],
        out_specs=pl.BlockSpec(memory_space=pl.ANY),
        scratch_shapes=[pltpu.VMEM((2, tm * ROW_CHUNKS, LANES), F32), pltpu.VMEM((MOE_BM * ROW_CHUNKS, LANES), F32),
                        pltpu.SemaphoreType.DMA((2,)), pltpu.SemaphoreType.DMA],
    )
    return pl.pallas_call(
        functools.partial(_dispatch_kernel, n_ctx_tiles=n_ctx_tiles, bm=MOE_BM),
        out_shape=jax.ShapeDtypeStruct((n_rows * ROW_CHUNKS, LANES), F32),
        grid_spec=gs,
        compiler_params=_cparams(("arbitrary",)),
        name="moe_dispatch",
    )(pad_start, pad_end, dest_flat, xs, mod)


def _expert_kernel(bexp_ref, bvalid_ref, x_ref, wgu_ref, bgu_ref, wd_ref, bd_ref, y_ref, wgu_bf, wd_bf):
    i = pl.program_id(0)
    e = bexp_ref[i]
    prev = bexp_ref[jnp.maximum(i - 1, 0)]

    @pl.when(jnp.logical_or(i == 0, e != prev))
    def _():
        wgu_bf[...] = wgu_ref[...].astype(BF16)
        wd_bf[...] = wd_ref[...].astype(BF16)

    @pl.when(bvalid_ref[i] > 0)
    def _():
        bm = y_ref.shape[0] // ROW_CHUNKS
        x = _from_chunk_rows(x_ref, bm).astype(BF16)
        gu = jnp.dot(x, wgu_bf[...], preferred_element_type=F32) + bgu_ref[...]
        gate = jnp.minimum(gu[:, :D_EXPERT], SWIGLU_LIMIT)
        up = jnp.clip(gu[:, D_EXPERT:], -SWIGLU_LIMIT, SWIGLU_LIMIT)
        act = gate * jax.nn.sigmoid(SWIGLU_ALPHA * gate) * (up + 1.0)
        _to_chunk_rows(y_ref, jnp.dot(act.astype(BF16), wd_bf[...], preferred_element_type=F32) + bd_ref[...])

    @pl.when(bvalid_ref[i] == 0)
    def _():
        y_ref[...] = jnp.zeros_like(y_ref)


def _experts(block_exp, block_valid, xb, layer, w_gate_up, b_gate_up, w_down, b_down):
    n_rows = xb.shape[0] // ROW_CHUNKS
    bm = MOE_BM
    depth = w_gate_up.shape[0]
    bgu = b_gate_up.reshape(depth, N_EXPERTS, 1, 2 * D_EXPERT)
    bd = b_down.reshape(depth, N_EXPERTS, 1, D_MODEL)
    gs = pltpu.PrefetchScalarGridSpec(
        num_scalar_prefetch=2,
        grid=(n_rows // bm,),
        in_specs=[pl.BlockSpec((bm * ROW_CHUNKS, LANES), lambda i, be, bv: (i, 0)),
                  pl.BlockSpec((None, None, D_MODEL, 2 * D_EXPERT), lambda i, be, bv: (layer, be[i], 0, 0)),
                  pl.BlockSpec((None, None, 1, 2 * D_EXPERT), lambda i, be, bv: (layer, be[i], 0, 0)),
                  pl.BlockSpec((None, None, D_EXPERT, D_MODEL), lambda i, be, bv: (layer, be[i], 0, 0)),
                  pl.BlockSpec((None, None, 1, D_MODEL), lambda i, be, bv: (layer, be[i], 0, 0))],
        out_specs=pl.BlockSpec((bm * ROW_CHUNKS, LANES), lambda i, be, bv: (i, 0)),
        scratch_shapes=[pltpu.VMEM((D_MODEL, 2 * D_EXPERT), BF16), pltpu.VMEM((D_EXPERT, D_MODEL), BF16)],
    )
    return pl.pallas_call(
        _expert_kernel,
        out_shape=jax.ShapeDtypeStruct((n_rows * ROW_CHUNKS, LANES), F32),
        grid_spec=gs,
        compiler_params=_cparams(("arbitrary",)),
        name="moe_experts",
    )(block_exp, block_valid, xb, w_gate_up, bgu, w_down, bd)


def _combine_kernel(dest_ref, dnext_ref, gate_ref, x_ref, mod_ref, lw_ref, lb_ref, yb_ref, o_ref, ybuf, sem, *,
                    n_ctx_tiles, alpha):
    i = pl.program_id(0)
    tm = x_ref.shape[0]
    slot = i % 2

    def gather(d_ref, s):
        def start(t, c):
            dst = pl.multiple_of(t * ROW_CHUNKS, ROW_CHUNKS)
            for k in range(TOP_K):
                src = pl.multiple_of(d_ref[t * TOP_K + k] * ROW_CHUNKS, ROW_CHUNKS)
                pltpu.make_async_copy(yb_ref.at[pl.ds(src, ROW_CHUNKS)], ybuf.at[s, k, pl.ds(dst, ROW_CHUNKS)],
                                      sem.at[s]).start(priority=k % 2)
            return c

        lax.fori_loop(0, tm, start, 0, unroll=DMA_UNROLL)

    @pl.when(i == 0)
    def _():
        gather(dest_ref, 0)

    @pl.when(i + 1 < pl.num_programs(0))
    def _():
        gather(dnext_ref, 1 - slot)

    for k in range(TOP_K):
        pltpu.make_async_copy(yb_ref.at[pl.ds(0, tm * ROW_CHUNKS)], ybuf.at[slot, k], sem.at[slot]).wait()
    gates = gate_ref[...]
    y = gates[:, 0:1] * _from_chunk_rows(ybuf.at[slot, 0], tm)
    for k in range(1, TOP_K):
        y = y + gates[:, k:k + 1] * _from_chunk_rows(ybuf.at[slot, k], tm)
    is_ctx = i < n_ctx_tiles
    g2 = _mod_rows(mod_ref, 5, is_ctx)
    o_ref[...] = _layer_norm(alpha * x_ref[...] + g2 * y, lw_ref[...], lb_ref[...])


def _combine(dest_flat, gates, xs, mod, ln_w, ln_b, yb, n_ctx_tiles, alpha):
    t = xs.shape[0]
    tm = TOK_TILE
    n_tiles = t // tm
    lw2, lb2 = ln_w.reshape(1, -1), ln_b.reshape(1, -1)
    full = lambda a: pl.BlockSpec(a.shape, lambda i: (0,) * a.ndim)
    return pl.pallas_call(
        functools.partial(_combine_kernel, n_ctx_tiles=n_ctx_tiles, alpha=alpha),
        out_shape=jax.ShapeDtypeStruct((t, D_MODEL), F32),
        grid=(n_tiles,),
        in_specs=[pl.BlockSpec((tm * TOP_K,), lambda i: (i,), memory_space=pltpu.SMEM),
                  pl.BlockSpec((tm * TOP_K,), lambda i: (jnp.minimum(i + 1, n_tiles - 1),), memory_space=pltpu.SMEM),
                  pl.BlockSpec((tm, TOP_K), lambda i: (i, 0)),
                  pl.BlockSpec((tm, D_MODEL), lambda i: (i, 0)),
                  full(mod), full(lw2), full(lb2),
                  pl.BlockSpec(memory_space=pl.ANY)],
        out_specs=pl.BlockSpec((tm, D_MODEL), lambda i: (i, 0)),
        scratch_shapes=[pltpu.VMEM((2, TOP_K, tm * ROW_CHUNKS, LANES), F32), pltpu.SemaphoreType.DMA((2,))],
        compiler_params=_cparams(("arbitrary",)),
        name="moe_combine",
    )(dest_flat, dest_flat, gates, xs, mod, lw2, lb2, yb)


def _moe(xs, mod, layer, w_router, b_router, w_gate_up, b_gate_up, w_down, b_down, ln_w, ln_b, n_ctx_tiles, alpha):
    t = xs.shape[0]
    bm = MOE_BM
    idx, gates, rank, cnt = _router(xs, mod, w_router, b_router, n_ctx_tiles)
    counts = cnt[:, 0]
    idx, gates, rank = idx[:TOP_K].T, gates[:TOP_K].T, rank[:TOP_K].T
    padded = (counts + bm - 1) // bm * bm
    pad_end = jnp.cumsum(padded).astype(jnp.int32)
    pad_start = pad_end - padded
    dest = pad_start[idx] + rank
    dest_flat = dest.reshape(-1).astype(jnp.int32)
    n_blocks = -(-(t * TOP_K) // bm) + N_EXPERTS
    blk0 = jnp.arange(n_blocks, dtype=jnp.int32) * bm
    block_exp = jnp.minimum(jnp.sum(pad_end[None, :] <= blk0[:, None], axis=1), N_EXPERTS - 1).astype(jnp.int32)
    block_valid = (blk0 < pad_end[-1]).astype(jnp.int32)
    xb = _dispatch(pad_start, pad_end, dest_flat, xs, mod, n_blocks * bm, n_ctx_tiles)
    yb = _experts(block_exp, block_valid, xb, layer, w_gate_up, b_gate_up, w_down, b_down)
    return _combine(dest_flat, gates, xs, mod, ln_w, ln_b, yb, n_ctx_tiles, alpha)


def kernel(x, c, ctx, c_ctx, w_ada, b_ada, w_in, mlstm_gate_bias, diff_lambda, diff_norm_w, mlstm_norm_w, w_out,
           ln1_w, ln1_b, w_router, b_router, w_gate_up, b_gate_up, w_down, b_down, ln2_w, ln2_b):
    assert x.shape[0] == 1 and ctx.shape[0] == 1
    depth = w_ada.shape[0]
    n_lat, n_ctx = x.shape[1], ctx.shape[1]
    assert n_ctx % TOK_TILE == 0 and n_lat % TOK_TILE == 0
    nct = n_ctx // TOK_TILE
    alpha = (2 * depth) ** 0.25

    vecs = jnp.zeros((8, D_MODEL), F32).at[0].set(c[0]).at[1].set(c_ctx)
    mods = _ada(vecs, w_ada, b_ada)
    cos, sin = _rope_tables(n_lat, n_ctx)
    xt = jnp.concatenate([ctx[0], x[0]], axis=0)

    for l in range(depth):
        need_ctx = l < depth - 1
        lam_init = 0.8 - 0.6 * math.exp(-0.3 * l)
        mod = mods[l]
        w_ext, w_vt = _prep_w_in(w_in[l])
        q, k, vt, mkt, mq, mk, mv, mo, g, cu = _inproj(xt, mod, w_ext, w_vt, cos, sin, mlstm_gate_bias[l], n_ctx)

        dl = diff_lambda[l].astype(F32)
        lam = (jnp.exp(jnp.sum(dl[0] * dl[1])) - jnp.exp(jnp.sum(dl[2] * dl[3])) + lam_init).reshape(1)
        post = 1.0 - lam_init
        attn_l = _attention(lam, q, k, vt, diff_norm_w[l], n_ctx, n_lat, n_ctx + n_lat, post)
        four_l = _fourier_lat(cu[n_ctx:])
        if need_ctx:
            attn_c = _attention(lam, q, k, vt, diff_norm_w[l], 0, n_ctx, n_ctx, post)
            four_c = _fourier_small(cu[:n_ctx])
        else:
            attn_c, four_c = attn_l, four_l
        hf, hb = _mlstm(mq, mk, mkt, mv, g, n_ctx)
        xs = _outproj(attn_l, attn_c, four_l, four_c, hf, hb, mo, xt, mod, mlstm_norm_w[l], w_out[l].astype(BF16),
                      ln1_w[l], ln1_b[l], n_ctx, need_ctx, alpha)
        xt = _moe(xs, mod, l, w_router[l], b_router[l], w_gate_up, b_gate_up, w_down, b_down, ln2_w[l], ln2_b[l],
                  nct if need_ctx else 0, alpha)
    return xt[None]
```

```python
import functools
import math

import numpy as np
import jax
import jax.numpy as jnp
from jax import lax
from jax.experimental import pallas as pl
from jax.experimental.pallas import tpu as pltpu

F32 = jnp.float32
BF16 = jnp.bfloat16
HI = lax.Precision.HIGHEST

D_MODEL = 1024
GRID_W = 64
HEAD_DIM = 64
A_HEADS = 4
A_QK_DIM = HEAD_DIM
A_V_DIM = 2 * HEAD_DIM
A_WIDTH = A_HEADS * A_V_DIM
A_SCALE = A_QK_DIM ** -0.5
B_HEADS = 4
B_HEAD_DIM = HEAD_DIM
B_WIDTH = B_HEADS * B_HEAD_DIM
C_GROUPS = 4
C_GROUP_DIM = HEAD_DIM
C_WIDTH = C_GROUPS * C_GROUP_DIM
IN_SIZES = (A_WIDTH, A_WIDTH, A_WIDTH, B_WIDTH, B_WIDTH, B_WIDTH, B_WIDTH, 4 * B_HEADS, C_WIDTH)
ROPE_BASE = 10000.0
MLSTM_CHUNK = 64
N_EXPERTS = 32
TOP_K = 4
D_EXPERT = D_MODEL
SWIGLU_LIMIT = 7.0
SWIGLU_ALPHA = 1.702
EPS = 1e-5

LANES = 128
VMEM_LIMIT = 56 * 1024 * 1024
TOK_TILE = 256
ATT_TQ = 512
ATT_COLS = 256
ATT_UNROLL = 32
MOE_BM = 512

_OQ, _OK = 0, 512
_OMQ, _OMK, _OMV, _OMO, _OG, _OCU = 1024, 1280, 1536, 1792, 2048, 2176
Q_SCALE = A_SCALE * math.log2(math.e)
VT_ROWS = A_V_DIM + 16


def _cparams(sem, vmem=VMEM_LIMIT):
    return pltpu.CompilerParams(dimension_semantics=sem, vmem_limit_bytes=vmem)


def _ada_kernel(v_ref, w_ref, b_ref, o_ref):
    v = v_ref[...]
    s = v * jax.nn.sigmoid(v)
    o_ref[...] = jnp.dot(s, w_ref[...], precision=HI, preferred_element_type=F32) + b_ref[...]


def _ada(vecs, w_ada, b_ada):
    depth, d, d6 = w_ada.shape
    bn = 1024
    return pl.pallas_call(
        _ada_kernel,
        out_shape=jax.ShapeDtypeStruct((depth, 8, d6), F32),
        grid=(depth, d6 // bn),
        in_specs=[pl.BlockSpec((8, d), lambda l, j: (0, 0)),
                  pl.BlockSpec((None, d, bn), lambda l, j: (l, 0, j)),
                  pl.BlockSpec((None, 1, bn), lambda l, j: (l, 0, j))],
        out_specs=pl.BlockSpec((None, 8, bn), lambda l, j: (l, 0, j)),
        compiler_params=_cparams(("arbitrary", "arbitrary")),
        name="ada_modulation",
    )(vecs, w_ada, b_ada.reshape(depth, 1, d6))


def _mod_rows(mod_ref, k, is_ctx):
    lat = mod_ref[0:1, k * D_MODEL:(k + 1) * D_MODEL]
    ctx = mod_ref[1:2, k * D_MODEL:(k + 1) * D_MODEL]
    return jnp.where(is_ctx, ctx, lat)


def _inproj_kernel(xc_ref, xl_ref, mod_ref, w_ref, wvt_ref, cos_ref, sin_ref, gb_ref,
                   q_ref, k_ref, vt_ref, mkt_ref, mq_ref, mk_ref, mv_ref, mo_ref, g_ref, cu_ref, *, n_ctx_tiles):
    is_ctx = pl.program_id(0) < n_ctx_tiles
    sh = _mod_rows(mod_ref, 0, is_ctx)
    sc = _mod_rows(mod_ref, 1, is_ctx)
    x = jnp.where(is_ctx, xc_ref[...], xl_ref[...])
    hm = (x * (1.0 + sc) + sh).astype(BF16)

    def proj(off, width):
        return jnp.dot(hm, w_ref[:, off:off + width], preferred_element_type=F32)

    cos = cos_ref[...]
    sin = sin_ref[...]
    quarter = A_QK_DIM // 4
    first_half = (lax.broadcasted_iota(jnp.int32, cos.shape, 1) % (2 * quarter)) < quarter

    def rope(t):
        rot = jnp.where(first_half, -pltpu.roll(t, LANES - quarter, 1), pltpu.roll(t, quarter, 1))
        return t * cos + rot * sin

    for j in range(A_HEADS):
        o = j * LANES
        q_ref[:, o:o + LANES] = (rope(proj(_OQ + o, LANES)) * Q_SCALE).astype(BF16)
        k_ref[:, o:o + LANES] = rope(proj(_OK + o, LANES)).astype(BF16)
    tr = lax.dot_general(wvt_ref[...], hm, (((1,), (1,)), ((), ())), preferred_element_type=F32)
    mkt_ref[...] = tr[A_WIDTH:, :] * (B_HEAD_DIM ** -0.5)
    vt = tr[:A_WIDTH, :].astype(BF16)
    for j in range(A_HEADS):
        vt_ref[j, 0:A_V_DIM, :] = vt[j * A_V_DIM:(j + 1) * A_V_DIM, :]
        vt_ref[j, A_V_DIM:, :] = jnp.ones((VT_ROWS - A_V_DIM, vt.shape[1]), BF16)
    mq_ref[...] = proj(_OMQ, B_WIDTH)
    mk_ref[...] = proj(_OMK, B_WIDTH) * (B_HEAD_DIM ** -0.5)
    mv_ref[...] = proj(_OMV, B_WIDTH)
    mo_ref[...] = proj(_OMO, B_WIDTH)
    g_ref[...] = proj(_OG, LANES) + gb_ref[...]
    cu_ref[...] = proj(_OCU, C_WIDTH)


def _prep_w_in(w_in):
    offs = np.cumsum((0,) + IN_SIZES)
    aq, ak, av, bq, bk, bv, bo, bg, cu = [w_in[:, offs[i]:offs[i + 1]] for i in range(9)]
    bg = jnp.pad(bg, ((0, 0), (0, LANES - bg.shape[1])))
    w = jnp.concatenate([aq, ak, bq, bk, bv, bo, bg, cu], axis=1)
    return w.astype(BF16), jnp.concatenate([av, bk], axis=1).T.astype(BF16)


def _rope_tables(n_lat, n_ctx):
    rows = n_lat // GRID_W
    axis_dim = A_QK_DIM // 2
    inv_freq = ROPE_BASE ** (-np.arange(0, axis_dim, 2, dtype=np.float64) / axis_dim)
    ang_r = np.repeat(np.arange(rows, dtype=np.float64), GRID_W)[:, None] * inv_freq
    ang_c = np.tile(np.arange(GRID_W, dtype=np.float64), rows)[:, None] * inv_freq
    ang = np.concatenate([ang_r, ang_r, ang_c, ang_c] * 2, axis=-1)
    cos = np.concatenate([np.ones((n_ctx, 2 * A_QK_DIM)), np.cos(ang)], axis=0)
    sin = np.concatenate([np.zeros((n_ctx, 2 * A_QK_DIM)), np.sin(ang)], axis=0)
    return jnp.asarray(cos.astype(np.float32)), jnp.asarray(sin.astype(np.float32))


def _inproj(xc, xl, lat_row0, t, mod, w_ext, w_vt, cos, sin, gate_bias, n_ctx):
    tm = TOK_TILE
    nct, lat_blk0 = n_ctx // tm, lat_row0 // tm
    ctx_rows = pl.BlockSpec((tm, D_MODEL), lambda i: (jnp.minimum(i, nct - 1), 0))
    lat_rows = pl.BlockSpec((tm, D_MODEL), lambda i: (jnp.maximum(i - nct, 0) + lat_blk0, 0))
    gb = jnp.pad(gate_bias.reshape(1, -1), ((0, 0), (0, LANES - 4 * B_HEADS)))
    row = lambda w: pl.BlockSpec((tm, w), lambda i: (i, 0))
    full = lambda a: pl.BlockSpec(a.shape, lambda i: (0,) * a.ndim)
    sds = jax.ShapeDtypeStruct
    rows_out = [(A_WIDTH, BF16)] * 2 + [(B_WIDTH, F32)] * 4 + [(LANES, F32), (C_WIDTH, F32)]
    out_shape = [sds((t, w), dt) for w, dt in rows_out]
    out_specs = [row(w) for w, _ in rows_out]
    out_shape.insert(2, sds((t // tm, A_HEADS, VT_ROWS, tm), BF16))
    out_specs.insert(2, pl.BlockSpec((None, A_HEADS, VT_ROWS, tm), lambda i: (i, 0, 0, 0)))
    out_shape.insert(3, sds((t // tm, B_WIDTH, tm), F32))
    out_specs.insert(3, pl.BlockSpec((None, B_WIDTH, tm), lambda i: (i, 0, 0)))
    return pl.pallas_call(
        functools.partial(_inproj_kernel, n_ctx_tiles=n_ctx // tm),
        out_shape=out_shape,
        grid=(t // tm,),
        in_specs=[ctx_rows, lat_rows, full(mod), full(w_ext), full(w_vt), row(LANES), row(LANES), full(gb)],
        out_specs=out_specs,
        compiler_params=_cparams(("arbitrary",)),
        name="in_projection",
    )(xc, xl, mod, w_ext, w_vt, cos, sin, gb)


def _attn_kernel(lam_ref, *refs, n_q_sub, n_kv, post_scale):
    q_refs, (k_ref, vt_ref, nw_ref, o_ref, acc_ref, s_ref) = refs[:n_q_sub], refs[n_q_sub:]
    q = jnp.concatenate([r[...] for r in q_refs], axis=0) if n_q_sub > 1 else q_refs[0][...]
    tq = q.shape[0]
    tk = vt_ref.shape[-1]
    lane = lax.broadcasted_iota(jnp.int32, q.shape, 1)
    zero = jnp.zeros_like(q)
    q2 = jnp.concatenate([jnp.where(lane < A_QK_DIM, q, zero), jnp.where(lane >= A_QK_DIM, q, zero)], axis=0)
    acc_ref[...] = jnp.zeros_like(acc_ref)

    def scores(j, slot):
        kb = k_ref[pl.ds(pl.multiple_of(j * tk, tk), tk), :]
        st = lax.dot_general(kb, q2, (((1,), (1,)), ((), ())), preferred_element_type=F32)
        s_ref[slot] = st
        return jnp.max(st, axis=0, keepdims=True)

    def consume(j, slot, cmax, m):
        m_new = jnp.maximum(m, cmax)
        alpha = jnp.exp2(m - m_new)
        for h in range(2 * tq // ATT_COLS):
            cs = slice(h * ATT_COLS, (h + 1) * ATT_COLS)
            p = jnp.exp2(s_ref[slot, :, cs] - m_new[:, cs]).astype(BF16)
            r = jnp.dot(vt_ref[j], p, preferred_element_type=F32)
            acc_ref[:, cs] = alpha[:, cs] * acc_ref[:, cs] + r
        return m_new

    cm0 = scores(0, 0)

    unroll = math.gcd(n_kv - 1, ATT_UNROLL)

    def group(t, carry):
        cm, m = carry
        a = unroll * t
        for i in range(unroll):
            cm_next = scores(a + i + 1, (i + 1) % 2)
            m = consume(a + i, i % 2, cm, m)
            cm = cm_next
        return cm, m

    assert unroll % 2 == 0
    cm_last, m = lax.fori_loop(0, (n_kv - 1) // unroll, group, (cm0, jnp.full((1, 2 * tq), -jnp.inf, F32)))
    consume(n_kv - 1, 0, cm_last, m)
    l = acc_ref[A_V_DIM:A_V_DIM + 1, :]
    acc = acc_ref[0:A_V_DIM, :]
    ot = acc[:, :tq] * (1.0 / l[:, :tq]) - acc[:, tq:] * (lam_ref[0] / l[:, tq:])
    yt = ot * lax.rsqrt(jnp.mean(ot * ot, axis=0, keepdims=True) + EPS)
    o_ref[...] = yt.T * nw_ref[...] * post_scale


def _attention(lam, q, k, vt, norm_w, q_row0, n_q, n_keys, post_scale):
    tq = min(ATT_TQ, n_q)
    tk = vt.shape[-1]
    n_kv = n_keys // tk
    sub = math.gcd(tq, q_row0) if q_row0 else tq
    n_sub = tq // sub
    q_specs = [pl.BlockSpec((sub, LANES), lambda h, i, j=j: (i * n_sub + j + q_row0 // sub, h))
               for j in range(n_sub)]
    return pl.pallas_call(
        functools.partial(_attn_kernel, n_q_sub=n_sub, n_kv=n_kv, post_scale=post_scale),
        out_shape=jax.ShapeDtypeStruct((n_q, A_WIDTH), F32),
        grid=(A_HEADS, n_q // tq),
        in_specs=[pl.BlockSpec(memory_space=pltpu.SMEM)] + q_specs + [
                  pl.BlockSpec((n_keys, LANES), lambda h, i: (0, h)),
                  pl.BlockSpec((n_kv, None, VT_ROWS, tk), lambda h, i: (0, h, 0, 0)),
                  pl.BlockSpec((1, LANES), lambda h, i: (0, h))],
        out_specs=pl.BlockSpec((tq, LANES), lambda h, i: (i, h)),
        scratch_shapes=[pltpu.VMEM((VT_ROWS, 2 * tq), F32), pltpu.VMEM((2, tk, 2 * tq), F32)],
        compiler_params=_cparams(("arbitrary", "arbitrary")),
        name="diff_attention",
    )(lam, *([q] * n_sub), k, vt, norm_w.reshape(1, A_WIDTH))


def _mlstm_kernel(qf, kf, ktf, vf, gf, qb, kb, ktb, vb, gb, of_ref, ob_ref, s_ref, m_ref, *, chunks):
    L = MLSTM_CHUNK
    tm = chunks * L
    neg = -jnp.inf

    @pl.when(pl.program_id(0) == 0)
    def _():
        s_ref[...] = jnp.zeros_like(s_ref)
        m_ref[...] = jnp.zeros_like(m_ref)

    tr = lax.broadcasted_iota(jnp.int32, (tm, tm), 0)
    tc = lax.broadcasted_iota(jnp.int32, (tm, tm), 1)
    same_chunk = tr // L == tc // L
    masks = (same_chunk & (tr >= tc), same_chunk & (tr <= tc))
    row_chunk = lax.broadcasted_iota(jnp.int32, (tm, 1), 0) // L
    col_chunk = lax.broadcasted_iota(jnp.int32, (1, tm), 1) // L
    lane_half = lax.broadcasted_iota(jnp.int32, (tm, LANES), 1) // B_HEAD_DIM
    row_half = lax.broadcasted_iota(jnp.int32, (LANES, 1), 0) // B_HEAD_DIM
    rr = lax.broadcasted_iota(jnp.int32, (LANES, 2 * LANES), 0) // B_HEAD_DIM
    cc = lax.broadcasted_iota(jnp.int32, (LANES, 2 * LANES), 1)
    blockmask = jnp.where(cc < LANES, cc // B_HEAD_DIM, cc - LANES) == rr
    aug = jnp.where(lax.broadcasted_iota(jnp.int32, (tm, LANES), 1) < 2, 1.0, 0.0).astype(F32)
    io = ((qf, kf, ktf, vf, gf, of_ref), (qb, kb, ktb, vb, gb, ob_ref))

    def by_chunk(vals, chunk_ids):
        out = vals[chunks - 1]
        for c in range(chunks - 2, -1, -1):
            out = jnp.where(chunk_ids == c, vals[c], out)
        return out

    dp = [(d, p) for d in range(2) for p in range(2)]
    dph = [(d, p, hh) for d, p in dp for hh in range(2)]

    g_tile, b_tile, g_t, b_t = {}, {}, {}, {}
    for d in range(2):
        g_tile[d] = io[d][4][...]
        b_tile[d] = jnp.dot(masks[d].astype(F32), jax.nn.log_sigmoid(g_tile[d]), precision=HI,
                            preferred_element_type=F32)
        g_t[d] = g_tile[d].T
        b_t[d] = b_tile[d].T
    qp, kp, v_aug, ktp = {}, {}, {}, {}
    for d, p in dp:
        sl = slice(p * LANES, (p + 1) * LANES)
        qp[d, p] = io[d][0][:, sl]
        kp[d, p] = io[d][1][:, sl].astype(BF16)
        ktp[d, p] = io[d][2][sl, :]
        v_aug[d, p] = jnp.concatenate([io[d][3][:, sl], aug], axis=1).astype(BF16)

    b_col, b_row, i_row, gs, a_row, m_loc, w_t = {}, {}, {}, {}, {}, {}, {}
    for d, p, hh in dph:
        h = 2 * p + hh
        il, fl = 4 * d + h, 8 + 4 * d + h
        last = L - 1 if d == 0 else 0
        b_col[d, p, hh] = b_tile[d][:, fl:fl + 1]
        b_row[d, p, hh] = b_t[d][fl:fl + 1, :]
        i_row[d, p, hh] = g_t[d][il:il + 1, :]
        gs[d, p, hh] = [b_row[d, p, hh][:, c * L + last:c * L + last + 1] for c in range(chunks)]
        a_row[d, p, hh] = by_chunk(gs[d, p, hh], col_chunk) - b_row[d, p, hh] + i_row[d, p, hh]
    for k3 in dph:
        m_loc[k3] = [jnp.max(jnp.where(col_chunk == c, a_row[k3], neg), axis=-1, keepdims=True)
                     for c in range(chunks)]
    for k3 in dph:
        w_t[k3] = jnp.exp(a_row[k3] - by_chunk(m_loc[k3], col_chunk))

    upd, sqk = {}, {}
    for d, p in dp:
        kw_t = ktp[d, p] * jnp.where(row_half == 0, w_t[d, p, 0], w_t[d, p, 1])
        for c in range(chunks):
            kw_c = jnp.where(col_chunk == c, kw_t, 0.0).astype(BF16)
            upd[d, p, c] = jnp.where(blockmask, jnp.dot(kw_c, v_aug[d, p], preferred_element_type=F32), 0.0)
    for d, p, hh in dph:
        qm = jnp.where(lane_half == hh, qp[d, p], 0.0).astype(BF16)
        sqk[d, p, hh] = lax.dot_general(qm, kp[d, p], (((1,), (1,)), ((), ())), preferred_element_type=F32)

    m_old = m_ref[...]
    orders = (range(chunks), range(chunks - 1, -1, -1))
    m_at, dec, inj, m_end = {}, {}, {}, {}
    for d, p, hh in dph:
        row = 4 * d + 2 * p + hh
        m_prev = m_old[row:row + 1, 0:1]
        for c in orders[d]:
            m_at[d, p, hh, c] = m_prev
            g_c = gs[d, p, hh][c]
            m_new = jnp.maximum(g_c + m_prev, m_loc[d, p, hh][c])
            dec[d, p, hh, c] = jnp.exp(g_c + m_prev - m_new)
            inj[d, p, hh, c] = jnp.exp(m_loc[d, p, hh][c] - m_new)
            m_prev = m_new
        m_end[row] = m_prev
    state_at, state_end = {}, {}
    for d, p in dp:
        state = s_ref[d, p]
        for c in orders[d]:
            state_at[d, p, c] = state
            state = (jnp.where(rr == 0, dec[d, p, 0, c], dec[d, p, 1, c]) * state
                     + jnp.where(rr == 0, inj[d, p, 0, c], inj[d, p, 1, c]) * upd[d, p, c])
        state_end[d, p] = state
    for d, p in dp:
        s_ref[d, p] = state_end[d, p]
    m_rows = lax.broadcasted_iota(jnp.int32, m_old.shape, 0)
    m_out = m_old
    for row, val in m_end.items():
        m_out = jnp.where(m_rows == row, val, m_out)
    m_ref[...] = m_out

    dmat = {k3: jnp.where(masks[k3[0]], b_col[k3] + (i_row[k3] - b_row[k3]), neg) for k3 in dph}
    m_row = {k3: jnp.max(dmat[k3], axis=-1, keepdims=True) for k3 in dph}
    s_loc = {k3: (sqk[k3] * jnp.exp(dmat[k3] - m_row[k3])).astype(BF16) for k3 in dph}

    r = {}
    for d, p, hh in dph:
        r[d, p, hh] = jnp.dot(s_loc[d, p, hh], v_aug[d, p], preferred_element_type=F32)
    q2 = {}
    for d, p in dp:
        q2[d, p] = jnp.concatenate(
            [jnp.dot(qp[d, p][c * L:(c + 1) * L].astype(BF16), state_at[d, p, c].astype(BF16),
                     preferred_element_type=F32) for c in range(chunks)], axis=0)

    inter = {k3: b_col[k3] + by_chunk([m_at[k3 + (c,)] for c in range(chunks)], row_chunk) for k3 in dph}
    m_col = {k3: jnp.maximum(inter[k3], m_row[k3]) for k3 in dph}
    e_loc = {k3: jnp.exp(m_row[k3] - m_col[k3]) for k3 in dph}
    e_int = {k3: jnp.exp(inter[k3] - m_col[k3]) for k3 in dph}
    floor = {k3: jnp.exp(-m_col[k3]) for k3 in dph}
    den = {(d, p, hh): e_loc[d, p, hh] * r[d, p, hh][:, LANES:LANES + 1]
           + e_int[d, p, hh] * q2[d, p][:, LANES + hh:LANES + hh + 1] for d, p, hh in dph}
    inv = {k3: 1.0 / jnp.maximum(jnp.abs(den[k3]), floor[k3]) for k3 in dph}
    a_loc = {k3: e_loc[k3] * inv[k3] for k3 in dph}
    a_int = {k3: e_int[k3] * inv[k3] for k3 in dph}
    head_out = {(d, p, hh): a_loc[d, p, hh] * r[d, p, hh][:, :LANES] + a_int[d, p, hh] * q2[d, p][:, :LANES]
                for d, p, hh in dph}
    for d in range(2):
        io[d][5][...] = jnp.concatenate(
            [jnp.where(lane_half == 0, head_out[d, p, 0], head_out[d, p, 1]) for p in range(2)], axis=1)


def _mlstm(mq, mk, mkt, mv, g, n_ctx):
    t = mq.shape[0]
    tm = TOK_TILE
    n_tiles = t // tm
    n_ctx_tiles = n_ctx // tm
    mirror = lambda i: jnp.where(i < n_ctx_tiles, n_ctx_tiles - 1 - i, n_tiles - 1 - (i - n_ctx_tiles))
    fwd = lambda i: (i, 0)
    bwd = lambda i: (mirror(i), 0)
    spec = lambda w, im: pl.BlockSpec((tm, w), im)
    tspec = lambda f: pl.BlockSpec((None, B_WIDTH, tm), lambda i: (f(i), 0, 0))
    ins = ([spec(B_WIDTH, fwd)] * 2 + [tspec(lambda i: i), spec(B_WIDTH, fwd), spec(LANES, fwd)]
           + [spec(B_WIDTH, bwd)] * 2 + [tspec(mirror), spec(B_WIDTH, bwd), spec(LANES, bwd)])
    return pl.pallas_call(
        functools.partial(_mlstm_kernel, chunks=tm // MLSTM_CHUNK),
        out_shape=[jax.ShapeDtypeStruct((t, B_WIDTH), F32)] * 2,
        grid=(n_tiles,),
        in_specs=ins,
        out_specs=[spec(B_WIDTH, fwd), spec(B_WIDTH, bwd)],
        scratch_shapes=[pltpu.VMEM((2, 2, LANES, 2 * LANES), F32), pltpu.VMEM((8, LANES), F32)],
        compiler_params=_cparams(("arbitrary",)),
        name="mlstm_scan",
    )(mq, mk, mkt, mv, g, mq, mk, mkt, mv, g)


def _dft_cs(n):
    j = np.arange(n, dtype=np.int64)
    ang = 2.0 * np.pi * ((j[:, None] * j[None, :]) % n).astype(np.float64) / n
    s = 1.0 / math.sqrt(n)
    return np.cos(ang) * s, np.sin(ang) * s


def _chan_mats():
    c, s = _dft_cs(C_GROUP_DIM)
    eye = np.eye(C_GROUPS)
    return np.concatenate([np.kron(eye, c), np.kron(eye, s)], axis=0).astype(np.float32)


def _split_bf16(x):
    hi = x.astype(BF16)
    return hi, (x - hi.astype(F32)).astype(BF16)


def _split_const(a):
    hi, lo = _split_bf16(jnp.asarray(a, F32))
    return jnp.stack([hi, lo])


def _dot3(a_hi, a_lo, b_hi, b_lo):
    dot = functools.partial(jnp.dot, preferred_element_type=F32)
    return dot(a_hi, b_hi) + dot(a_hi, b_lo) + dot(a_lo, b_hi)


def _fft1_kernel(fa_ref, u_ref, z_ref):
    z_ref[...] = _dot3(fa_ref[0], fa_ref[1], *_split_bf16(u_ref[...]))


def _fft2_kernel(z_ref, twc_ref, tws_ref, fb_ref, ch_ref, o_ref, *, kb):
    n2 = fb_ref.shape[1] // 2
    for j in range(kb):
        zr = z_ref[0, j]
        zi = z_ref[1, j]
        tc = twc_ref[j]
        ts = tws_ref[j]
        st = jnp.concatenate([zr * tc + zi * ts, zi * tc - zr * ts], axis=0)
        a = _dot3(fb_ref[0], fb_ref[1], *_split_bf16(st))
        ari_hi, ari_lo = _split_bf16(jnp.concatenate([a[:n2], a[n2:]], axis=1))
        o_ref[:, j * C_WIDTH:(j + 1) * C_WIDTH] = _dot3(ari_hi, ari_lo, ch_ref[0], ch_ref[1])


def _fourier_lat(u):
    n = u.shape[0]
    n1 = int(round(math.sqrt(n)))
    assert n1 * n1 == n
    n2 = n1
    c, s = _dft_cs(n1)
    fa = _split_const(np.concatenate([c, -s], axis=0))
    fb = _split_const(np.block([[c, s], [-s, c]]))
    k1 = np.arange(n1, dtype=np.int64)
    ang = 2.0 * np.pi * ((k1[:, None] * k1[None, :]) % n).astype(np.float64) / n
    twc = jnp.asarray(np.cos(ang).astype(np.float32)).reshape(n1, n2, 1)
    tws = jnp.asarray(np.sin(ang).astype(np.float32)).reshape(n1, n2, 1)
    ch = _split_const(_chan_mats())
    cols = n2 * C_WIDTH
    tb = min(2048, cols)
    z = pl.pallas_call(
        _fft1_kernel,
        out_shape=jax.ShapeDtypeStruct((2 * n1, cols), F32),
        grid=(cols // tb,),
        in_specs=[pl.BlockSpec((2, 2 * n1, n1), lambda i: (0, 0, 0)), pl.BlockSpec((n1, tb), lambda i: (0, i))],
        out_specs=pl.BlockSpec((2 * n1, tb), lambda i: (0, i)),
        compiler_params=_cparams(("arbitrary",)),
        name="fourier_stage1",
    )(fa, u.reshape(n1, cols))
    kb = 8
    out = pl.pallas_call(
        functools.partial(_fft2_kernel, kb=kb),
        out_shape=jax.ShapeDtypeStruct((n2, n1 * C_WIDTH), F32),
        grid=(n1 // kb,),
        in_specs=[pl.BlockSpec((2, kb, n2, C_WIDTH), lambda i: (0, i, 0, 0)),
                  pl.BlockSpec((kb, n2, 1), lambda i: (i, 0, 0)),
                  pl.BlockSpec((kb, n2, 1), lambda i: (i, 0, 0)),
                  pl.BlockSpec((2, 2 * n2, 2 * n2), lambda i: (0, 0, 0)),
                  pl.BlockSpec((2, 2 * C_WIDTH, C_WIDTH), lambda i: (0, 0, 0))],
        out_specs=pl.BlockSpec((n2, kb * C_WIDTH), lambda i: (0, i)),
        compiler_params=_cparams(("arbitrary",)),
        name="fourier_stage2",
    )(z.reshape(2, n1, n2, C_WIDTH), twc, tws, fb, ch)
    return out.reshape(n, C_WIDTH)


def _fft_small_kernel(fa_ref, u_ref, ch_ref, o_ref):
    n = u_ref.shape[0]
    a = jnp.dot(fa_ref[...], u_ref[...], precision=HI, preferred_element_type=F32)
    ari = jnp.concatenate([a[:n], a[n:]], axis=1)
    o_ref[...] = jnp.dot(ari, ch_ref[...], precision=HI, preferred_element_type=F32)


def _fourier_small(u):
    n = u.shape[0]
    c, s = _dft_cs(n)
    fa = jnp.asarray(np.concatenate([c, -s], axis=0).astype(np.float32))
    return pl.pallas_call(
        _fft_small_kernel,
        out_shape=jax.ShapeDtypeStruct((n, C_WIDTH), F32),
        compiler_params=_cparams(None),
        name="fourier_small",
    )(fa, u, jnp.asarray(_chan_mats()))


def _layer_norm(z, w, b):
    mu = jnp.mean(z, axis=-1, keepdims=True)
    zc = z - mu
    var = jnp.mean(zc * zc, axis=-1, keepdims=True)
    return zc * lax.rsqrt(var + EPS) * w + b


def _outproj_kernel(al_ref, ac_ref, fl_ref, fc_ref, hf_ref, hb_ref, bo_ref, xc_ref, xl_ref, mod_ref, mnw_ref, seg_ref,
                    w_ref, lw_ref, lb_ref, o_ref, *, n_ctx_tiles, alpha):
    is_ctx = pl.program_id(0) < n_ctx_tiles
    attn = jnp.where(is_ctx, ac_ref[...], al_ref[...])
    four = jnp.where(is_ctx, fc_ref[...], fl_ref[...])
    h = hf_ref[...] + hb_ref[...]
    ms = jnp.dot(h * h, seg_ref[...], precision=HI, preferred_element_type=F32)
    y = h * lax.rsqrt(ms + EPS) * mnw_ref[...] * jax.nn.sigmoid(bo_ref[...])
    mix = (jnp.dot(attn.astype(BF16), w_ref[0:A_WIDTH, :], preferred_element_type=F32)
           + jnp.dot(y.astype(BF16), w_ref[A_WIDTH:A_WIDTH + B_WIDTH, :], preferred_element_type=F32)
           + jnp.dot(four.astype(BF16), w_ref[A_WIDTH + B_WIDTH:, :], preferred_element_type=F32))
    g1 = _mod_rows(mod_ref, 2, is_ctx)
    x = jnp.where(is_ctx, xc_ref[...], xl_ref[...])
    o_ref[...] = _layer_norm(alpha * x + g1 * mix, lw_ref[...], lb_ref[...])


def _outproj(attn_l, attn_c, four_l, four_c, hf, hb, bo, xc, xl, lat_row0, mod, mnw, w_out_bf, ln_w, ln_b, n_ctx,
             with_ctx, alpha):
    tm = TOK_TILE
    nct = n_ctx // tm
    n_lat_tiles = attn_l.shape[0] // tm
    if with_ctx:
        n_tiles, uoff, n_ctx_tiles = n_lat_tiles + nct, 0, nct
        lat = lambda i: (jnp.maximum(i - nct, 0), 0)
    else:
        n_tiles, uoff, n_ctx_tiles = n_lat_tiles, nct, 0
        lat = lambda i: (i, 0)
    uni = lambda i: (i + uoff, 0)
    ctxm = lambda i: (jnp.minimum(i, nct - 1), 0)
    lat_blk0 = lat_row0 // tm
    latx = lambda i: (lat(i)[0] + lat_blk0, 0)
    seg = jnp.asarray(np.kron(np.eye(B_HEADS), np.full((B_HEAD_DIM, B_HEAD_DIM), 1.0 / B_HEAD_DIM)).astype(np.float32))
    full = lambda a: pl.BlockSpec(a.shape, lambda i: (0,) * a.ndim)
    mnw2, lw2, lb2 = mnw.reshape(1, -1), ln_w.reshape(1, -1), ln_b.reshape(1, -1)
    return pl.pallas_call(
        functools.partial(_outproj_kernel, n_ctx_tiles=n_ctx_tiles, alpha=alpha),
        out_shape=jax.ShapeDtypeStruct((n_tiles * tm, D_MODEL), F32),
        grid=(n_tiles,),
        in_specs=[pl.BlockSpec((tm, A_WIDTH), lat), pl.BlockSpec((tm, A_WIDTH), ctxm),
                  pl.BlockSpec((tm, C_WIDTH), lat), pl.BlockSpec((tm, C_WIDTH), ctxm),
                  pl.BlockSpec((tm, B_WIDTH), uni), pl.BlockSpec((tm, B_WIDTH), uni),
                  pl.BlockSpec((tm, B_WIDTH), uni), pl.BlockSpec((tm, D_MODEL), ctxm), pl.BlockSpec((tm, D_MODEL), latx),
                  full(mod), full(mnw2), full(seg), full(w_out_bf), full(lw2), full(lb2)],
        out_specs=pl.BlockSpec((tm, D_MODEL), lambda i: (i, 0)),
        compiler_params=_cparams(("arbitrary",)),
        name="out_projection",
    )(attn_l, attn_c, four_l, four_c, hf, hb, bo, xc, xl, mod, mnw2, seg, w_out_bf, lw2, lb2)


def _router_kernel(x_ref, mod_ref, wrt_ref, br_ref, idx_ref, gate_ref, rank_ref, cnt_ref, run_ref, *, n_ctx_tiles):
    i = pl.program_id(0)

    @pl.when(i == 0)
    def _():
        run_ref[...] = jnp.zeros_like(run_ref)

    is_ctx = i < n_ctx_tiles
    f_in = x_ref[...] * (1.0 + _mod_rows(mod_ref, 4, is_ctx)) + _mod_rows(mod_ref, 3, is_ctx)
    logits = lax.dot_general(wrt_ref[...], f_in, (((1,), (1,)), ((), ())), precision=HI,
                             preferred_element_type=F32) + br_ref[...]
    tm = logits.shape[1]
    eidx = lax.broadcasted_iota(jnp.int32, logits.shape, 0).astype(F32)
    work = logits
    vals, sels = [], []
    for _ in range(TOP_K):
        mx = jnp.max(work, axis=0, keepdims=True)
        sel = jnp.min(jnp.where(work == mx, eidx, float(N_EXPERTS)), axis=0, keepdims=True)
        vals.append(mx)
        sels.append(sel)
        work = jnp.where(eidx == sel, -jnp.inf, work)
    es = [jnp.exp(v - vals[0]) for v in vals]
    tot = es[0] + es[1] + es[2] + es[3]
    onehots = [(eidx == s).astype(F32) for s in sels]
    oh_all = onehots[0] + onehots[1] + onehots[2] + onehots[3]
    r = lax.broadcasted_iota(jnp.int32, (tm, tm), 0)
    c = lax.broadcasted_iota(jnp.int32, (tm, tm), 1)
    before = jnp.dot(oh_all.astype(BF16), (r < c).astype(BF16), preferred_element_type=F32) + run_ref[:, 0:1]
    krow = lax.broadcasted_iota(jnp.int32, (8, tm), 0)
    idx_o = jnp.zeros((8, tm), jnp.int32)
    gate_o = jnp.zeros((8, tm), F32)
    rank_o = jnp.zeros((8, tm), jnp.int32)
    for k in range(TOP_K):
        rank = jnp.sum(onehots[k] * before, axis=0, keepdims=True).astype(jnp.int32)
        idx_o = jnp.where(krow == k, sels[k].astype(jnp.int32), idx_o)
        gate_o = jnp.where(krow == k, es[k] / tot, gate_o)
        rank_o = jnp.where(krow == k, rank, rank_o)
    idx_ref[...] = idx_o
    gate_ref[...] = gate_o
    rank_ref[...] = rank_o
    run_new = run_ref[...] + jnp.sum(oh_all, axis=1, keepdims=True)
    run_ref[...] = run_new
    cnt_ref[...] = run_new.astype(jnp.int32)


def _router(xs, mod, w_router, b_router, n_ctx_tiles):
    t = xs.shape[0]
    tm = TOK_TILE
    wrt = w_router.T
    br = b_router.reshape(-1, 1)
    full = lambda a: pl.BlockSpec(a.shape, lambda i: (0,) * a.ndim)
    col = pl.BlockSpec((8, tm), lambda i: (0, i))
    return pl.pallas_call(
        functools.partial(_router_kernel, n_ctx_tiles=n_ctx_tiles),
        out_shape=[jax.ShapeDtypeStruct((8, t), jnp.int32), jax.ShapeDtypeStruct((8, t), F32),
                   jax.ShapeDtypeStruct((8, t), jnp.int32), jax.ShapeDtypeStruct((N_EXPERTS, LANES), jnp.int32)],
        grid=(t // tm,),
        in_specs=[pl.BlockSpec((tm, D_MODEL), lambda i: (i, 0)), full(mod), full(wrt), full(br)],
        out_specs=[col, col, col, pl.BlockSpec((N_EXPERTS, LANES), lambda i: (0, 0))],
        scratch_shapes=[pltpu.VMEM((N_EXPERTS, LANES), F32)],
        compiler_params=_cparams(("arbitrary",)),
        name="moe_router",
    )(xs, mod, wrt, br)


ROW_CHUNKS = D_MODEL // LANES
DMA_UNROLL = 8


def _to_chunk_rows(ref, val):
    m = val.shape[0]
    for c in range(ROW_CHUNKS):
        ref[pl.ds(c, m, stride=ROW_CHUNKS), :] = val[:, c * LANES:(c + 1) * LANES]


def _from_chunk_rows(ref, m):
    return jnp.concatenate([ref[pl.ds(c, m, stride=ROW_CHUNKS), :] for c in range(ROW_CHUNKS)], axis=1)


def _dispatch_kernel(pstart_ref, pend_ref, idx_ref, rank_ref, x_ref, mod_ref, xb_ref, fbuf, zbuf, sem, zsem, *,
                     n_ctx_tiles, bm):
    i = pl.program_id(0)
    tm = x_ref.shape[0]

    @pl.when(i == 0)
    def _():
        zbuf[...] = jnp.zeros_like(zbuf)

        def zcopy(e):
            row0 = pl.multiple_of((pend_ref[e] - bm) * ROW_CHUNKS, bm * ROW_CHUNKS)
            return pltpu.make_async_copy(zbuf, xb_ref.at[pl.ds(row0, bm * ROW_CHUNKS)], zsem)

        def zstart(e, c):
            @pl.when(pend_ref[e] > pstart_ref[e])
            def _():
                zcopy(e).start()
            return c

        def zwait(e, c):
            @pl.when(pend_ref[e] > pstart_ref[e])
            def _():
                zcopy(e).wait()
            return c

        lax.fori_loop(0, N_EXPERTS, zstart, 0)
        lax.fori_loop(0, N_EXPERTS, zwait, 0)

        def tcopy(b):
            row0 = pl.multiple_of(b * (bm * ROW_CHUNKS), bm * ROW_CHUNKS)
            return pltpu.make_async_copy(zbuf, xb_ref.at[pl.ds(row0, bm * ROW_CHUNKS)], zsem)

        def tstart(b, c):
            tcopy(b).start()
            return c

        def twait(b, c):
            tcopy(b).wait()
            return c

        tail0 = pend_ref[N_EXPERTS - 1] // bm
        n_blocks = xb_ref.shape[0] // (bm * ROW_CHUNKS)
        lax.fori_loop(tail0, n_blocks, tstart, 0)
        lax.fori_loop(tail0, n_blocks, twait, 0)

    is_ctx = i < n_ctx_tiles
    f = x_ref[...] * (1.0 + _mod_rows(mod_ref, 4, is_ctx)) + _mod_rows(mod_ref, 3, is_ctx)
    slot = i % 2
    _to_chunk_rows(fbuf.at[slot], f)

    def start(t, c):
        src = pl.multiple_of(t * ROW_CHUNKS, ROW_CHUNKS)
        for k in range(TOP_K):
            a = t * TOP_K + k
            dst = pl.multiple_of((pstart_ref[idx_ref[a]] + rank_ref[a]) * ROW_CHUNKS, ROW_CHUNKS)
            pltpu.make_async_copy(fbuf.at[slot, pl.ds(src, ROW_CHUNKS)], xb_ref.at[pl.ds(dst, ROW_CHUNKS)],
                                  sem.at[slot]).start(priority=k % 2)
        return c

    lax.fori_loop(0, tm, start, 0, unroll=DMA_UNROLL)

    def wait_tile(s):
        for _ in range(TOP_K):
            pltpu.make_async_copy(fbuf.at[s], xb_ref.at[pl.ds(0, tm * ROW_CHUNKS)], sem.at[s]).wait()

    @pl.when(i > 0)
    def _():
        wait_tile(1 - slot)

    @pl.when(i == pl.num_programs(0) - 1)
    def _():
        wait_tile(slot)


def _dispatch(pad_start, pad_end, idx_flat, rank_flat, xs, mod, n_rows, n_ctx_tiles):
    t = xs.shape[0]
    tm = TOK_TILE
    gs = pltpu.PrefetchScalarGridSpec(
        num_scalar_prefetch=2,
        grid=(t // tm,),
        in_specs=[pl.BlockSpec((tm * TOP_K,), lambda i, *_: (i,), memory_space=pltpu.SMEM)] * 2
        + [pl.BlockSpec((tm, D_MODEL), lambda i, *_: (i, 0)), pl.BlockSpec(mod.shape, lambda i, *_: (0, 0))],
        out_specs=pl.BlockSpec(memory_space=pl.ANY),
        scratch_shapes=[pltpu.VMEM((2, tm * ROW_CHUNKS, LANES), F32), pltpu.VMEM((MOE_BM * ROW_CHUNKS, LANES), F32),
                        pltpu.SemaphoreType.DMA((2,)), pltpu.SemaphoreType.DMA],
    )
    return pl.pallas_call(
        functools.partial(_dispatch_kernel, n_ctx_tiles=n_ctx_tiles, bm=MOE_BM),
        out_shape=jax.ShapeDtypeStruct((n_rows * ROW_CHUNKS, LANES), F32),
        grid_spec=gs,
        compiler_params=_cparams(("arbitrary",)),
        name="moe_dispatch",
    )(pad_start, pad_end, idx_flat, rank_flat, xs, mod)


def _expert_kernel(bexp_ref, bvalid_ref, x_ref, wgu_ref, bgu_ref, wd_ref, bd_ref, y_ref, wgu_bf, wd_bf):
    i = pl.program_id(0)
    e = bexp_ref[i]
    prev = bexp_ref[jnp.maximum(i - 1, 0)]

    @pl.when(jnp.logical_or(i == 0, e != prev))
    def _():
        wgu_bf[...] = wgu_ref[...].astype(BF16)
        wd_bf[...] = wd_ref[...].astype(BF16)

    @pl.when(bvalid_ref[i] > 0)
    def _():
        bm = y_ref.shape[0] // ROW_CHUNKS
        x = _from_chunk_rows(x_ref, bm).astype(BF16)
        gu = jnp.dot(x, wgu_bf[...], preferred_element_type=F32) + bgu_ref[...]
        gate = jnp.minimum(gu[:, :D_EXPERT], SWIGLU_LIMIT)
        up = jnp.clip(gu[:, D_EXPERT:], -SWIGLU_LIMIT, SWIGLU_LIMIT)
        act = gate * jax.nn.sigmoid(SWIGLU_ALPHA * gate) * (up + 1.0)
        _to_chunk_rows(y_ref, jnp.dot(act.astype(BF16), wd_bf[...], preferred_element_type=F32) + bd_ref[...])

    @pl.when(bvalid_ref[i] == 0)
    def _():
        y_ref[...] = jnp.zeros_like(y_ref)


def _experts(block_exp, block_valid, xb, layer, w_gate_up, b_gate_up, w_down, b_down):
    n_rows = xb.shape[0] // ROW_CHUNKS
    bm = MOE_BM
    depth = w_gate_up.shape[0]
    bgu = b_gate_up.reshape(depth, N_EXPERTS, 1, 2 * D_EXPERT)
    bd = b_down.reshape(depth, N_EXPERTS, 1, D_MODEL)
    gs = pltpu.PrefetchScalarGridSpec(
        num_scalar_prefetch=2,
        grid=(n_rows // bm,),
        in_specs=[pl.BlockSpec((bm * ROW_CHUNKS, LANES), lambda i, be, bv: (i, 0)),
                  pl.BlockSpec((None, None, D_MODEL, 2 * D_EXPERT), lambda i, be, bv: (layer, be[i], 0, 0)),
                  pl.BlockSpec((None, None, 1, 2 * D_EXPERT), lambda i, be, bv: (layer, be[i], 0, 0)),
                  pl.BlockSpec((None, None, D_EXPERT, D_MODEL), lambda i, be, bv: (layer, be[i], 0, 0)),
                  pl.BlockSpec((None, None, 1, D_MODEL), lambda i, be, bv: (layer, be[i], 0, 0))],
        out_specs=pl.BlockSpec((bm * ROW_CHUNKS, LANES), lambda i, be, bv: (i, 0)),
        scratch_shapes=[pltpu.VMEM((D_MODEL, 2 * D_EXPERT), BF16), pltpu.VMEM((D_EXPERT, D_MODEL), BF16)],
    )
    return pl.pallas_call(
        _expert_kernel,
        out_shape=jax.ShapeDtypeStruct((n_rows * ROW_CHUNKS, LANES), F32),
        grid_spec=gs,
        compiler_params=_cparams(("arbitrary",)),
        name="moe_experts",
    )(block_exp, block_valid, xb, w_gate_up, bgu, w_down, bd)


def _combine_kernel(pstart_ref, idx_ref, rank_ref, idxn_ref, rankn_ref, gate_ref, x_ref, mod_ref, lw_ref, lb_ref, yb_ref,
                    o_ref, ybuf, sem, *, n_ctx_tiles, alpha):
    i = pl.program_id(0)
    tm = x_ref.shape[0]
    slot = i % 2

    def gather(e_ref, r_ref, s):
        def start(t, c):
            dst = pl.multiple_of(t * ROW_CHUNKS, ROW_CHUNKS)
            for k in range(TOP_K):
                a = t * TOP_K + k
                src = pl.multiple_of((pstart_ref[e_ref[a]] + r_ref[a]) * ROW_CHUNKS, ROW_CHUNKS)
                pltpu.make_async_copy(yb_ref.at[pl.ds(src, ROW_CHUNKS)], ybuf.at[s, k, pl.ds(dst, ROW_CHUNKS)],
                                      sem.at[s]).start(priority=k % 2)
            return c

        lax.fori_loop(0, tm, start, 0, unroll=DMA_UNROLL)

    @pl.when(i == 0)
    def _():
        gather(idx_ref, rank_ref, 0)

    @pl.when(i + 1 < pl.num_programs(0))
    def _():
        gather(idxn_ref, rankn_ref, 1 - slot)

    for k in range(TOP_K):
        pltpu.make_async_copy(yb_ref.at[pl.ds(0, tm * ROW_CHUNKS)], ybuf.at[slot, k], sem.at[slot]).wait()
    gates = gate_ref[...]
    y = gates[:, 0:1] * _from_chunk_rows(ybuf.at[slot, 0], tm)
    for k in range(1, TOP_K):
        y = y + gates[:, k:k + 1] * _from_chunk_rows(ybuf.at[slot, k], tm)
    is_ctx = i < n_ctx_tiles
    g2 = _mod_rows(mod_ref, 5, is_ctx)
    o_ref[...] = _layer_norm(alpha * x_ref[...] + g2 * y, lw_ref[...], lb_ref[...])


def _combine(pad_start, idx_flat, rank_flat, gates, xs, mod, ln_w, ln_b, yb, n_ctx_tiles, alpha):
    t = xs.shape[0]
    tm = TOK_TILE
    n_tiles = t // tm
    lw2, lb2 = ln_w.reshape(1, -1), ln_b.reshape(1, -1)
    full = lambda a: pl.BlockSpec(a.shape, lambda i, *_: (0,) * a.ndim)
    cur = pl.BlockSpec((tm * TOP_K,), lambda i, *_: (i,), memory_space=pltpu.SMEM)
    nxt = pl.BlockSpec((tm * TOP_K,), lambda i, *_: (jnp.minimum(i + 1, n_tiles - 1),), memory_space=pltpu.SMEM)
    gs = pltpu.PrefetchScalarGridSpec(
        num_scalar_prefetch=1,
        grid=(n_tiles,),
        in_specs=[cur, cur, nxt, nxt,
                  pl.BlockSpec((tm, TOP_K), lambda i, *_: (i, 0)),
                  pl.BlockSpec((tm, D_MODEL), lambda i, *_: (i, 0)),
                  full(mod), full(lw2), full(lb2),
                  pl.BlockSpec(memory_space=pl.ANY)],
        out_specs=pl.BlockSpec((tm, D_MODEL), lambda i, *_: (i, 0)),
        scratch_shapes=[pltpu.VMEM((2, TOP_K, tm * ROW_CHUNKS, LANES), F32), pltpu.SemaphoreType.DMA((2,))],
    )
    return pl.pallas_call(
        functools.partial(_combine_kernel, n_ctx_tiles=n_ctx_tiles, alpha=alpha),
        out_shape=jax.ShapeDtypeStruct((t, D_MODEL), F32),
        grid_spec=gs,
        compiler_params=_cparams(("arbitrary",)),
        name="moe_combine",
    )(pad_start, idx_flat, rank_flat, idx_flat, rank_flat, gates, xs, mod, lw2, lb2, yb)


def _moe(xs, mod, layer, w_router, b_router, w_gate_up, b_gate_up, w_down, b_down, ln_w, ln_b, n_ctx_tiles, alpha):
    t = xs.shape[0]
    bm = MOE_BM
    idx, gates, rank, cnt = _router(xs, mod, w_router, b_router, n_ctx_tiles)
    counts = cnt[:, 0]
    idx, gates, rank = idx[:TOP_K].T, gates[:TOP_K].T, rank[:TOP_K].T
    padded = (counts + bm - 1) // bm * bm
    pad_end = jnp.cumsum(padded).astype(jnp.int32)
    pad_start = pad_end - padded
    idx_flat, rank_flat = idx.reshape(-1), rank.reshape(-1)
    n_blocks = -(-(t * TOP_K) // bm) + N_EXPERTS
    blk0 = jnp.arange(n_blocks, dtype=jnp.int32) * bm
    block_exp = jnp.minimum(jnp.sum(pad_end[None, :] <= blk0[:, None], axis=1), N_EXPERTS - 1).astype(jnp.int32)
    block_valid = (blk0 < pad_end[-1]).astype(jnp.int32)
    xb = _dispatch(pad_start, pad_end, idx_flat, rank_flat, xs, mod, n_blocks * bm, n_ctx_tiles)
    yb = _experts(block_exp, block_valid, xb, layer, w_gate_up, b_gate_up, w_down, b_down)
    return _combine(pad_start, idx_flat, rank_flat, gates, xs, mod, ln_w, ln_b, yb, n_ctx_tiles, alpha)


def kernel(x, c, ctx, c_ctx, w_ada, b_ada, w_in, mlstm_gate_bias, diff_lambda, diff_norm_w, mlstm_norm_w, w_out,
           ln1_w, ln1_b, w_router, b_router, w_gate_up, b_gate_up, w_down, b_down, ln2_w, ln2_b):
    assert x.shape[0] == 1 and ctx.shape[0] == 1
    depth = w_ada.shape[0]
    n_lat, n_ctx = x.shape[1], ctx.shape[1]
    assert n_ctx % TOK_TILE == 0 and n_lat % TOK_TILE == 0
    nct = n_ctx // TOK_TILE
    alpha = (2 * depth) ** 0.25

    vecs = jnp.zeros((8, D_MODEL), F32).at[0].set(c[0]).at[1].set(c_ctx)
    mods = _ada(vecs, w_ada, b_ada)
    cos, sin = _rope_tables(n_lat, n_ctx)
    xc, xl, lat_row0 = ctx[0], x[0], 0

    for l in range(depth):
        need_ctx = l < depth - 1
        lam_init = 0.8 - 0.6 * math.exp(-0.3 * l)
        mod = mods[l]
        w_ext, w_vt = _prep_w_in(w_in[l])
        q, k, vt, mkt, mq, mk, mv, mo, g, cu = _inproj(xc, xl, lat_row0, n_ctx + n_lat, mod, w_ext, w_vt, cos, sin,
                                                         mlstm_gate_bias[l], n_ctx)

        dl = diff_lambda[l].astype(F32)
        lam = (jnp.exp(jnp.sum(dl[0] * dl[1])) - jnp.exp(jnp.sum(dl[2] * dl[3])) + lam_init).reshape(1)
        post = 1.0 - lam_init
        attn_l = _attention(lam, q, k, vt, diff_norm_w[l], n_ctx, n_lat, n_ctx + n_lat, post)
        four_l = _fourier_lat(cu[n_ctx:])
        if need_ctx:
            attn_c = _attention(lam, q, k, vt, diff_norm_w[l], 0, n_ctx, n_ctx, post)
            four_c = _fourier_small(cu[:n_ctx])
        else:
            attn_c, four_c = attn_l, four_l
        hf, hb = _mlstm(mq, mk, mkt, mv, g, n_ctx)
        xs = _outproj(attn_l, attn_c, four_l, four_c, hf, hb, mo, xc, xl, lat_row0, mod, mlstm_norm_w[l], w_out[l].astype(BF16),
                      ln1_w[l], ln1_b[l], n_ctx, need_ctx, alpha)
        xt = _moe(xs, mod, l, w_router[l], b_router[l], w_gate_up, b_gate_up, w_down, b_down, ln2_w[l], ln2_b[l],
                  nct if need_ctx else 0, alpha)
        xc, xl, lat_row0 = xt, xt, n_ctx
    return xt[None]
```

```python
import functools
import math

import numpy as np
import jax
import jax.numpy as jnp
from jax import lax
from jax.experimental import pallas as pl
from jax.experimental.pallas import tpu as pltpu

F32 = jnp.float32
BF16 = jnp.bfloat16
HI = lax.Precision.HIGHEST

D_MODEL = 1024
GRID_W = 64
HEAD_DIM = 64
A_HEADS = 4
A_QK_DIM = HEAD_DIM
A_V_DIM = 2 * HEAD_DIM
A_WIDTH = A_HEADS * A_V_DIM
A_SCALE = A_QK_DIM ** -0.5
B_HEADS = 4
B_HEAD_DIM = HEAD_DIM
B_WIDTH = B_HEADS * B_HEAD_DIM
C_GROUPS = 4
C_GROUP_DIM = HEAD_DIM
C_WIDTH = C_GROUPS * C_GROUP_DIM
IN_SIZES = (A_WIDTH, A_WIDTH, A_WIDTH, B_WIDTH, B_WIDTH, B_WIDTH, B_WIDTH, 4 * B_HEADS, C_WIDTH)
ROPE_BASE = 10000.0
MLSTM_CHUNK = 64
N_EXPERTS = 32
TOP_K = 4
D_EXPERT = D_MODEL
SWIGLU_LIMIT = 7.0
SWIGLU_ALPHA = 1.702
EPS = 1e-5

LANES = 128
VMEM_LIMIT = 56 * 1024 * 1024
TOK_TILE = 256
ATT_TQ = 512
ATT_COLS = 256
ATT_UNROLL = 32
MOE_BM = 512

_OQ, _OK = 0, 512
_OMQ, _OMK, _OMV, _OMO, _OG, _OCU = 1024, 1280, 1536, 1792, 2048, 2176
Q_SCALE = A_SCALE * math.log2(math.e)
VT_ROWS = A_V_DIM + 16


def _cparams(sem, vmem=VMEM_LIMIT):
    return pltpu.CompilerParams(dimension_semantics=sem, vmem_limit_bytes=vmem)


def _ada_kernel(v_ref, w_ref, b_ref, o_ref):
    v = v_ref[...]
    s = v * jax.nn.sigmoid(v)
    o_ref[...] = jnp.dot(s, w_ref[...], precision=HI, preferred_element_type=F32) + b_ref[...]


def _ada(vecs, w_ada, b_ada):
    depth, d, d6 = w_ada.shape
    bn = 1024
    return pl.pallas_call(
        _ada_kernel,
        out_shape=jax.ShapeDtypeStruct((depth, 8, d6), F32),
        grid=(depth, d6 // bn),
        in_specs=[pl.BlockSpec((8, d), lambda l, j: (0, 0)),
                  pl.BlockSpec((None, d, bn), lambda l, j: (l, 0, j)),
                  pl.BlockSpec((None, 1, bn), lambda l, j: (l, 0, j))],
        out_specs=pl.BlockSpec((None, 8, bn), lambda l, j: (l, 0, j)),
        compiler_params=_cparams(("arbitrary", "arbitrary")),
        name="ada_modulation",
    )(vecs, w_ada, b_ada.reshape(depth, 1, d6))


def _mod_rows(mod_ref, k, is_ctx):
    lat = mod_ref[0:1, k * D_MODEL:(k + 1) * D_MODEL]
    ctx = mod_ref[1:2, k * D_MODEL:(k + 1) * D_MODEL]
    return jnp.where(is_ctx, ctx, lat)


def _inproj_kernel(xc_ref, xl_ref, mod_ref, w_ref, wvt_ref, cos_ref, sin_ref, gb_ref,
                   q_ref, k_ref, vt_ref, mkt_ref, mq_ref, mk_ref, mv_ref, mo_ref, g_ref, cu_ref, *, n_ctx_tiles):
    is_ctx = pl.program_id(0) < n_ctx_tiles
    sh = _mod_rows(mod_ref, 0, is_ctx)
    sc = _mod_rows(mod_ref, 1, is_ctx)
    x = jnp.where(is_ctx, xc_ref[...], xl_ref[...])
    hm = (x * (1.0 + sc) + sh).astype(BF16)

    def proj(off, width):
        return jnp.dot(hm, w_ref[:, off:off + width], preferred_element_type=F32)

    cos = cos_ref[...]
    sin = sin_ref[...]
    quarter = A_QK_DIM // 4
    first_half = (lax.broadcasted_iota(jnp.int32, cos.shape, 1) % (2 * quarter)) < quarter

    def rope(t):
        rot = jnp.where(first_half, -pltpu.roll(t, LANES - quarter, 1), pltpu.roll(t, quarter, 1))
        return t * cos + rot * sin

    for j in range(A_HEADS):
        o = j * LANES
        q_ref[:, o:o + LANES] = (rope(proj(_OQ + o, LANES)) * Q_SCALE).astype(BF16)
        k_ref[:, o:o + LANES] = rope(proj(_OK + o, LANES)).astype(BF16)
    tr = lax.dot_general(wvt_ref[...], hm, (((1,), (1,)), ((), ())), preferred_element_type=F32)
    mkt_ref[...] = tr[A_WIDTH:, :] * (B_HEAD_DIM ** -0.5)
    vt = tr[:A_WIDTH, :].astype(BF16)
    for j in range(A_HEADS):
        vt_ref[j, 0:A_V_DIM, :] = vt[j * A_V_DIM:(j + 1) * A_V_DIM, :]
        vt_ref[j, A_V_DIM:, :] = jnp.ones((VT_ROWS - A_V_DIM, vt.shape[1]), BF16)
    mq_ref[...] = proj(_OMQ, B_WIDTH)
    mk_ref[...] = proj(_OMK, B_WIDTH) * (B_HEAD_DIM ** -0.5)
    mv_ref[...] = proj(_OMV, B_WIDTH)
    mo_ref[...] = proj(_OMO, B_WIDTH)
    g_ref[...] = proj(_OG, LANES) + gb_ref[...]
    cu_ref[...] = proj(_OCU, C_WIDTH)


def _prep_w_in(w_in):
    offs = np.cumsum((0,) + IN_SIZES)
    aq, ak, av, bq, bk, bv, bo, bg, cu = [w_in[:, offs[i]:offs[i + 1]] for i in range(9)]
    bg = jnp.pad(bg, ((0, 0), (0, LANES - bg.shape[1])))
    w = jnp.concatenate([aq, ak, bq, bk, bv, bo, bg, cu], axis=1)
    return w.astype(BF16), jnp.concatenate([av, bk], axis=1).T.astype(BF16)


def _rope_tables(n_lat, n_ctx):
    rows = n_lat // GRID_W
    axis_dim = A_QK_DIM // 2
    inv_freq = ROPE_BASE ** (-np.arange(0, axis_dim, 2, dtype=np.float64) / axis_dim)
    ang_r = np.repeat(np.arange(rows, dtype=np.float64), GRID_W)[:, None] * inv_freq
    ang_c = np.tile(np.arange(GRID_W, dtype=np.float64), rows)[:, None] * inv_freq
    ang = np.concatenate([ang_r, ang_r, ang_c, ang_c] * 2, axis=-1)
    cos = np.concatenate([np.ones((n_ctx, 2 * A_QK_DIM)), np.cos(ang)], axis=0)
    sin = np.concatenate([np.zeros((n_ctx, 2 * A_QK_DIM)), np.sin(ang)], axis=0)
    return jnp.asarray(cos.astype(np.float32)), jnp.asarray(sin.astype(np.float32))


def _inproj(xc, xl, lat_row0, t, mod, w_ext, w_vt, cos, sin, gate_bias, n_ctx):
    tm = TOK_TILE
    nct, lat_blk0 = n_ctx // tm, lat_row0 // tm
    ctx_rows = pl.BlockSpec((tm, D_MODEL), lambda i: (jnp.minimum(i, nct - 1), 0))
    lat_rows = pl.BlockSpec((tm, D_MODEL), lambda i: (jnp.maximum(i - nct, 0) + lat_blk0, 0))
    gb = jnp.pad(gate_bias.reshape(1, -1), ((0, 0), (0, LANES - 4 * B_HEADS)))
    row = lambda w: pl.BlockSpec((tm, w), lambda i: (i, 0))
    full = lambda a: pl.BlockSpec(a.shape, lambda i: (0,) * a.ndim)
    sds = jax.ShapeDtypeStruct
    rows_out = [(A_WIDTH, BF16)] * 2 + [(B_WIDTH, F32)] * 4 + [(LANES, F32), (C_WIDTH, F32)]
    out_shape = [sds((t, w), dt) for w, dt in rows_out]
    out_specs = [row(w) for w, _ in rows_out]
    out_shape.insert(2, sds((t // tm, A_HEADS, VT_ROWS, tm), BF16))
    out_specs.insert(2, pl.BlockSpec((None, A_HEADS, VT_ROWS, tm), lambda i: (i, 0, 0, 0)))
    out_shape.insert(3, sds((t // tm, B_WIDTH, tm), F32))
    out_specs.insert(3, pl.BlockSpec((None, B_WIDTH, tm), lambda i: (i, 0, 0)))
    return pl.pallas_call(
        functools.partial(_inproj_kernel, n_ctx_tiles=n_ctx // tm),
        out_shape=out_shape,
        grid=(t // tm,),
        in_specs=[ctx_rows, lat_rows, full(mod), full(w_ext), full(w_vt), row(LANES), row(LANES), full(gb)],
        out_specs=out_specs,
        compiler_params=_cparams(("arbitrary",)),
        name="in_projection",
    )(xc, xl, mod, w_ext, w_vt, cos, sin, gb)


def _attn_kernel(lam_ref, *refs, n_q_sub, n_kv, post_scale):
    q_refs, (k_ref, vt_ref, nw_ref, o_ref, acc_ref, s_ref) = refs[:n_q_sub], refs[n_q_sub:]
    q = jnp.concatenate([r[...] for r in q_refs], axis=0) if n_q_sub > 1 else q_refs[0][...]
    tq = q.shape[0]
    tk = vt_ref.shape[-1]
    lane = lax.broadcasted_iota(jnp.int32, q.shape, 1)
    zero = jnp.zeros_like(q)
    q2 = jnp.concatenate([jnp.where(lane < A_QK_DIM, q, zero), jnp.where(lane >= A_QK_DIM, q, zero)], axis=0)
    acc_ref[...] = jnp.zeros_like(acc_ref)

    def scores(j, slot):
        kb = k_ref[pl.ds(pl.multiple_of(j * tk, tk), tk), :]
        st = lax.dot_general(kb, q2, (((1,), (1,)), ((), ())), preferred_element_type=F32)
        s_ref[slot] = st
        return jnp.max(st, axis=0, keepdims=True)

    def consume(j, slot, cmax, m):
        m_new = jnp.maximum(m, cmax)
        alpha = jnp.exp2(m - m_new)
        for h in range(2 * tq // ATT_COLS):
            cs = slice(h * ATT_COLS, (h + 1) * ATT_COLS)
            p = jnp.exp2(s_ref[slot, :, cs] - m_new[:, cs]).astype(BF16)
            r = jnp.dot(vt_ref[j], p, preferred_element_type=F32)
            acc_ref[:, cs] = alpha[:, cs] * acc_ref[:, cs] + r
        return m_new

    cm0 = scores(0, 0)

    unroll = math.gcd(n_kv - 1, ATT_UNROLL)

    def group(t, carry):
        cm, m = carry
        a = unroll * t
        for i in range(unroll):
            cm_next = scores(a + i + 1, (i + 1) % 2)
            m = consume(a + i, i % 2, cm, m)
            cm = cm_next
        return cm, m

    assert unroll % 2 == 0
    cm_last, m = lax.fori_loop(0, (n_kv - 1) // unroll, group, (cm0, jnp.full((1, 2 * tq), -jnp.inf, F32)))
    consume(n_kv - 1, 0, cm_last, m)
    l = acc_ref[A_V_DIM:A_V_DIM + 1, :]
    acc = acc_ref[0:A_V_DIM, :]
    ot = acc[:, :tq] * (1.0 / l[:, :tq]) - acc[:, tq:] * (lam_ref[0] / l[:, tq:])
    yt = ot * lax.rsqrt(jnp.mean(ot * ot, axis=0, keepdims=True) + EPS)
    o_ref[...] = yt.T * nw_ref[...] * post_scale


def _attention(lam, q, k, vt, norm_w, q_row0, n_q, n_keys, post_scale):
    tq = min(ATT_TQ, n_q)
    tk = vt.shape[-1]
    n_kv = n_keys // tk
    sub = math.gcd(tq, q_row0) if q_row0 else tq
    n_sub = tq // sub
    q_specs = [pl.BlockSpec((sub, LANES), lambda h, i, j=j: (i * n_sub + j + q_row0 // sub, h))
               for j in range(n_sub)]
    return pl.pallas_call(
        functools.partial(_attn_kernel, n_q_sub=n_sub, n_kv=n_kv, post_scale=post_scale),
        out_shape=jax.ShapeDtypeStruct((n_q, A_WIDTH), F32),
        grid=(A_HEADS, n_q // tq),
        in_specs=[pl.BlockSpec(memory_space=pltpu.SMEM)] + q_specs + [
                  pl.BlockSpec((n_keys, LANES), lambda h, i: (0, h)),
                  pl.BlockSpec((n_kv, None, VT_ROWS, tk), lambda h, i: (0, h, 0, 0)),
                  pl.BlockSpec((1, LANES), lambda h, i: (0, h))],
        out_specs=pl.BlockSpec((tq, LANES), lambda h, i: (i, h)),
        scratch_shapes=[pltpu.VMEM((VT_ROWS, 2 * tq), F32), pltpu.VMEM((2, tk, 2 * tq), F32)],
        compiler_params=_cparams(("arbitrary", "arbitrary")),
        name="diff_attention",
    )(lam, *([q] * n_sub), k, vt, norm_w.reshape(1, A_WIDTH))


def _mlstm_kernel(qf, kf, ktf, vf, gf, qb, kb, ktb, vb, gb, of_ref, ob_ref, s_ref, m_ref, *, chunks):
    L = MLSTM_CHUNK
    tm = chunks * L
    neg = -jnp.inf

    @pl.when(pl.program_id(0) == 0)
    def _():
        s_ref[...] = jnp.zeros_like(s_ref)
        m_ref[...] = jnp.zeros_like(m_ref)

    tr = lax.broadcasted_iota(jnp.int32, (tm, tm), 0)
    tc = lax.broadcasted_iota(jnp.int32, (tm, tm), 1)
    same_chunk = tr // L == tc // L
    masks = (same_chunk & (tr >= tc), same_chunk & (tr <= tc))
    row_chunk = lax.broadcasted_iota(jnp.int32, (tm, 1), 0) // L
    col_chunk = lax.broadcasted_iota(jnp.int32, (1, tm), 1) // L
    lane_half = lax.broadcasted_iota(jnp.int32, (tm, LANES), 1) // B_HEAD_DIM
    row_half = lax.broadcasted_iota(jnp.int32, (LANES, 1), 0) // B_HEAD_DIM
    rr = lax.broadcasted_iota(jnp.int32, (LANES, 2 * LANES), 0) // B_HEAD_DIM
    cc = lax.broadcasted_iota(jnp.int32, (LANES, 2 * LANES), 1)
    blockmask = jnp.where(cc < LANES, cc // B_HEAD_DIM, cc - LANES) == rr
    aug = jnp.where(lax.broadcasted_iota(jnp.int32, (tm, LANES), 1) < 2, 1.0, 0.0).astype(F32)
    io = ((qf, kf, ktf, vf, gf, of_ref), (qb, kb, ktb, vb, gb, ob_ref))

    def by_chunk(vals, chunk_ids):
        out = vals[chunks - 1]
        for c in range(chunks - 2, -1, -1):
            out = jnp.where(chunk_ids == c, vals[c], out)
        return out

    dp = [(d, p) for d in range(2) for p in range(2)]
    dph = [(d, p, hh) for d, p in dp for hh in range(2)]

    g_tile, b_tile, g_t, b_t = {}, {}, {}, {}
    for d in range(2):
        g_tile[d] = io[d][4][...]
        b_tile[d] = jnp.dot(masks[d].astype(F32), jax.nn.log_sigmoid(g_tile[d]), precision=HI,
                            preferred_element_type=F32)
        g_t[d] = g_tile[d].T
        b_t[d] = b_tile[d].T
    qp, kp, v_aug, ktp = {}, {}, {}, {}
    for d, p in dp:
        sl = slice(p * LANES, (p + 1) * LANES)
        qp[d, p] = io[d][0][:, sl]
        kp[d, p] = io[d][1][:, sl].astype(BF16)
        ktp[d, p] = io[d][2][sl, :]
        v_aug[d, p] = jnp.concatenate([io[d][3][:, sl], aug], axis=1).astype(BF16)

    b_col, b_row, i_row, gs, a_row, m_loc, w_t = {}, {}, {}, {}, {}, {}, {}
    for d, p, hh in dph:
        h = 2 * p + hh
        il, fl = 4 * d + h, 8 + 4 * d + h
        last = L - 1 if d == 0 else 0
        b_col[d, p, hh] = b_tile[d][:, fl:fl + 1]
        b_row[d, p, hh] = b_t[d][fl:fl + 1, :]
        i_row[d, p, hh] = g_t[d][il:il + 1, :]
        gs[d, p, hh] = [b_row[d, p, hh][:, c * L + last:c * L + last + 1] for c in range(chunks)]
        a_row[d, p, hh] = by_chunk(gs[d, p, hh], col_chunk) - b_row[d, p, hh] + i_row[d, p, hh]
    for k3 in dph:
        m_loc[k3] = [jnp.max(jnp.where(col_chunk == c, a_row[k3], neg), axis=-1, keepdims=True)
                     for c in range(chunks)]
    for k3 in dph:
        w_t[k3] = jnp.exp(a_row[k3] - by_chunk(m_loc[k3], col_chunk))

    upd, sqk = {}, {}
    for d, p in dp:
        kw_t = ktp[d, p] * jnp.where(row_half == 0, w_t[d, p, 0], w_t[d, p, 1])
        for c in range(chunks):
            kw_c = jnp.where(col_chunk == c, kw_t, 0.0).astype(BF16)
            upd[d, p, c] = jnp.where(blockmask, jnp.dot(kw_c, v_aug[d, p], preferred_element_type=F32), 0.0)
    for d, p, hh in dph:
        qm = jnp.where(lane_half == hh, qp[d, p], 0.0).astype(BF16)
        sqk[d, p, hh] = lax.dot_general(qm, kp[d, p], (((1,), (1,)), ((), ())), preferred_element_type=F32)

    m_old = m_ref[...]
    orders = (range(chunks), range(chunks - 1, -1, -1))
    m_at, dec, inj, m_end = {}, {}, {}, {}
    for d, p, hh in dph:
        row = 4 * d + 2 * p + hh
        m_prev = m_old[row:row + 1, 0:1]
        for c in orders[d]:
            m_at[d, p, hh, c] = m_prev
            g_c = gs[d, p, hh][c]
            m_new = jnp.maximum(g_c + m_prev, m_loc[d, p, hh][c])
            dec[d, p, hh, c] = jnp.exp(g_c + m_prev - m_new)
            inj[d, p, hh, c] = jnp.exp(m_loc[d, p, hh][c] - m_new)
            m_prev = m_new
        m_end[row] = m_prev
    state_at, state_end = {}, {}
    for d, p in dp:
        state = s_ref[d, p]
        for c in orders[d]:
            state_at[d, p, c] = state
            state = (jnp.where(rr == 0, dec[d, p, 0, c], dec[d, p, 1, c]) * state
                     + jnp.where(rr == 0, inj[d, p, 0, c], inj[d, p, 1, c]) * upd[d, p, c])
        state_end[d, p] = state
    for d, p in dp:
        s_ref[d, p] = state_end[d, p]
    m_rows = lax.broadcasted_iota(jnp.int32, m_old.shape, 0)
    m_out = m_old
    for row, val in m_end.items():
        m_out = jnp.where(m_rows == row, val, m_out)
    m_ref[...] = m_out

    dmat = {k3: jnp.where(masks[k3[0]], b_col[k3] + (i_row[k3] - b_row[k3]), neg) for k3 in dph}
    m_row = {k3: jnp.max(dmat[k3], axis=-1, keepdims=True) for k3 in dph}
    s_loc = {k3: (sqk[k3] * jnp.exp(dmat[k3] - m_row[k3])).astype(BF16) for k3 in dph}

    r = {}
    for d, p, hh in dph:
        r[d, p, hh] = jnp.dot(s_loc[d, p, hh], v_aug[d, p], preferred_element_type=F32)
    q2 = {}
    for d, p in dp:
        q2[d, p] = jnp.concatenate(
            [jnp.dot(qp[d, p][c * L:(c + 1) * L].astype(BF16), state_at[d, p, c].astype(BF16),
                     preferred_element_type=F32) for c in range(chunks)], axis=0)

    inter = {k3: b_col[k3] + by_chunk([m_at[k3 + (c,)] for c in range(chunks)], row_chunk) for k3 in dph}
    m_col = {k3: jnp.maximum(inter[k3], m_row[k3]) for k3 in dph}
    e_loc = {k3: jnp.exp(m_row[k3] - m_col[k3]) for k3 in dph}
    e_int = {k3: jnp.exp(inter[k3] - m_col[k3]) for k3 in dph}
    floor = {k3: jnp.exp(-m_col[k3]) for k3 in dph}
    den = {(d, p, hh): e_loc[d, p, hh] * r[d, p, hh][:, LANES:LANES + 1]
           + e_int[d, p, hh] * q2[d, p][:, LANES + hh:LANES + hh + 1] for d, p, hh in dph}
    inv = {k3: 1.0 / jnp.maximum(jnp.abs(den[k3]), floor[k3]) for k3 in dph}
    a_loc = {k3: e_loc[k3] * inv[k3] for k3 in dph}
    a_int = {k3: e_int[k3] * inv[k3] for k3 in dph}
    head_out = {(d, p, hh): a_loc[d, p, hh] * r[d, p, hh][:, :LANES] + a_int[d, p, hh] * q2[d, p][:, :LANES]
                for d, p, hh in dph}
    for d in range(2):
        io[d][5][...] = jnp.concatenate(
            [jnp.where(lane_half == 0, head_out[d, p, 0], head_out[d, p, 1]) for p in range(2)], axis=1)


def _mlstm(mq, mk, mkt, mv, g, n_ctx):
    t = mq.shape[0]
    tm = TOK_TILE
    n_tiles = t // tm
    n_ctx_tiles = n_ctx // tm
    mirror = lambda i: jnp.where(i < n_ctx_tiles, n_ctx_tiles - 1 - i, n_tiles - 1 - (i - n_ctx_tiles))
    fwd = lambda i: (i, 0)
    bwd = lambda i: (mirror(i), 0)
    spec = lambda w, im: pl.BlockSpec((tm, w), im)
    tspec = lambda f: pl.BlockSpec((None, B_WIDTH, tm), lambda i: (f(i), 0, 0))
    ins = ([spec(B_WIDTH, fwd)] * 2 + [tspec(lambda i: i), spec(B_WIDTH, fwd), spec(LANES, fwd)]
           + [spec(B_WIDTH, bwd)] * 2 + [tspec(mirror), spec(B_WIDTH, bwd), spec(LANES, bwd)])
    return pl.pallas_call(
        functools.partial(_mlstm_kernel, chunks=tm // MLSTM_CHUNK),
        out_shape=[jax.ShapeDtypeStruct((t, B_WIDTH), F32)] * 2,
        grid=(n_tiles,),
        in_specs=ins,
        out_specs=[spec(B_WIDTH, fwd), spec(B_WIDTH, bwd)],
        scratch_shapes=[pltpu.VMEM((2, 2, LANES, 2 * LANES), F32), pltpu.VMEM((8, LANES), F32)],
        compiler_params=_cparams(("arbitrary",)),
        name="mlstm_scan",
    )(mq, mk, mkt, mv, g, mq, mk, mkt, mv, g)


def _dft_cs(n):
    j = np.arange(n, dtype=np.int64)
    ang = 2.0 * np.pi * ((j[:, None] * j[None, :]) % n).astype(np.float64) / n
    s = 1.0 / math.sqrt(n)
    return np.cos(ang) * s, np.sin(ang) * s


def _chan_mats():
    c, s = _dft_cs(C_GROUP_DIM)
    eye = np.eye(C_GROUPS)
    return np.concatenate([np.kron(eye, c), np.kron(eye, s)], axis=0).astype(np.float32)


def _split_bf16(x):
    hi = x.astype(BF16)
    return hi, (x - hi.astype(F32)).astype(BF16)


def _split_const(a):
    hi, lo = _split_bf16(jnp.asarray(a, F32))
    return jnp.stack([hi, lo])


def _dot3(a_hi, a_lo, b_hi, b_lo):
    dot = functools.partial(jnp.dot, preferred_element_type=F32)
    return dot(a_hi, b_hi) + dot(a_hi, b_lo) + dot(a_lo, b_hi)


def _fft1_kernel(fa_ref, u_ref, z_ref):
    z_ref[...] = _dot3(fa_ref[0], fa_ref[1], *_split_bf16(u_ref[...]))


def _fft2_kernel(z_ref, twc_ref, tws_ref, fb_ref, ch_ref, o_ref, *, kb):
    n2 = fb_ref.shape[1] // 2
    for j in range(kb):
        zr = z_ref[0, j]
        zi = z_ref[1, j]
        tc = twc_ref[j]
        ts = tws_ref[j]
        st = jnp.concatenate([zr * tc + zi * ts, zi * tc - zr * ts], axis=0)
        a = _dot3(fb_ref[0], fb_ref[1], *_split_bf16(st))
        ari_hi, ari_lo = _split_bf16(jnp.concatenate([a[:n2], a[n2:]], axis=1))
        o_ref[:, j * C_WIDTH:(j + 1) * C_WIDTH] = _dot3(ari_hi, ari_lo, ch_ref[0], ch_ref[1])


def _fourier_lat(u):
    n = u.shape[0]
    n1 = int(round(math.sqrt(n)))
    assert n1 * n1 == n
    n2 = n1
    c, s = _dft_cs(n1)
    fa = _split_const(np.concatenate([c, -s], axis=0))
    fb = _split_const(np.block([[c, s], [-s, c]]))
    k1 = np.arange(n1, dtype=np.int64)
    ang = 2.0 * np.pi * ((k1[:, None] * k1[None, :]) % n).astype(np.float64) / n
    twc = jnp.asarray(np.cos(ang).astype(np.float32)).reshape(n1, n2, 1)
    tws = jnp.asarray(np.sin(ang).astype(np.float32)).reshape(n1, n2, 1)
    ch = _split_const(_chan_mats())
    cols = n2 * C_WIDTH
    tb = min(2048, cols)
    z = pl.pallas_call(
        _fft1_kernel,
        out_shape=jax.ShapeDtypeStruct((2 * n1, cols), F32),
        grid=(cols // tb,),
        in_specs=[pl.BlockSpec((2, 2 * n1, n1), lambda i: (0, 0, 0)), pl.BlockSpec((n1, tb), lambda i: (0, i))],
        out_specs=pl.BlockSpec((2 * n1, tb), lambda i: (0, i)),
        compiler_params=_cparams(("arbitrary",)),
        name="fourier_stage1",
    )(fa, u.reshape(n1, cols))
    kb = 8
    out = pl.pallas_call(
        functools.partial(_fft2_kernel, kb=kb),
        out_shape=jax.ShapeDtypeStruct((n2, n1 * C_WIDTH), F32),
        grid=(n1 // kb,),
        in_specs=[pl.BlockSpec((2, kb, n2, C_WIDTH), lambda i: (0, i, 0, 0)),
                  pl.BlockSpec((kb, n2, 1), lambda i: (i, 0, 0)),
                  pl.BlockSpec((kb, n2, 1), lambda i: (i, 0, 0)),
                  pl.BlockSpec((2, 2 * n2, 2 * n2), lambda i: (0, 0, 0)),
                  pl.BlockSpec((2, 2 * C_WIDTH, C_WIDTH), lambda i: (0, 0, 0))],
        out_specs=pl.BlockSpec((n2, kb * C_WIDTH), lambda i: (0, i)),
        compiler_params=_cparams(("arbitrary",)),
        name="fourier_stage2",
    )(z.reshape(2, n1, n2, C_WIDTH), twc, tws, fb, ch)
    return out.reshape(n, C_WIDTH)


def _fft_small_kernel(fa_ref, u_ref, ch_ref, o_ref):
    n = u_ref.shape[0]
    a = jnp.dot(fa_ref[...], u_ref[...], precision=HI, preferred_element_type=F32)
    ari = jnp.concatenate([a[:n], a[n:]], axis=1)
    o_ref[...] = jnp.dot(ari, ch_ref[...], precision=HI, preferred_element_type=F32)


def _fourier_small(u):
    n = u.shape[0]
    c, s = _dft_cs(n)
    fa = jnp.asarray(np.concatenate([c, -s], axis=0).astype(np.float32))
    return pl.pallas_call(
        _fft_small_kernel,
        out_shape=jax.ShapeDtypeStruct((n, C_WIDTH), F32),
        compiler_params=_cparams(None),
        name="fourier_small",
    )(fa, u, jnp.asarray(_chan_mats()))


def _layer_norm(z, w, b):
    mu = jnp.mean(z, axis=-1, keepdims=True)
    zc = z - mu
    var = jnp.mean(zc * zc, axis=-1, keepdims=True)
    return zc * lax.rsqrt(var + EPS) * w + b


def _outproj_kernel(al_ref, ac_ref, fl_ref, fc_ref, hf_ref, hb_ref, bo_ref, xc_ref, xl_ref, mod_ref, mnw_ref, seg_ref,
                    w_ref, lw_ref, lb_ref, o_ref, *, n_ctx_tiles, alpha):
    is_ctx = pl.program_id(0) < n_ctx_tiles
    attn = jnp.where(is_ctx, ac_ref[...], al_ref[...])
    four = jnp.where(is_ctx, fc_ref[...], fl_ref[...])
    h = hf_ref[...] + hb_ref[...]
    ms = jnp.dot(h * h, seg_ref[...], precision=HI, preferred_element_type=F32)
    y = h * lax.rsqrt(ms + EPS) * mnw_ref[...] * jax.nn.sigmoid(bo_ref[...])
    mix = (jnp.dot(attn.astype(BF16), w_ref[0:A_WIDTH, :], preferred_element_type=F32)
           + jnp.dot(y.astype(BF16), w_ref[A_WIDTH:A_WIDTH + B_WIDTH, :], preferred_element_type=F32)
           + jnp.dot(four.astype(BF16), w_ref[A_WIDTH + B_WIDTH:, :], preferred_element_type=F32))
    g1 = _mod_rows(mod_ref, 2, is_ctx)
    x = jnp.where(is_ctx, xc_ref[...], xl_ref[...])
    o_ref[...] = _layer_norm(alpha * x + g1 * mix, lw_ref[...], lb_ref[...])


def _outproj(attn_l, attn_c, four_l, four_c, hf, hb, bo, xc, xl, lat_row0, mod, mnw, w_out_bf, ln_w, ln_b, n_ctx,
             with_ctx, alpha):
    tm = TOK_TILE
    nct = n_ctx // tm
    n_lat_tiles = attn_l.shape[0] // tm
    if with_ctx:
        n_tiles, uoff, n_ctx_tiles = n_lat_tiles + nct, 0, nct
        lat = lambda i: (jnp.maximum(i - nct, 0), 0)
    else:
        n_tiles, uoff, n_ctx_tiles = n_lat_tiles, nct, 0
        lat = lambda i: (i, 0)
    uni = lambda i: (i + uoff, 0)
    ctxm = lambda i: (jnp.minimum(i, nct - 1), 0)
    lat_blk0 = lat_row0 // tm
    latx = lambda i: (lat(i)[0] + lat_blk0, 0)
    seg = jnp.asarray(np.kron(np.eye(B_HEADS), np.full((B_HEAD_DIM, B_HEAD_DIM), 1.0 / B_HEAD_DIM)).astype(np.float32))
    full = lambda a: pl.BlockSpec(a.shape, lambda i: (0,) * a.ndim)
    mnw2, lw2, lb2 = mnw.reshape(1, -1), ln_w.reshape(1, -1), ln_b.reshape(1, -1)
    return pl.pallas_call(
        functools.partial(_outproj_kernel, n_ctx_tiles=n_ctx_tiles, alpha=alpha),
        out_shape=jax.ShapeDtypeStruct((n_tiles * tm, D_MODEL), F32),
        grid=(n_tiles,),
        in_specs=[pl.BlockSpec((tm, A_WIDTH), lat), pl.BlockSpec((tm, A_WIDTH), ctxm),
                  pl.BlockSpec((tm, C_WIDTH), lat), pl.BlockSpec((tm, C_WIDTH), ctxm),
                  pl.BlockSpec((tm, B_WIDTH), uni), pl.BlockSpec((tm, B_WIDTH), uni),
                  pl.BlockSpec((tm, B_WIDTH), uni), pl.BlockSpec((tm, D_MODEL), ctxm), pl.BlockSpec((tm, D_MODEL), latx),
                  full(mod), full(mnw2), full(seg), full(w_out_bf), full(lw2), full(lb2)],
        out_specs=pl.BlockSpec((tm, D_MODEL), lambda i: (i, 0)),
        compiler_params=_cparams(("arbitrary",)),
        name="out_projection",
    )(attn_l, attn_c, four_l, four_c, hf, hb, bo, xc, xl, mod, mnw2, seg, w_out_bf, lw2, lb2)


def _router_kernel(x_ref, mod_ref, wrt_ref, br_ref, idx_ref, gate_ref, rank_ref, cnt_ref, run_ref, *, n_ctx_tiles):
    i = pl.program_id(0)

    @pl.when(i == 0)
    def _():
        run_ref[...] = jnp.zeros_like(run_ref)

    is_ctx = i < n_ctx_tiles
    f_in = x_ref[...] * (1.0 + _mod_rows(mod_ref, 4, is_ctx)) + _mod_rows(mod_ref, 3, is_ctx)
    logits = lax.dot_general(wrt_ref[...], f_in, (((1,), (1,)), ((), ())), precision=HI,
                             preferred_element_type=F32) + br_ref[...]
    tm = logits.shape[1]
    eidx = lax.broadcasted_iota(jnp.int32, logits.shape, 0).astype(F32)
    work = logits
    vals, sels = [], []
    for _ in range(TOP_K):
        mx = jnp.max(work, axis=0, keepdims=True)
        sel = jnp.min(jnp.where(work == mx, eidx, float(N_EXPERTS)), axis=0, keepdims=True)
        vals.append(mx)
        sels.append(sel)
        work = jnp.where(eidx == sel, -jnp.inf, work)
    es = [jnp.exp(v - vals[0]) for v in vals]
    tot = es[0] + es[1] + es[2] + es[3]
    onehots = [(eidx == s).astype(F32) for s in sels]
    oh_all = onehots[0] + onehots[1] + onehots[2] + onehots[3]
    r = lax.broadcasted_iota(jnp.int32, (tm, tm), 0)
    c = lax.broadcasted_iota(jnp.int32, (tm, tm), 1)
    before = jnp.dot(oh_all.astype(BF16), (r < c).astype(BF16), preferred_element_type=F32) + run_ref[:, 0:1]
    krow = lax.broadcasted_iota(jnp.int32, (8, tm), 0)
    idx_o = jnp.zeros((8, tm), jnp.int32)
    gate_o = jnp.zeros((8, tm), F32)
    rank_o = jnp.zeros((8, tm), jnp.int32)
    for k in range(TOP_K):
        rank = jnp.sum(onehots[k] * before, axis=0, keepdims=True).astype(jnp.int32)
        idx_o = jnp.where(krow == k, sels[k].astype(jnp.int32), idx_o)
        gate_o = jnp.where(krow == k, es[k] / tot, gate_o)
        rank_o = jnp.where(krow == k, rank, rank_o)
    idx_ref[...] = idx_o
    gate_ref[...] = gate_o
    rank_ref[...] = rank_o
    run_new = run_ref[...] + jnp.sum(oh_all, axis=1, keepdims=True)
    run_ref[...] = run_new
    cnt_ref[...] = run_new.astype(jnp.int32)


def _router(xs, mod, w_router, b_router, n_ctx_tiles):
    t = xs.shape[0]
    tm = TOK_TILE
    wrt = w_router.T
    br = b_router.reshape(-1, 1)
    full = lambda a: pl.BlockSpec(a.shape, lambda i: (0,) * a.ndim)
    col = pl.BlockSpec((8, tm), lambda i: (0, i))
    return pl.pallas_call(
        functools.partial(_router_kernel, n_ctx_tiles=n_ctx_tiles),
        out_shape=[jax.ShapeDtypeStruct((8, t), jnp.int32), jax.ShapeDtypeStruct((8, t), F32),
                   jax.ShapeDtypeStruct((8, t), jnp.int32), jax.ShapeDtypeStruct((N_EXPERTS, LANES), jnp.int32)],
        grid=(t // tm,),
        in_specs=[pl.BlockSpec((tm, D_MODEL), lambda i: (i, 0)), full(mod), full(wrt), full(br)],
        out_specs=[col, col, col, pl.BlockSpec((N_EXPERTS, LANES), lambda i: (0, 0))],
        scratch_shapes=[pltpu.VMEM((N_EXPERTS, LANES), F32)],
        compiler_params=_cparams(("arbitrary",)),
        name="moe_router",
    )(xs, mod, wrt, br)


ROW_CHUNKS = D_MODEL // LANES
DMA_UNROLL = 8


def _to_chunk_rows(ref, val):
    m = val.shape[0]
    for c in range(ROW_CHUNKS):
        ref[pl.ds(c, m, stride=ROW_CHUNKS), :] = val[:, c * LANES:(c + 1) * LANES]


def _from_chunk_rows(ref, m):
    return jnp.concatenate([ref[pl.ds(c, m, stride=ROW_CHUNKS), :] for c in range(ROW_CHUNKS)], axis=1)


def _dispatch_kernel(pstart_ref, pend_ref, dest_ref, x_ref, mod_ref, xb_ref, fbuf, zbuf, sem, zsem, *, n_ctx_tiles, bm):
    i = pl.program_id(0)
    tm = x_ref.shape[0]

    @pl.when(i == 0)
    def _():
        zbuf[...] = jnp.zeros_like(zbuf)

        def zcopy(e):
            row0 = pl.multiple_of((pend_ref[e] - bm) * ROW_CHUNKS, bm * ROW_CHUNKS)
            return pltpu.make_async_copy(zbuf, xb_ref.at[pl.ds(row0, bm * ROW_CHUNKS)], zsem)

        def zstart(e, c):
            @pl.when(pend_ref[e] > pstart_ref[e])
            def _():
                zcopy(e).start()
            return c

        def zwait(e, c):
            @pl.when(pend_ref[e] > pstart_ref[e])
            def _():
                zcopy(e).wait()
            return c

        lax.fori_loop(0, N_EXPERTS, zstart, 0)
        lax.fori_loop(0, N_EXPERTS, zwait, 0)

        def tcopy(b):
            row0 = pl.multiple_of(b * (bm * ROW_CHUNKS), bm * ROW_CHUNKS)
            return pltpu.make_async_copy(zbuf, xb_ref.at[pl.ds(row0, bm * ROW_CHUNKS)], zsem)

        def tstart(b, c):
            tcopy(b).start()
            return c

        def twait(b, c):
            tcopy(b).wait()
            return c

        tail0 = pend_ref[N_EXPERTS - 1] // bm
        n_blocks = xb_ref.shape[0] // (bm * ROW_CHUNKS)
        lax.fori_loop(tail0, n_blocks, tstart, 0)
        lax.fori_loop(tail0, n_blocks, twait, 0)

    is_ctx = i < n_ctx_tiles
    f = x_ref[...] * (1.0 + _mod_rows(mod_ref, 4, is_ctx)) + _mod_rows(mod_ref, 3, is_ctx)
    slot = i % 2
    _to_chunk_rows(fbuf.at[slot], f)

    def start(t, c):
        src = pl.multiple_of(t * ROW_CHUNKS, ROW_CHUNKS)
        for k in range(TOP_K):
            dst = pl.multiple_of(dest_ref[t * TOP_K + k] * ROW_CHUNKS, ROW_CHUNKS)
            pltpu.make_async_copy(fbuf.at[slot, pl.ds(src, ROW_CHUNKS)], xb_ref.at[pl.ds(dst, ROW_CHUNKS)],
                                  sem.at[slot]).start(priority=k % 2)
        return c

    lax.fori_loop(0, tm, start, 0, unroll=DMA_UNROLL)

    def wait_tile(s):
        for _ in range(TOP_K):
            pltpu.make_async_copy(fbuf.at[s], xb_ref.at[pl.ds(0, tm * ROW_CHUNKS)], sem.at[s]).wait()

    @pl.when(i > 0)
    def _():
        wait_tile(1 - slot)

    @pl.when(i == pl.num_programs(0) - 1)
    def _():
        wait_tile(slot)


def _dispatch(pad_start, pad_end, dest_flat, xs, mod, n_rows, n_ctx_tiles):
    t = xs.shape[0]
    tm = TOK_TILE
    gs = pltpu.PrefetchScalarGridSpec(
        num_scalar_prefetch=2,
        grid=(t // tm,),
        in_specs=[pl.BlockSpec((tm * TOP_K,), lambda i, *_: (i,), memory_space=pltpu.SMEM),
                  pl.BlockSpec((tm, D_MODEL), lambda i, *_: (i, 0)),
                  pl.BlockSpec(mod.shape, lambda i, *_: (0, 0))],
        out_specs=pl.BlockSpec(memory_space=pl.ANY),
        scratch_shapes=[pltpu.VMEM((2, tm * ROW_CHUNKS, LANES), F32), pltpu.VMEM((MOE_BM * ROW_CHUNKS, LANES), F32),
                        pltpu.SemaphoreType.DMA((2,)), pltpu.SemaphoreType.DMA],
    )
    return pl.pallas_call(
        functools.partial(_dispatch_kernel, n_ctx_tiles=n_ctx_tiles, bm=MOE_BM),
        out_shape=jax.ShapeDtypeStruct((n_rows * ROW_CHUNKS, LANES), F32),
        grid_spec=gs,
        compiler_params=_cparams(("arbitrary",)),
        name="moe_dispatch",
    )(pad_start, pad_end, dest_flat, xs, mod)


def _expert_kernel(bexp_ref, bvalid_ref, x_ref, wgu_ref, bgu_ref, wd_ref, bd_ref, y_ref, wgu_bf, wd_bf):
    i = pl.program_id(0)
    e = bexp_ref[i]
    prev = bexp_ref[jnp.maximum(i - 1, 0)]

    @pl.when(jnp.logical_or(i == 0, e != prev))
    def _():
        wgu_bf[...] = wgu_ref[...].astype(BF16)
        wd_bf[...] = wd_ref[...].astype(BF16)

    @pl.when(bvalid_ref[i] > 0)
    def _():
        bm = y_ref.shape[0] // ROW_CHUNKS
        x = _from_chunk_rows(x_ref, bm).astype(BF16)
        gu = jnp.dot(x, wgu_bf[...], preferred_element_type=F32) + bgu_ref[...]
        gate = jnp.minimum(gu[:, :D_EXPERT], SWIGLU_LIMIT)
        up = jnp.clip(gu[:, D_EXPERT:], -SWIGLU_LIMIT, SWIGLU_LIMIT)
        act = gate * jax.nn.sigmoid(SWIGLU_ALPHA * gate) * (up + 1.0)
        _to_chunk_rows(y_ref, jnp.dot(act.astype(BF16), wd_bf[...], preferred_element_type=F32) + bd_ref[...])

    @pl.when(bvalid_ref[i] == 0)
    def _():
        y_ref[...] = jnp.zeros_like(y_ref)


def _experts(block_exp, block_valid, xb, layer, w_gate_up, b_gate_up, w_down, b_down):
    n_rows = xb.shape[0] // ROW_CHUNKS
    bm = MOE_BM
    depth = w_gate_up.shape[0]
    bgu = b_gate_up.reshape(depth, N_EXPERTS, 1, 2 * D_EXPERT)
    bd = b_down.reshape(depth, N_EXPERTS, 1, D_MODEL)
    gs = pltpu.PrefetchScalarGridSpec(
        num_scalar_prefetch=2,
        grid=(n_rows // bm,),
        in_specs=[pl.BlockSpec((bm * ROW_CHUNKS, LANES), lambda i, be, bv: (i, 0)),
                  pl.BlockSpec((None, None, D_MODEL, 2 * D_EXPERT), lambda i, be, bv: (layer, be[i], 0, 0)),
                  pl.BlockSpec((None, None, 1, 2 * D_EXPERT), lambda i, be, bv: (layer, be[i], 0, 0)),
                  pl.BlockSpec((None, None, D_EXPERT, D_MODEL), lambda i, be, bv: (layer, be[i], 0, 0)),
                  pl.BlockSpec((None, None, 1, D_MODEL), lambda i, be, bv: (layer, be[i], 0, 0))],
        out_specs=pl.BlockSpec((bm * ROW_CHUNKS, LANES), lambda i, be, bv: (i, 0)),
        scratch_shapes=[pltpu.VMEM((D_MODEL, 2 * D_EXPERT), BF16), pltpu.VMEM((D_EXPERT, D_MODEL), BF16)],
    )
    return pl.pallas_call(
        _expert_kernel,
        out_shape=jax.ShapeDtypeStruct((n_rows * ROW_CHUNKS, LANES), F32),
        grid_spec=gs,
        compiler_params=_cparams(("arbitrary",)),
        name="moe_experts",
    )(block_exp, block_valid, xb, w_gate_up, bgu, w_down, bd)


def _combine_kernel(dest_ref, dnext_ref, gate_ref, x_ref, mod_ref, lw_ref, lb_ref, yb_ref, o_ref, ybuf, sem, *,
                    n_ctx_tiles, alpha):
    i = pl.program_id(0)
    tm = x_ref.shape[0]
    slot = i % 2

    def gather(d_ref, s):
        def start(t, c):
            dst = pl.multiple_of(t * ROW_CHUNKS, ROW_CHUNKS)
            for k in range(TOP_K):
                src = pl.multiple_of(d_ref[t * TOP_K + k] * ROW_CHUNKS, ROW_CHUNKS)
                pltpu.make_async_copy(yb_ref.at[pl.ds(src, ROW_CHUNKS)], ybuf.at[s, k, pl.ds(dst, ROW_CHUNKS)],
                                      sem.at[s]).start(priority=k % 2)
            return c

        lax.fori_loop(0, tm, start, 0, unroll=DMA_UNROLL)

    @pl.when(i == 0)
    def _():
        gather(dest_ref, 0)

    @pl.when(i + 1 < pl.num_programs(0))
    def _():
        gather(dnext_ref, 1 - slot)

    for k in range(TOP_K):
        pltpu.make_async_copy(yb_ref.at[pl.ds(0, tm * ROW_CHUNKS)], ybuf.at[slot, k], sem.at[slot]).wait()
    gates = gate_ref[...]
    y = gates[:, 0:1] * _from_chunk_rows(ybuf.at[slot, 0], tm)
    for k in range(1, TOP_K):
        y = y + gates[:, k:k + 1] * _from_chunk_rows(ybuf.at[slot, k], tm)
    is_ctx = i < n_ctx_tiles
    g2 = _mod_rows(mod_ref, 5, is_ctx)
    o_ref[...] = _layer_norm(alpha * x_ref[...] + g2 * y, lw_ref[...], lb_ref[...])


def _combine(dest_flat, gates, xs, mod, ln_w, ln_b, yb, n_ctx_tiles, alpha):
    t = xs.shape[0]
    tm = TOK_TILE
    n_tiles = t // tm
    lw2, lb2 = ln_w.reshape(1, -1), ln_b.reshape(1, -1)
    full = lambda a: pl.BlockSpec(a.shape, lambda i: (0,) * a.ndim)
    return pl.pallas_call(
        functools.partial(_combine_kernel, n_ctx_tiles=n_ctx_tiles, alpha=alpha),
        out_shape=jax.ShapeDtypeStruct((t, D_MODEL), F32),
        grid=(n_tiles,),
        in_specs=[pl.BlockSpec((tm * TOP_K,), lambda i: (i,), memory_space=pltpu.SMEM),
                  pl.BlockSpec((tm * TOP_K,), lambda i: (jnp.minimum(i + 1, n_tiles - 1),), memory_space=pltpu.SMEM),
                  pl.BlockSpec((tm, TOP_K), lambda i: (i, 0)),
                  pl.BlockSpec((tm, D_MODEL), lambda i: (i, 0)),
                  full(mod), full(lw2), full(lb2),
                  pl.BlockSpec(memory_space=pl.ANY)],
        out_specs=pl.BlockSpec((tm, D_MODEL), lambda i: (i, 0)),
        scratch_shapes=[pltpu.VMEM((2, TOP_K, tm * ROW_CHUNKS, LANES), F32), pltpu.SemaphoreType.DMA((2,))],
        compiler_params=_cparams(("arbitrary",)),
        name="moe_combine",
    )(dest_flat, dest_flat, gates, xs, mod, lw2, lb2, yb)


def _moe(xs, mod, layer, w_router, b_router, w_gate_up, b_gate_up, w_down, b_down, ln_w, ln_b, n_ctx_tiles, alpha):
    t = xs.shape[0]
    bm = MOE_BM
    idx, gates, rank, cnt = _router(xs, mod, w_router, b_router, n_ctx_tiles)
    counts = cnt[:, 0]
    idx, gates, rank = idx[:TOP_K].T, gates[:TOP_K].T, rank[:TOP_K].T
    padded = (counts + bm - 1) // bm * bm
    pad_end = jnp.cumsum(padded).astype(jnp.int32)
    pad_start = pad_end - padded
    dest = pad_start[idx] + rank
    dest_flat = dest.reshape(-1).astype(jnp.int32)
    n_blocks = -(-(t * TOP_K) // bm) + N_EXPERTS
    blk0 = jnp.arange(n_blocks, dtype=jnp.int32) * bm
    block_exp = jnp.minimum(jnp.sum(pad_end[None, :] <= blk0[:, None], axis=1), N_EXPERTS - 1).astype(jnp.int32)
    block_valid = (blk0 < pad_end[-1]).astype(jnp.int32)
    xb = _dispatch(pad_start, pad_end, dest_flat, xs, mod, n_blocks * bm, n_ctx_tiles)
    yb = _experts(block_exp, block_valid, xb, layer, w_gate_up, b_gate_up, w_down, b_down)
    return _combine(dest_flat, gates, xs, mod, ln_w, ln_b, yb, n_ctx_tiles, alpha)


def kernel(x, c, ctx, c_ctx, w_ada, b_ada, w_in, mlstm_gate_bias, diff_lambda, diff_norm_w, mlstm_norm_w, w_out,
           ln1_w, ln1_b, w_router, b_router, w_gate_up, b_gate_up, w_down, b_down, ln2_w, ln2_b):
    assert x.shape[0] == 1 and ctx.shape[0] == 1
    depth = w_ada.shape[0]
    n_lat, n_ctx = x.shape[1], ctx.shape[1]
    assert n_ctx % TOK_TILE == 0 and n_lat % TOK_TILE == 0
    nct = n_ctx // TOK_TILE
    alpha = (2 * depth) ** 0.25

    vecs = jnp.zeros((8, D_MODEL), F32).at[0].set(c[0]).at[1].set(c_ctx)
    mods = _ada(vecs, w_ada, b_ada)
    cos, sin = _rope_tables(n_lat, n_ctx)
    xc, xl, lat_row0 = ctx[0], x[0], 0

    for l in range(depth):
        need_ctx = l < depth - 1
        lam_init = 0.8 - 0.6 * math.exp(-0.3 * l)
        mod = mods[l]
        w_ext, w_vt = _prep_w_in(w_in[l])
        q, k, vt, mkt, mq, mk, mv, mo, g, cu = _inproj(xc, xl, lat_row0, n_ctx + n_lat, mod, w_ext, w_vt, cos, sin,
                                                         mlstm_gate_bias[l], n_ctx)

        dl = diff_lambda[l].astype(F32)
        lam = (jnp.exp(jnp.sum(dl[0] * dl[1])) - jnp.exp(jnp.sum(dl[2] * dl[3])) + lam_init).reshape(1)
        post = 1.0 - lam_init
        attn_l = _attention(lam, q, k, vt, diff_norm_w[l], n_ctx, n_lat, n_ctx + n_lat, post)
        four_l = _fourier_lat(cu[n_ctx:])
        if need_ctx:
            attn_c = _attention(lam, q, k, vt, diff_norm_w[l], 0, n_ctx, n_ctx, post)
            four_c = _fourier_small(cu[:n_ctx])
        else:
            attn_c, four_c = attn_l, four_l
        hf, hb = _mlstm(mq, mk, mkt, mv, g, n_ctx)
        xs = _outproj(attn_l, attn_c, four_l, four_c, hf, hb, mo, xc, xl, lat_row0, mod, mlstm_norm_w[l], w_out[l].astype(BF16),
                      ln1_w[l], ln1_b[l], n_ctx, need_ctx, alpha)
        xt = _moe(xs, mod, l, w_router[l], b_router[l], w_gate_up, b_gate_up, w_down, b_down, ln2_w[l], ln2_b[l],
                  nct if need_ctx else 0, alpha)
        xc, xl, lat_row0 = xt, xt, n_ctx
    return xt[None]
```
